```python
import jax, jax.numpy as jnp
from jax import lax
import numpy as np

D_MODEL = 4096
BATCH = 4
SEQ = 4096
DEPTH = 1

CHUNK = 64
CONV_WIDTH = 3
CONV_DIM = D_MODEL // 2
CONV_GROUPS = 16
SG_DIM = D_MODEL - CONV_DIM
SG_HEADS = 16
SG_HEAD_DIM = SG_DIM // SG_HEADS
SG_BLOCK = 128
N_IN = 3 * CONV_DIM + 2 * SG_DIM
N_GROUPS = 8
EXPERTS_PER_GROUP = 8
N_EXPERTS = N_GROUPS * EXPERTS_PER_GROUP
TOP_K = 2
D_EXPERT = 768
PLE_DIM = 256
MOE_ROW_BLOCK = 128
ALPHA = (2 * DEPTH) ** 0.25
BETA = (8 * DEPTH) ** -0.25
LN_EPS = 1e-5

kernel_name = "hybrid_conv_sgu_hmoe_deepnorm"


def layer_norm(x, g, b):
    xf = x.astype(jnp.float32)
    mu = jnp.mean(xf, axis=-1, keepdims=True)
    xc = xf - mu
    var = jnp.mean(xc * xc, axis=-1, keepdims=True)
    y = xc * lax.rsqrt(var + LN_EPS) * g.astype(jnp.float32) + b.astype(jnp.float32)
    return y.astype(x.dtype)


def causal_depthwise_conv(z, w):
    s = z.shape[1]
    zp = jnp.pad(z, ((0, 0), (CONV_WIDTH - 1, 0), (0, 0)))
    y = w[0] * zp[:, 0:s]
    for k in range(1, CONV_WIDTH):
        y = y + w[k] * zp[:, k:k + s]
    return y


def spatial_gating(u, v, ln_g, ln_b, w_s, b_s):
    bsz, s, _ = v.shape
    vh = v.reshape(bsz, s, SG_HEADS, SG_HEAD_DIM)
    vh = layer_norm(vh, ln_g, ln_b)
    vb = vh.reshape(bsz, s // SG_BLOCK, SG_BLOCK, SG_HEADS, SG_HEAD_DIM)
    pos = jnp.arange(SG_BLOCK)
    mask = (pos[None, :] // CHUNK) <= (pos[:, None] // CHUNK)
    ws = jnp.where(mask[None], w_s, jnp.zeros((), w_s.dtype))
    sg = jnp.einsum('hij,bnjhd->bnihd', ws, vb) + jnp.transpose(b_s)[None, None, :, :, None]
    return u * sg.reshape(bsz, s, SG_DIM)


def token_mixer(x, w_in, conv_w, sg_ln_g, sg_ln_b, sg_w, sg_b, w_out):
    proj = x @ w_in
    b_a, c_a, h_a, u, v = jnp.split(
        proj, [CONV_DIM, 2 * CONV_DIM, 3 * CONV_DIM, 3 * CONV_DIM + SG_DIM], axis=-1)
    y_a = b_a * causal_depthwise_conv(c_a * h_a, conv_w)
    y_b = spatial_gating(jax.nn.gelu(u), jax.nn.gelu(v), sg_ln_g, sg_ln_b, sg_w, sg_b)
    return jnp.concatenate([y_a, y_b], axis=-1) @ w_out


def hierarchical_moe(x, w_rg, b_rg, w_re, b_re, w_gate, w_up, w_down):
    bsz, s, d = x.shape
    t = bsz * s
    xf = x.reshape(t, d)
    pg = jax.nn.softmax((xf @ w_rg + b_rg).astype(jnp.float32), axis=-1)
    gp, gi = lax.top_k(pg, 1)
    le = (xf @ w_re + b_re).astype(jnp.float32).reshape(t, N_GROUPS, EXPERTS_PER_GROUP)
    le_sel = jnp.take_along_axis(le, gi[:, :, None], axis=1)[:, 0]
    ev, ei = lax.top_k(le_sel, TOP_K)
    ew = jax.nn.softmax(ev, axis=-1) * gp
    expert_id = (gi * EXPERTS_PER_GROUP + ei).reshape(-1).astype(jnp.int32)
    weight = ew.reshape(-1).astype(x.dtype)
    m = t * TOP_K
    token_id = jnp.arange(m, dtype=jnp.int32) // TOP_K
    n_blocks = (m + N_EXPERTS * (MOE_ROW_BLOCK - 1)) // MOE_ROW_BLOCK
    n_rows = n_blocks * MOE_ROW_BLOCK
    counts = jnp.bincount(expert_id, length=N_EXPERTS)
    padded = ((counts + MOE_ROW_BLOCK - 1) // MOE_ROW_BLOCK) * MOE_ROW_BLOCK
    pad_end = jnp.cumsum(padded)
    pad_start = pad_end - padded
    start = jnp.cumsum(counts) - counts
    order = jnp.argsort(expert_id)
    se = expert_id[order]
    dest = pad_start[se] + (jnp.arange(m, dtype=jnp.int32) - start[se])
    row_tok = jnp.full((n_rows,), t, jnp.int32).at[dest].set(token_id[order])
    row_w = jnp.zeros((n_rows,), x.dtype).at[dest].set(weight[order])
    block_start = jnp.arange(n_blocks, dtype=jnp.int32) * MOE_ROW_BLOCK
    block_e = jnp.minimum(jnp.searchsorted(pad_end, block_start, side='right'), N_EXPERTS - 1)
    x_pad = jnp.concatenate([xf, jnp.zeros((1, d), x.dtype)], axis=0)

    def expert_block(args):
        e, tok, w = args
        xb = x_pad[tok]
        hb = jax.nn.silu(xb @ w_gate[e]) * (xb @ w_up[e])
        return (hb @ w_down[e]) * w[:, None]

    y = lax.map(expert_block, (block_e, row_tok.reshape(n_blocks, MOE_ROW_BLOCK),
                               row_w.reshape(n_blocks, MOE_ROW_BLOCK)))
    out = jnp.zeros((t + 1, d), x.dtype).at[row_tok].add(y.reshape(n_rows, d))[:t]
    return out.reshape(bsz, s, d)


def setup_inputs(seed: int = 0) -> dict:
    key = jax.random.key(seed)
    ks = jax.random.split(key, 24)
    f32 = jnp.float32
    nrm = lambda k, shape, scale: jax.random.normal(k, shape, f32) * scale
    return {
        "x": nrm(ks[0], (BATCH, SEQ, D_MODEL), 1.0),
        "p": nrm(ks[1], (DEPTH, BATCH, SEQ, PLE_DIM), 1.0),
        "w_in": nrm(ks[2], (DEPTH, D_MODEL, N_IN), D_MODEL ** -0.5),
        "conv_w": nrm(ks[3], (DEPTH, CONV_WIDTH, CONV_DIM), CONV_WIDTH ** -0.5),
        "sg_ln_g": 1.0 + nrm(ks[4], (DEPTH, SG_HEADS, SG_HEAD_DIM), 0.02),
        "sg_ln_b": nrm(ks[5], (DEPTH, SG_HEADS, SG_HEAD_DIM), 0.02),
        "sg_w": nrm(ks[6], (DEPTH, SG_HEADS, SG_BLOCK, SG_BLOCK), SG_BLOCK ** -0.5),
        "sg_b": 1.0 + nrm(ks[7], (DEPTH, SG_HEADS, SG_BLOCK), 0.02),
        "w_out": nrm(ks[8], (DEPTH, CONV_DIM + SG_DIM, D_MODEL), BETA * D_MODEL ** -0.5),
        "ln1_g": 1.0 + nrm(ks[9], (DEPTH, D_MODEL), 0.02),
        "ln1_b": nrm(ks[10], (DEPTH, D_MODEL), 0.02),
        "w_rg": nrm(ks[11], (DEPTH, D_MODEL, N_GROUPS), D_MODEL ** -0.5),
        "b_rg": nrm(ks[12], (DEPTH, N_GROUPS), 0.01),
        "w_re": nrm(ks[13], (DEPTH, D_MODEL, N_EXPERTS), D_MODEL ** -0.5),
        "b_re": nrm(ks[14], (DEPTH, N_EXPERTS), 0.01),
        "w_gate": nrm(ks[15], (DEPTH, N_EXPERTS, D_MODEL, D_EXPERT), D_MODEL ** -0.5),
        "w_up": nrm(ks[16], (DEPTH, N_EXPERTS, D_MODEL, D_EXPERT), D_MODEL ** -0.5),
        "w_down": nrm(ks[17], (DEPTH, N_EXPERTS, D_EXPERT, D_MODEL), BETA * D_EXPERT ** -0.5),
        "ln2_g": 1.0 + nrm(ks[18], (DEPTH, D_MODEL), 0.02),
        "ln2_b": nrm(ks[19], (DEPTH, D_MODEL), 0.02),
        "w_pg": nrm(ks[20], (DEPTH, D_MODEL, D_MODEL), D_MODEL ** -0.5),
        "b_pg": nrm(ks[21], (DEPTH, D_MODEL), 0.02),
        "w_pe": nrm(ks[22], (DEPTH, PLE_DIM, D_MODEL), PLE_DIM ** -0.5),
    }


def reference(x, p, w_in, conv_w, sg_ln_g, sg_ln_b, sg_w, sg_b, w_out, ln1_g, ln1_b,
              w_rg, b_rg, w_re, b_re, w_gate, w_up, w_down, ln2_g, ln2_b,
              w_pg, b_pg, w_pe):
    h = x
    for i in range(DEPTH):
        mix = token_mixer(h, w_in[i], conv_w[i], sg_ln_g[i], sg_ln_b[i], sg_w[i], sg_b[i], w_out[i])
        h = layer_norm(ALPHA * h + mix, ln1_g[i], ln1_b[i])
        ffn = hierarchical_moe(h, w_rg[i], b_rg[i], w_re[i], b_re[i], w_gate[i], w_up[i], w_down[i])
        h = layer_norm(ALPHA * h + ffn, ln2_g[i], ln2_b[i])
        gate = jax.nn.sigmoid((h @ w_pg[i] + b_pg[i]).astype(jnp.float32)).astype(h.dtype)
        h = h + gate * (p[i] @ w_pe[i])
    return h
```

```python
import functools

import jax
import jax.numpy as jnp
from jax import lax
from jax.experimental import pallas as pl
from jax.experimental.pallas import tpu as pltpu

F32 = jnp.float32
BF16 = jnp.bfloat16

LANE = 128
SUBLANE = 8
MXU_COL = 256
VMEM_LIMIT = 56 * 1024 * 1024
MOE_VMEM_LIMIT = 60 * 1024 * 1024

CHUNK = 64
SG_BLOCK = 128
LN_EPS = 1e-5
TOP_K = 2

MIX_TM = 512
MIX_TN = MXU_COL
OUT_TN = 512
ROUTE_TM = 512
MOE_SUB = 128
MOE_ROWS = 512
MOE_TF = MXU_COL
CMB_TM = 256
GATE_TM = 1024

_dot = functools.partial(jnp.dot, preferred_element_type=F32)


def _params(*sem, vmem=VMEM_LIMIT):
    return pltpu.CompilerParams(dimension_semantics=sem, vmem_limit_bytes=vmem)


def _layer_norm(z, g, b):
    mu = jnp.mean(z, axis=-1, keepdims=True)
    zc = z - mu
    var = jnp.mean(zc * zc, axis=-1, keepdims=True)
    return zc * lax.rsqrt(var + LN_EPS) * g + b


def _mixer_kernel(x_ref, wb_ref, wc_ref, wh_ref, wu_ref, wv_ref, cw_ref, lng_ref, lnb_ref,
                  sgw_ref, sgb_ref, ya_ref, yb_ref, xb_ref, gbuf_ref, carry_ref, *, tm, tiles_per_seq):
    i = pl.program_id(0)
    j = pl.program_id(1)

    @pl.when(j == 0)
    def _():
        xb_ref[...] = x_ref[...].astype(BF16)

    xv = xb_ref[...]
    g = _dot(xv, wc_ref[...]) * _dot(xv, wh_ref[...])
    first = (i % tiles_per_seq) == 0

    @pl.when(first)
    def _():
        gbuf_ref[0:SUBLANE, :] = jnp.zeros((SUBLANE, g.shape[1]), F32)

    @pl.when(jnp.logical_not(first))
    def _():
        gbuf_ref[0:SUBLANE, :] = carry_ref[j]

    gbuf_ref[SUBLANE:, :] = g
    carry_ref[j] = g[tm - SUBLANE:, :]
    g1 = gbuf_ref[pl.ds(SUBLANE - 1, tm), :]
    g2 = gbuf_ref[pl.ds(SUBLANE - 2, tm), :]
    cw = cw_ref[...]
    conv = cw[0:1, :] * g2 + cw[1:2, :] * g1 + cw[2:3, :] * g
    ya_ref[...] = (_dot(xv, wb_ref[...]) * conv).astype(ya_ref.dtype)

    u = _dot(xv, wu_ref[...])
    v = _dot(xv, wv_ref[...])
    pos_i = lax.broadcasted_iota(jnp.int32, (SG_BLOCK, SG_BLOCK), 0)
    pos_j = lax.broadcasted_iota(jnp.int32, (SG_BLOCK, SG_BLOCK), 1)
    mask = (pos_j // CHUNK) <= (pos_i // CHUNK)
    for hh in range(u.shape[1] // LANE):
        sl = slice(hh * LANE, (hh + 1) * LANE)
        vn = _layer_norm(jax.nn.gelu(v[:, sl]), lng_ref[:, sl], lnb_ref[:, sl]).astype(BF16)
        gu = jax.nn.gelu(u[:, sl])
        ws = jnp.where(mask, sgw_ref[hh], 0.0).astype(BF16)
        bcol = sgb_ref[hh]
        for n in range(tm // SG_BLOCK):
            rows = slice(n * SG_BLOCK, (n + 1) * SG_BLOCK)
            sg = _dot(ws, vn[rows, :]) + bcol
            yb_ref[rows, sl] = (gu[rows, :] * sg).astype(yb_ref.dtype)


def _mixer(x2, w_in_b, conv_w, lng, lnb, sg_w, sg_bcol, seq):
    t, d = x2.shape
    cdim = conv_w.shape[1]
    nj = cdim // MIX_TN
    tm = MIX_TM
    assert seq % tm == 0 and t % tm == 0 and cdim % MIX_TN == 0 and w_in_b.shape[1] == 5 * cdim
    hp = MIX_TN // LANE
    wspec = lambda s: pl.BlockSpec((d, MIX_TN), lambda i, j, s=s: (0, s * nj + j))
    return pl.pallas_call(
        functools.partial(_mixer_kernel, tm=tm, tiles_per_seq=seq // tm),
        grid=(t // tm, nj),
        in_specs=[
            pl.BlockSpec((tm, d), lambda i, j: (i, 0)),
            wspec(0), wspec(1), wspec(2), wspec(3), wspec(4),
            pl.BlockSpec((conv_w.shape[0], MIX_TN), lambda i, j: (0, j)),
            pl.BlockSpec((1, MIX_TN), lambda i, j: (0, j)),
            pl.BlockSpec((1, MIX_TN), lambda i, j: (0, j)),
            pl.BlockSpec((hp, SG_BLOCK, SG_BLOCK), lambda i, j: (j, 0, 0)),
            pl.BlockSpec((hp, SG_BLOCK, 1), lambda i, j: (j, 0, 0)),
        ],
        out_specs=[pl.BlockSpec((tm, MIX_TN), lambda i, j: (i, j)),
                   pl.BlockSpec((tm, MIX_TN), lambda i, j: (i, j))],
        out_shape=[jax.ShapeDtypeStruct((t, cdim), BF16), jax.ShapeDtypeStruct((t, cdim), BF16)],
        scratch_shapes=[pltpu.VMEM((tm, d), BF16),
                        pltpu.VMEM((tm + SUBLANE, MIX_TN), F32),
                        pltpu.VMEM((nj, SUBLANE, MIX_TN), F32)],
        compiler_params=_params("arbitrary", "arbitrary"),
        name="mixer",
    )(x2, w_in_b, w_in_b, w_in_b, w_in_b, w_in_b, conv_w, lng, lnb, sg_w, sg_bcol)


def _outproj_kernel(ya_ref, yb_ref, wa_ref, wb_ref, x_ref, g_ref, b_ref, wr_ref, br_ref,
                    h1_ref, lg_ref, *, tn, nj, alpha):
    j = pl.program_id(1)
    z = alpha * x_ref[...] + (_dot(ya_ref[...], wa_ref[...]) + _dot(yb_ref[...], wb_ref[...]))
    for jj in range(nj):
        @pl.when(j == jj)
        def _(jj=jj):
            h1_ref[:, jj * tn:(jj + 1) * tn] = z

    @pl.when(j == nj - 1)
    def _():
        h = _layer_norm(h1_ref[...], g_ref[...], b_ref[...])
        h1_ref[...] = h
        wr = wr_ref[...]
        wr_hi = wr.astype(BF16)
        wr_lo = (wr - wr_hi.astype(F32)).astype(BF16)
        h_hi = h.astype(BF16)
        h_lo = (h - h_hi.astype(F32)).astype(BF16)
        lg_ref[...] = (_dot(h_hi, wr_hi) + _dot(h_lo, wr_hi) + _dot(h_hi, wr_lo)) + br_ref[...]


def _outproj(ya, yb, w_out_b, x2, g, b, wr, br, alpha):
    t, d = x2.shape
    cdim = ya.shape[1]
    tm, tn = MIX_TM, OUT_TN
    nj = d // tn
    assert t % tm == 0 and d % tn == 0 and w_out_b.shape[0] == 2 * cdim
    return pl.pallas_call(
        functools.partial(_outproj_kernel, tn=tn, nj=nj, alpha=alpha),
        grid=(t // tm, nj),
        in_specs=[
            pl.BlockSpec((tm, cdim), lambda i, j: (i, 0)),
            pl.BlockSpec((tm, cdim), lambda i, j: (i, 0)),
            pl.BlockSpec((cdim, tn), lambda i, j: (0, j)),
            pl.BlockSpec((cdim, tn), lambda i, j: (1, j)),
            pl.BlockSpec((tm, tn), lambda i, j: (i, j)),
            pl.BlockSpec((1, d), lambda i, j: (0, 0)),
            pl.BlockSpec((1, d), lambda i, j: (0, 0)),
            pl.BlockSpec((d, LANE), lambda i, j: (0, 0)),
            pl.BlockSpec((1, LANE), lambda i, j: (0, 0)),
        ],
        out_specs=[pl.BlockSpec((tm, d), lambda i, j: (i, 0)),
                   pl.BlockSpec((tm, LANE), lambda i, j: (i, 0))],
        out_shape=[jax.ShapeDtypeStruct((t, d), F32), jax.ShapeDtypeStruct((t, LANE), F32)],
        compiler_params=_params("arbitrary", "arbitrary"),
        name="outproj",
    )(ya, yb, w_out_b, w_out_b, x2, g, b, wr, br)


def _route_kernel(lg_ref, pk_ref, cnt_ref, run_ref, *, n_groups, per_group):
    i = pl.program_id(0)

    @pl.when(i == 0)
    def _():
        run_ref[...] = jnp.zeros_like(run_ref)

    l = lg_ref[...]
    tm = l.shape[0]
    lane = lax.broadcasted_iota(jnp.int32, l.shape, 1)
    neg = jnp.float32(-jnp.inf)
    gmask = lane < n_groups
    gl = jnp.where(gmask, l, neg)
    gmax = jnp.max(gl, axis=-1, keepdims=True)
    gi = jnp.min(jnp.where(gl == gmax, lane, LANE), axis=-1, keepdims=True)
    gp = 1.0 / jnp.sum(jnp.where(gmask, jnp.exp(l - gmax), 0.0), axis=-1, keepdims=True)
    elane = lane - n_groups
    emask = (elane >= 0) & (elane // per_group == gi)
    el = jnp.where(emask, l, neg)
    m1 = jnp.max(el, axis=-1, keepdims=True)
    i1 = jnp.min(jnp.where(el == m1, lane, LANE), axis=-1, keepdims=True)
    el2 = jnp.where(lane == i1, neg, el)
    m2 = jnp.max(el2, axis=-1, keepdims=True)
    i2 = jnp.min(jnp.where(el2 == m2, lane, LANE), axis=-1, keepdims=True)
    t2 = jnp.exp(m2 - m1)
    w0 = gp / (1.0 + t2)
    w1 = gp * t2 / (1.0 + t2)
    e0 = i1 - n_groups
    e1 = i2 - n_groups
    oh0 = lane == e0
    oh1 = lane == e1
    oh = (oh0 | oh1).astype(BF16)
    row = lax.broadcasted_iota(jnp.int32, (tm, tm), 0)
    col = lax.broadcasted_iota(jnp.int32, (tm, tm), 1)
    tri = (row > col).astype(BF16)
    before = run_ref[...] + _dot(tri, oh)
    r0 = jnp.sum(jnp.where(oh0, before, 0.0), axis=-1, keepdims=True)
    r1 = jnp.sum(jnp.where(oh1, before, 0.0), axis=-1, keepdims=True)
    total = run_ref[...] + jnp.sum(oh.astype(F32), axis=0, keepdims=True)
    run_ref[...] = total
    cnt_ref[...] = total
    vals = (e0.astype(F32), e1.astype(F32), w0, w1, r0, r1)
    pk = jnp.zeros(l.shape, F32)
    for k, val in enumerate(vals):
        pk = jnp.where(lane == k, val, pk)
    pk_ref[...] = pk


def _route(logits, n_groups, per_group):
    t = logits.shape[0]
    tm = ROUTE_TM
    assert t % tm == 0 and n_groups * (1 + per_group) <= LANE
    return pl.pallas_call(
        functools.partial(_route_kernel, n_groups=n_groups, per_group=per_group),
        grid=(t // tm,),
        in_specs=[pl.BlockSpec((tm, LANE), lambda i: (i, 0))],
        out_specs=[pl.BlockSpec((tm, LANE), lambda i: (i, 0)),
                   pl.BlockSpec((1, LANE), lambda i: (0, 0))],
        out_shape=[jax.ShapeDtypeStruct((t, LANE), F32), jax.ShapeDtypeStruct((1, LANE), F32)],
        scratch_shapes=[pltpu.VMEM((1, LANE), F32)],
        compiler_params=_params("arbitrary"),
        name="route",
    )(logits)


def _rowmap_kernel(eid_ref, rank_ref, start_ref, asg_ref, *, n_rows, n_asg):
    def fill(r, c):
        asg_ref[r] = 0
        return c

    lax.fori_loop(0, n_rows, fill, 0)

    def place(a, c):
        asg_ref[start_ref[eid_ref[a]] + rank_ref[a]] = a
        return c

    lax.fori_loop(0, n_asg, place, 0)


def _rowmap(eid, rank, start, n_rows):
    n_asg = eid.shape[0]
    smem = pl.BlockSpec(memory_space=pltpu.SMEM)
    return pl.pallas_call(
        functools.partial(_rowmap_kernel, n_rows=n_rows, n_asg=n_asg),
        in_specs=[smem, smem, smem],
        out_specs=smem,
        out_shape=jax.ShapeDtypeStruct((n_rows,), jnp.int32),
        name="rowmap",
    )(eid, rank, start)


def _moe_kernel(be_ref, row0_ref, nsub_ref, nreal_ref, asg_ref, h1_hbm, wg_ref, wu_ref, wd_ref, y_hbm,
                xg_ref, xb_ref, yacc_ref, gsem, ssem, *, nb, nf, sub, tok_mask):
    b = pl.program_id(0)
    f = pl.program_id(1)
    nsub = nsub_ref[b]

    def gather_start(blk):
        base = row0_ref[blk]

        def body(r, c):
            tok = asg_ref[base + r] & tok_mask
            pltpu.make_async_copy(h1_hbm.at[pl.ds(tok, 1)], xg_ref.at[pl.ds(r, 1)], gsem).start()
            return c

        lax.fori_loop(0, nsub_ref[blk] * sub, body, 0)

    def gather_wait(blk):
        def body(s, c):
            pltpu.make_async_copy(h1_hbm.at[pl.ds(0, sub)], xg_ref.at[pl.ds(s * sub, sub)], gsem).wait()
            return c

        lax.fori_loop(0, nsub_ref[blk], body, 0)

    def scatter_start(blk):
        base = row0_ref[blk]

        def body(r, c):
            dst = asg_ref[base + r]
            pltpu.make_async_copy(yacc_ref.at[pl.ds(r, 1)], y_hbm.at[pl.ds(dst, 1)], ssem).start()
            return c

        lax.fori_loop(0, nreal_ref[blk], body, 0)

    def scatter_wait(blk):
        full = nreal_ref[blk] // sub

        def tile(s, c):
            pltpu.make_async_copy(yacc_ref.at[pl.ds(s * sub, sub)], y_hbm.at[pl.ds(0, sub)], ssem).wait()
            return c

        lax.fori_loop(0, full, tile, 0)

        def row(r, c):
            pltpu.make_async_copy(yacc_ref.at[pl.ds(0, 1)], y_hbm.at[pl.ds(0, 1)], ssem).wait()
            return c

        lax.fori_loop(full * sub, nreal_ref[blk], row, 0)

    @pl.when((f == 0) & (b > 0))
    def _():
        scatter_wait(jnp.maximum(b - 1, 0))

    @pl.when(nsub > 0)
    def _():
        @pl.when(f == 0)
        def _():
            @pl.when(b == 0)
            def _():
                gather_start(0)

            gather_wait(b)

            def cast(s, c):
                rows = pl.ds(pl.multiple_of(s * sub, sub), sub)
                xb_ref[rows, :] = xg_ref[rows, :].astype(BF16)
                return c

            lax.fori_loop(0, nsub, cast, 0)

            @pl.when(b + 1 < nb)
            def _():
                gather_start(jnp.minimum(b + 1, nb - 1))

        wg = wg_ref[...].astype(BF16)
        wu = wu_ref[...].astype(BF16)
        wd = wd_ref[...].astype(BF16)

        def body(s, c):
            rows = pl.ds(pl.multiple_of(s * sub, sub), sub)
            xs = xb_ref[rows, :]
            hb = (jax.nn.silu(_dot(xs, wg)) * _dot(xs, wu)).astype(BF16)
            yp = _dot(hb, wd)

            @pl.when(f == 0)
            def _():
                yacc_ref[rows, :] = yp

            @pl.when(f > 0)
            def _():
                yacc_ref[rows, :] += yp

            return c

        lax.fori_loop(0, nsub, body, 0)

        @pl.when(f == nf - 1)
        def _():
            scatter_start(b)

            @pl.when(b == nb - 1)
            def _():
                scatter_wait(b)


def _moe(be, row0, nsub, nreal, asg, h1, w_gate, w_up, w_down, n_out_rows):
    t, d = h1.shape
    ne, _, de = w_gate.shape
    nb = be.shape[0]
    nf = de // MOE_TF
    assert de % MOE_TF == 0 and t & (t - 1) == 0
    last = nf - 1
    fsel = lambda f, n: jnp.where(n > 0, f, last)
    grid_spec = pltpu.PrefetchScalarGridSpec(
        num_scalar_prefetch=5,
        grid=(nb, nf),
        in_specs=[
            pl.BlockSpec(memory_space=pl.ANY),
            pl.BlockSpec((None, d, MOE_TF), lambda b, f, be, r0, ns, nr, asg: (be[b], 0, fsel(f, ns[b]))),
            pl.BlockSpec((None, d, MOE_TF), lambda b, f, be, r0, ns, nr, asg: (be[b], 0, fsel(f, ns[b]))),
            pl.BlockSpec((None, MOE_TF, d), lambda b, f, be, r0, ns, nr, asg: (be[b], fsel(f, ns[b]), 0)),
        ],
        out_specs=pl.BlockSpec(memory_space=pl.ANY),
        scratch_shapes=[pltpu.VMEM((MOE_ROWS, d), F32),
                        pltpu.VMEM((MOE_ROWS, d), BF16),
                        pltpu.VMEM((MOE_ROWS, d), F32),
                        pltpu.SemaphoreType.DMA,
                        pltpu.SemaphoreType.DMA],
    )
    return pl.pallas_call(
        functools.partial(_moe_kernel, nb=nb, nf=nf, sub=MOE_SUB, tok_mask=t - 1),
        grid_spec=grid_spec,
        out_shape=jax.ShapeDtypeStruct((n_out_rows, d), F32),
        compiler_params=_params("arbitrary", "arbitrary", vmem=MOE_VMEM_LIMIT),
        name="moe",
    )(be, row0, nsub, nreal, asg, h1, w_gate, w_up, w_down)


def _combine_kernel(h1_ref, y0_ref, y1_ref, pk_ref, g_ref, b_ref, h2_ref, h2b_ref, *, alpha):
    pk = pk_ref[...]
    ffn = pk[:, 2:3] * y0_ref[...] + pk[:, 3:4] * y1_ref[...]
    h = _layer_norm(alpha * h1_ref[...] + ffn, g_ref[...], b_ref[...])
    h2_ref[...] = h
    h2b_ref[...] = h.astype(BF16)


def _combine(h1, y2, pk, g, b, alpha):
    t, d = h1.shape
    tm = CMB_TM
    assert t % tm == 0
    nt = t // tm
    return pl.pallas_call(
        functools.partial(_combine_kernel, alpha=alpha),
        grid=(nt,),
        in_specs=[
            pl.BlockSpec((tm, d), lambda i: (i, 0)),
            pl.BlockSpec((tm, d), lambda i: (i, 0)),
            pl.BlockSpec((tm, d), lambda i: (i + nt, 0)),
            pl.BlockSpec((tm, LANE), lambda i: (i, 0)),
            pl.BlockSpec((1, d), lambda i: (0, 0)),
            pl.BlockSpec((1, d), lambda i: (0, 0)),
        ],
        out_specs=[pl.BlockSpec((tm, d), lambda i: (i, 0)), pl.BlockSpec((tm, d), lambda i: (i, 0))],
        out_shape=[jax.ShapeDtypeStruct((t, d), F32), jax.ShapeDtypeStruct((t, d), BF16)],
        compiler_params=_params("arbitrary"),
        name="combine",
    )(h1, y2, y2, pk, g, b)


def _gate_kernel(h2b_ref, wpg_ref, bpg_ref, h2_ref, p_ref, wpe_ref, o_ref):
    gate = jax.nn.sigmoid(_dot(h2b_ref[...], wpg_ref[...]) + bpg_ref[...])
    ple = _dot(p_ref[...].astype(BF16), wpe_ref[...])
    o_ref[...] = h2_ref[...] + gate * ple


def _gate(h2, h2b, w_pg_b, b_pg, p2, w_pe_b):
    t, d = h2.shape
    pd = p2.shape[1]
    tm, tn = GATE_TM, OUT_TN
    assert t % tm == 0 and d % tn == 0
    return pl.pallas_call(
        _gate_kernel,
        grid=(t // tm, d // tn),
        in_specs=[
            pl.BlockSpec((tm, d), lambda i, j: (i, 0)),
            pl.BlockSpec((d, tn), lambda i, j: (0, j)),
            pl.BlockSpec((1, tn), lambda i, j: (0, j)),
            pl.BlockSpec((tm, tn), lambda i, j: (i, j)),
            pl.BlockSpec((tm, pd), lambda i, j: (i, 0)),
            pl.BlockSpec((pd, tn), lambda i, j: (0, j)),
        ],
        out_specs=pl.BlockSpec((tm, tn), lambda i, j: (i, j)),
        out_shape=jax.ShapeDtypeStruct((t, d), F32),
        compiler_params=_params("arbitrary", "arbitrary"),
        name="gate",
    )(h2b, w_pg_b, b_pg, h2, p2, w_pe_b)


def _block_table(counts, n_asg):
    ne = counts.shape[0]
    padded = ((counts + MOE_SUB - 1) // MOE_SUB) * MOE_SUB
    pad_end = jnp.cumsum(padded)
    pad_start = pad_end - padded
    nblk = (padded + MOE_ROWS - 1) // MOE_ROWS
    blk_end = jnp.cumsum(nblk)
    blk_start = blk_end - nblk
    nb = ne + n_asg // MOE_ROWS
    bidx = jnp.arange(nb, dtype=jnp.int32)
    used = bidx < blk_end[-1]
    be = jnp.minimum(jnp.searchsorted(blk_end, jnp.minimum(bidx, blk_end[-1] - 1), side="right"), ne - 1)
    be = be.astype(jnp.int32)
    within = bidx - blk_start[be]
    row0 = pad_start[be] + within * MOE_ROWS
    nsub = jnp.clip((padded[be] - within * MOE_ROWS) // MOE_SUB, 0, MOE_ROWS // MOE_SUB)
    nsub = jnp.where(used, nsub, 0)
    row0 = jnp.where(used, row0, 0)
    nreal = jnp.where(used, jnp.clip(counts[be] - within * MOE_ROWS, 0, MOE_ROWS), 0)
    i32 = lambda a: a.astype(jnp.int32)
    return i32(pad_start), be, i32(row0), i32(nsub), i32(nreal)


def kernel(x, p, w_in, conv_w, sg_ln_g, sg_ln_b, sg_w, sg_b, w_out, ln1_g, ln1_b, w_rg, b_rg, w_re, b_re,
           w_gate, w_up, w_down, ln2_g, ln2_b, w_pg, b_pg, w_pe):
    depth = w_in.shape[0]
    bsz, seq, d = x.shape
    t = bsz * seq
    alpha = (2 * depth) ** 0.25
    n_groups = w_rg.shape[-1]
    ne = w_re.shape[-1]
    per_group = ne // n_groups
    n_asg = t * TOP_K
    n_rows = ((n_asg + ne * (MOE_SUB - 1) + MOE_SUB - 1) // MOE_SUB) * MOE_SUB

    h = x.reshape(t, d)
    for i in range(depth):
        ya, yb = _mixer(h, w_in[i].astype(BF16), conv_w[i], sg_ln_g[i].reshape(1, -1), sg_ln_b[i].reshape(1, -1),
                        sg_w[i], sg_b[i][:, :, None], seq)
        wr = jnp.concatenate([w_rg[i], w_re[i], jnp.zeros((d, LANE - n_groups - ne), F32)], axis=1)
        br = jnp.concatenate([b_rg[i], b_re[i], jnp.zeros((LANE - n_groups - ne,), F32)]).reshape(1, LANE)
        h1, logits = _outproj(ya, yb, w_out[i].astype(BF16), h, ln1_g[i].reshape(1, d), ln1_b[i].reshape(1, d),
                              wr, br, alpha)
        pk, cnt = _route(logits, n_groups, per_group)
        pad_start, be, row0, nsub, nreal = _block_table(cnt[0, :ne].astype(jnp.int32), n_asg)
        eid = jnp.concatenate([pk[:, 0], pk[:, 1]]).astype(jnp.int32)
        rank = jnp.concatenate([pk[:, 4], pk[:, 5]]).astype(jnp.int32)
        asg = _rowmap(eid, rank, pad_start, n_rows)
        y2 = _moe(be, row0, nsub, nreal, asg, h1, w_gate[i], w_up[i], w_down[i], n_asg)
        h2, h2b = _combine(h1, y2, pk, ln2_g[i].reshape(1, d), ln2_b[i].reshape(1, d), alpha)
        h = _gate(h2, h2b, w_pg[i].astype(BF16), b_pg[i].reshape(1, d), p[i].reshape(t, -1), w_pe[i].astype(BF16))
    return h.reshape(bsz, seq, d)
```

```python
import functools

import jax
import jax.numpy as jnp
from jax import lax
from jax.experimental import pallas as pl
from jax.experimental.pallas import tpu as pltpu

F32 = jnp.float32
BF16 = jnp.bfloat16

LANE = 128
SUBLANE = 8
MXU_COL = 256
VMEM_LIMIT = 56 * 1024 * 1024
MOE_VMEM_LIMIT = 60 * 1024 * 1024

CHUNK = 64
SG_BLOCK = 128
LN_EPS = 1e-5
TOP_K = 2

MIX_TM = 512
MIX_TN = MXU_COL
OUT_TN = 512
ROUTE_TM = 512
MOE_SUB = 128
MOE_ROWS = 512
MOE_TF = MXU_COL
ISSUE_UNROLL = 8
CMB_TM = 256
GATE_TM = 1024

_dot = functools.partial(jnp.dot, preferred_element_type=F32)


def _params(*sem, vmem=VMEM_LIMIT):
    return pltpu.CompilerParams(dimension_semantics=sem, vmem_limit_bytes=vmem)


def _layer_norm(z, g, b):
    mu = jnp.mean(z, axis=-1, keepdims=True)
    zc = z - mu
    var = jnp.mean(zc * zc, axis=-1, keepdims=True)
    return zc * lax.rsqrt(var + LN_EPS) * g + b


def _mixer_kernel(x_ref, w_ref, cw_ref, lng_ref, lnb_ref, sgw_ref, sgb_ref, y_ref,
                  xb_ref, gbuf_ref, carry_ref, *, tm, tn, tiles_per_seq):
    i = pl.program_id(0)
    j = pl.program_id(1)

    @pl.when(j == 0)
    def _():
        xb_ref[...] = x_ref[...].astype(BF16)

    proj = _dot(xb_ref[...], w_ref[...])
    b, c, h, u, v = (proj[:, k * tn:(k + 1) * tn] for k in range(5))
    g = c * h
    first = (i % tiles_per_seq) == 0

    @pl.when(first)
    def _():
        gbuf_ref[0:SUBLANE, :] = jnp.zeros((SUBLANE, tn), F32)

    @pl.when(jnp.logical_not(first))
    def _():
        gbuf_ref[0:SUBLANE, :] = carry_ref[j]

    gbuf_ref[SUBLANE:, :] = g
    carry_ref[j] = g[tm - SUBLANE:, :]
    g1 = gbuf_ref[pl.ds(SUBLANE - 1, tm), :]
    g2 = gbuf_ref[pl.ds(SUBLANE - 2, tm), :]
    cw = cw_ref[...]
    conv = cw[0:1, :] * g2 + cw[1:2, :] * g1 + cw[2:3, :] * g
    y_ref[:, 0:tn] = (b * conv).astype(y_ref.dtype)

    pos_i = lax.broadcasted_iota(jnp.int32, (SG_BLOCK, SG_BLOCK), 0)
    pos_j = lax.broadcasted_iota(jnp.int32, (SG_BLOCK, SG_BLOCK), 1)
    mask = (pos_j // CHUNK) <= (pos_i // CHUNK)
    for hh in range(tn // LANE):
        sl = slice(hh * LANE, (hh + 1) * LANE)
        vn = _layer_norm(jax.nn.gelu(v[:, sl]), lng_ref[:, sl], lnb_ref[:, sl]).astype(BF16)
        gu = jax.nn.gelu(u[:, sl])
        ws = jnp.where(mask, sgw_ref[hh], 0.0).astype(BF16)
        bcol = sgb_ref[hh]
        for n in range(tm // SG_BLOCK):
            rows = slice(n * SG_BLOCK, (n + 1) * SG_BLOCK)
            sg = _dot(ws, vn[rows, :]) + bcol
            y_ref[rows, tn + hh * LANE:tn + (hh + 1) * LANE] = (gu[rows, :] * sg).astype(y_ref.dtype)


def _mixer(x2, w_in_p, conv_w, lng, lnb, sg_w, sg_bcol, seq):
    t, d = x2.shape
    cdim = conv_w.shape[1]
    tm, tn = MIX_TM, MIX_TN
    nj = cdim // tn
    assert seq % tm == 0 and t % tm == 0 and cdim % tn == 0 and w_in_p.shape[1] == 5 * cdim
    hp = tn // LANE
    return pl.pallas_call(
        functools.partial(_mixer_kernel, tm=tm, tn=tn, tiles_per_seq=seq // tm),
        grid=(t // tm, nj),
        in_specs=[
            pl.BlockSpec((tm, d), lambda i, j: (i, 0)),
            pl.BlockSpec((d, 5 * tn), lambda i, j: (0, j)),
            pl.BlockSpec((conv_w.shape[0], tn), lambda i, j: (0, j)),
            pl.BlockSpec((1, tn), lambda i, j: (0, j)),
            pl.BlockSpec((1, tn), lambda i, j: (0, j)),
            pl.BlockSpec((hp, SG_BLOCK, SG_BLOCK), lambda i, j: (j, 0, 0)),
            pl.BlockSpec((hp, SG_BLOCK, 1), lambda i, j: (j, 0, 0)),
        ],
        out_specs=pl.BlockSpec((tm, 2 * tn), lambda i, j: (i, j)),
        out_shape=jax.ShapeDtypeStruct((t, 2 * cdim), BF16),
        scratch_shapes=[pltpu.VMEM((tm, d), BF16),
                        pltpu.VMEM((tm + SUBLANE, tn), F32),
                        pltpu.VMEM((nj, SUBLANE, tn), F32)],
        compiler_params=_params("arbitrary", "arbitrary"),
        name="mixer",
    )(x2, w_in_p, conv_w, lng, lnb, sg_w, sg_bcol)


def _outproj_kernel(y_ref, w_ref, x_ref, g_ref, b_ref, wr_ref, br_ref, h1_ref, lg_ref, *, tn, nj, alpha):
    j = pl.program_id(1)
    z = alpha * x_ref[...] + _dot(y_ref[...], w_ref[...])
    for jj in range(nj):
        @pl.when(j == jj)
        def _(jj=jj):
            h1_ref[:, jj * tn:(jj + 1) * tn] = z

    @pl.when(j == nj - 1)
    def _():
        h = _layer_norm(h1_ref[...], g_ref[...], b_ref[...])
        h1_ref[...] = h
        lg_ref[...] = _dot(h.astype(BF16), wr_ref[...]) + br_ref[...]


def _outproj(y, w_out_p, x2, g, b, wr, br, alpha):
    t, d = x2.shape
    kdim = y.shape[1]
    tm, tn = MIX_TM, OUT_TN
    nj = d // tn
    assert t % tm == 0 and d % tn == 0 and w_out_p.shape[0] == kdim
    return pl.pallas_call(
        functools.partial(_outproj_kernel, tn=tn, nj=nj, alpha=alpha),
        grid=(t // tm, nj),
        in_specs=[
            pl.BlockSpec((tm, kdim), lambda i, j: (i, 0)),
            pl.BlockSpec((kdim, tn), lambda i, j: (0, j)),
            pl.BlockSpec((tm, tn), lambda i, j: (i, j)),
            pl.BlockSpec((1, d), lambda i, j: (0, 0)),
            pl.BlockSpec((1, d), lambda i, j: (0, 0)),
            pl.BlockSpec((d, LANE), lambda i, j: (0, 0)),
            pl.BlockSpec((1, LANE), lambda i, j: (0, 0)),
        ],
        out_specs=[pl.BlockSpec((tm, d), lambda i, j: (i, 0)),
                   pl.BlockSpec((tm, LANE), lambda i, j: (i, 0))],
        out_shape=[jax.ShapeDtypeStruct((t, d), F32), jax.ShapeDtypeStruct((t, LANE), F32)],
        compiler_params=_params("arbitrary", "arbitrary"),
        name="outproj",
    )(y, w_out_p, x2, g, b, wr, br)


def _route_kernel(lg_ref, pk_ref, cnt_ref, run_ref, *, n_groups, per_group):
    i = pl.program_id(0)

    @pl.when(i == 0)
    def _():
        run_ref[...] = jnp.zeros_like(run_ref)

    l = lg_ref[...]
    tm = l.shape[0]
    lane = lax.broadcasted_iota(jnp.int32, l.shape, 1)
    neg = jnp.float32(-jnp.inf)
    gmask = lane < n_groups
    gl = jnp.where(gmask, l, neg)
    gmax = jnp.max(gl, axis=-1, keepdims=True)
    gi = jnp.min(jnp.where(gl == gmax, lane, LANE), axis=-1, keepdims=True)
    gp = 1.0 / jnp.sum(jnp.where(gmask, jnp.exp(l - gmax), 0.0), axis=-1, keepdims=True)
    elane = lane - n_groups
    emask = (elane >= 0) & (elane // per_group == gi)
    el = jnp.where(emask, l, neg)
    m1 = jnp.max(el, axis=-1, keepdims=True)
    i1 = jnp.min(jnp.where(el == m1, lane, LANE), axis=-1, keepdims=True)
    el2 = jnp.where(lane == i1, neg, el)
    m2 = jnp.max(el2, axis=-1, keepdims=True)
    i2 = jnp.min(jnp.where(el2 == m2, lane, LANE), axis=-1, keepdims=True)
    t2 = jnp.exp(m2 - m1)
    w0 = gp / (1.0 + t2)
    w1 = gp * t2 / (1.0 + t2)
    e0 = i1 - n_groups
    e1 = i2 - n_groups
    oh0 = lane == e0
    oh1 = lane == e1
    oh = (oh0 | oh1).astype(BF16)
    row = lax.broadcasted_iota(jnp.int32, (tm, tm), 0)
    col = lax.broadcasted_iota(jnp.int32, (tm, tm), 1)
    tri = (row > col).astype(BF16)
    before = run_ref[...] + _dot(tri, oh)
    r0 = jnp.sum(jnp.where(oh0, before, 0.0), axis=-1, keepdims=True)
    r1 = jnp.sum(jnp.where(oh1, before, 0.0), axis=-1, keepdims=True)
    total = run_ref[...] + jnp.sum(oh.astype(F32), axis=0, keepdims=True)
    run_ref[...] = total
    cnt_ref[...] = total
    vals = (e0.astype(F32), e1.astype(F32), w0, w1, r0, r1)
    pk = jnp.zeros(l.shape, F32)
    for k, val in enumerate(vals):
        pk = jnp.where(lane == k, val, pk)
    pk_ref[...] = pk


def _route(logits, n_groups, per_group):
    t = logits.shape[0]
    tm = ROUTE_TM
    assert t % tm == 0 and n_groups * (1 + per_group) <= LANE
    return pl.pallas_call(
        functools.partial(_route_kernel, n_groups=n_groups, per_group=per_group),
        grid=(t // tm,),
        in_specs=[pl.BlockSpec((tm, LANE), lambda i: (i, 0))],
        out_specs=[pl.BlockSpec((tm, LANE), lambda i: (i, 0)),
                   pl.BlockSpec((1, LANE), lambda i: (0, 0))],
        out_shape=[jax.ShapeDtypeStruct((t, LANE), F32), jax.ShapeDtypeStruct((1, LANE), F32)],
        scratch_shapes=[pltpu.VMEM((1, LANE), F32)],
        compiler_params=_params("arbitrary"),
        name="route",
    )(logits)


def _rowmap_kernel(eid_ref, rank_ref, start_ref, asg_ref, *, n_rows, n_asg):
    def fill(r, c):
        asg_ref[r] = 0
        return c

    lax.fori_loop(0, n_rows, fill, 0, unroll=ISSUE_UNROLL)

    def place(a, c):
        asg_ref[start_ref[eid_ref[a]] + rank_ref[a]] = a
        return c

    lax.fori_loop(0, n_asg, place, 0, unroll=ISSUE_UNROLL)


def _rowmap(eid, rank, start, n_rows):
    n_asg = eid.shape[0]
    smem = pl.BlockSpec(memory_space=pltpu.SMEM)
    return pl.pallas_call(
        functools.partial(_rowmap_kernel, n_rows=n_rows, n_asg=n_asg),
        in_specs=[smem, smem, smem],
        out_specs=smem,
        out_shape=jax.ShapeDtypeStruct((n_rows,), jnp.int32),
        name="rowmap",
    )(eid, rank, start)


def _moe_kernel(be_ref, row0_ref, nsub_ref, nreal_ref, asg_ref, h1_hbm, wg_ref, wu_ref, wd_ref, y_hbm,
                xg_ref, xb_ref, yacc_ref, gsem, ssem, *, nb, nf, sub, tok_mask):
    b = pl.program_id(0)
    f = pl.program_id(1)
    nsub = nsub_ref[b]

    def gather_start(blk):
        base = row0_ref[blk]

        def row(r):
            tok = asg_ref[base + r] & tok_mask
            pltpu.make_async_copy(h1_hbm.at[pl.ds(tok, 1)], xg_ref.at[pl.ds(r, 1)], gsem).start()

        def body(q, c):
            for k in range(ISSUE_UNROLL):
                row(q * ISSUE_UNROLL + k)
            return c

        lax.fori_loop(0, nsub_ref[blk] * (sub // ISSUE_UNROLL), body, 0)

    def gather_wait(blk):
        def body(s, c):
            pltpu.make_async_copy(h1_hbm.at[pl.ds(0, sub)], xg_ref.at[pl.ds(s * sub, sub)], gsem).wait()
            return c

        lax.fori_loop(0, nsub_ref[blk], body, 0)

    def scatter_start(blk):
        base = row0_ref[blk]

        def row(r):
            dst = asg_ref[base + r]
            pltpu.make_async_copy(yacc_ref.at[pl.ds(r, 1)], y_hbm.at[pl.ds(dst, 1)], ssem).start()

        def body(q, c):
            for k in range(ISSUE_UNROLL):
                row(q * ISSUE_UNROLL + k)
            return c

        groups = nreal_ref[blk] // ISSUE_UNROLL
        lax.fori_loop(0, groups, body, 0)

        def tail(r, c):
            row(r)
            return c

        lax.fori_loop(groups * ISSUE_UNROLL, nreal_ref[blk], tail, 0)

    def scatter_wait(blk):
        full = nreal_ref[blk] // sub

        def tile(s, c):
            pltpu.make_async_copy(yacc_ref.at[pl.ds(s * sub, sub)], y_hbm.at[pl.ds(0, sub)], ssem).wait()
            return c

        lax.fori_loop(0, full, tile, 0)

        def row(r, c):
            pltpu.make_async_copy(yacc_ref.at[pl.ds(0, 1)], y_hbm.at[pl.ds(0, 1)], ssem).wait()
            return c

        lax.fori_loop(full * sub, nreal_ref[blk], row, 0)

    @pl.when((f == 0) & (b > 0))
    def _():
        scatter_wait(jnp.maximum(b - 1, 0))

    @pl.when(nsub > 0)
    def _():
        @pl.when(f == 0)
        def _():
            @pl.when(b == 0)
            def _():
                gather_start(0)

            gather_wait(b)

            def cast(s, c):
                rows = pl.ds(pl.multiple_of(s * sub, sub), sub)
                xb_ref[rows, :] = xg_ref[rows, :].astype(BF16)
                return c

            lax.fori_loop(0, nsub, cast, 0)

            @pl.when(b + 1 < nb)
            def _():
                gather_start(jnp.minimum(b + 1, nb - 1))

        wg = wg_ref[...].astype(BF16)
        wu = wu_ref[...].astype(BF16)
        wd = wd_ref[...].astype(BF16)

        def body(s, c):
            rows = pl.ds(pl.multiple_of(s * sub, sub), sub)
            xs = xb_ref[rows, :]
            hb = (jax.nn.silu(_dot(xs, wg)) * _dot(xs, wu)).astype(BF16)
            yp = _dot(hb, wd)

            @pl.when(f == 0)
            def _():
                yacc_ref[rows, :] = yp

            @pl.when(f > 0)
            def _():
                yacc_ref[rows, :] += yp

            return c

        lax.fori_loop(0, nsub, body, 0)

        @pl.when(f == nf - 1)
        def _():
            scatter_start(b)

            @pl.when(b == nb - 1)
            def _():
                scatter_wait(b)


def _moe(be, row0, nsub, nreal, asg, h1, w_gate, w_up, w_down, n_out_rows):
    t, d = h1.shape
    ne, _, de = w_gate.shape
    nb = be.shape[0]
    nf = de // MOE_TF
    assert de % MOE_TF == 0 and t & (t - 1) == 0
    last = nf - 1
    fsel = lambda f, n: jnp.where(n > 0, f, last)
    grid_spec = pltpu.PrefetchScalarGridSpec(
        num_scalar_prefetch=5,
        grid=(nb, nf),
        in_specs=[
            pl.BlockSpec(memory_space=pl.ANY),
            pl.BlockSpec((None, d, MOE_TF), lambda b, f, be, r0, ns, nr, asg: (be[b], 0, fsel(f, ns[b]))),
            pl.BlockSpec((None, d, MOE_TF), lambda b, f, be, r0, ns, nr, asg: (be[b], 0, fsel(f, ns[b]))),
            pl.BlockSpec((None, MOE_TF, d), lambda b, f, be, r0, ns, nr, asg: (be[b], fsel(f, ns[b]), 0)),
        ],
        out_specs=pl.BlockSpec(memory_space=pl.ANY),
        scratch_shapes=[pltpu.VMEM((MOE_ROWS, d), F32),
                        pltpu.VMEM((MOE_ROWS, d), BF16),
                        pltpu.VMEM((MOE_ROWS, d), F32),
                        pltpu.SemaphoreType.DMA,
                        pltpu.SemaphoreType.DMA],
    )
    return pl.pallas_call(
        functools.partial(_moe_kernel, nb=nb, nf=nf, sub=MOE_SUB, tok_mask=t - 1),
        grid_spec=grid_spec,
        out_shape=jax.ShapeDtypeStruct((n_out_rows, d), F32),
        compiler_params=_params("arbitrary", "arbitrary", vmem=MOE_VMEM_LIMIT),
        name="moe",
    )(be, row0, nsub, nreal, asg, h1, w_gate, w_up, w_down)


def _combine_kernel(h1_ref, y0_ref, y1_ref, pk_ref, g_ref, b_ref, h2_ref, h2b_ref, *, alpha):
    pk = pk_ref[...]
    ffn = pk[:, 2:3] * y0_ref[...] + pk[:, 3:4] * y1_ref[...]
    h = _layer_norm(alpha * h1_ref[...] + ffn, g_ref[...], b_ref[...])
    h2_ref[...] = h
    h2b_ref[...] = h.astype(BF16)


def _combine(h1, y2, pk, g, b, alpha):
    t, d = h1.shape
    tm = CMB_TM
    assert t % tm == 0
    nt = t // tm
    return pl.pallas_call(
        functools.partial(_combine_kernel, alpha=alpha),
        grid=(nt,),
        in_specs=[
            pl.BlockSpec((tm, d), lambda i: (i, 0)),
            pl.BlockSpec((tm, d), lambda i: (i, 0)),
            pl.BlockSpec((tm, d), lambda i: (i + nt, 0)),
            pl.BlockSpec((tm, LANE), lambda i: (i, 0)),
            pl.BlockSpec((1, d), lambda i: (0, 0)),
            pl.BlockSpec((1, d), lambda i: (0, 0)),
        ],
        out_specs=[pl.BlockSpec((tm, d), lambda i: (i, 0)), pl.BlockSpec((tm, d), lambda i: (i, 0))],
        out_shape=[jax.ShapeDtypeStruct((t, d), F32), jax.ShapeDtypeStruct((t, d), BF16)],
        compiler_params=_params("arbitrary"),
        name="combine",
    )(h1, y2, y2, pk, g, b)


def _gate_kernel(h2b_ref, wpg_ref, bpg_ref, h2_ref, p_ref, wpe_ref, o_ref):
    gate = jax.nn.sigmoid(_dot(h2b_ref[...], wpg_ref[...]) + bpg_ref[...])
    ple = _dot(p_ref[...].astype(BF16), wpe_ref[...])
    o_ref[...] = h2_ref[...] + gate * ple


def _gate(h2, h2b, w_pg_b, b_pg, p2, w_pe_b):
    t, d = h2.shape
    pd = p2.shape[1]
    tm, tn = GATE_TM, OUT_TN
    assert t % tm == 0 and d % tn == 0
    return pl.pallas_call(
        _gate_kernel,
        grid=(t // tm, d // tn),
        in_specs=[
            pl.BlockSpec((tm, d), lambda i, j: (i, 0)),
            pl.BlockSpec((d, tn), lambda i, j: (0, j)),
            pl.BlockSpec((1, tn), lambda i, j: (0, j)),
            pl.BlockSpec((tm, tn), lambda i, j: (i, j)),
            pl.BlockSpec((tm, pd), lambda i, j: (i, 0)),
            pl.BlockSpec((pd, tn), lambda i, j: (0, j)),
        ],
        out_specs=pl.BlockSpec((tm, tn), lambda i, j: (i, j)),
        out_shape=jax.ShapeDtypeStruct((t, d), F32),
        compiler_params=_params("arbitrary", "arbitrary"),
        name="gate",
    )(h2b, w_pg_b, b_pg, h2, p2, w_pe_b)


def _block_table(counts, n_asg):
    ne = counts.shape[0]
    padded = ((counts + MOE_SUB - 1) // MOE_SUB) * MOE_SUB
    pad_end = jnp.cumsum(padded)
    pad_start = pad_end - padded
    nblk = (padded + MOE_ROWS - 1) // MOE_ROWS
    blk_end = jnp.cumsum(nblk)
    blk_start = blk_end - nblk
    nb = ne + n_asg // MOE_ROWS
    bidx = jnp.arange(nb, dtype=jnp.int32)
    used = bidx < blk_end[-1]
    be = jnp.minimum(jnp.searchsorted(blk_end, jnp.minimum(bidx, blk_end[-1] - 1), side="right"), ne - 1)
    be = be.astype(jnp.int32)
    within = bidx - blk_start[be]
    row0 = pad_start[be] + within * MOE_ROWS
    nsub = jnp.clip((padded[be] - within * MOE_ROWS) // MOE_SUB, 0, MOE_ROWS // MOE_SUB)
    nsub = jnp.where(used, nsub, 0)
    row0 = jnp.where(used, row0, 0)
    nreal = jnp.where(used, jnp.clip(counts[be] - within * MOE_ROWS, 0, MOE_ROWS), 0)
    i32 = lambda a: a.astype(jnp.int32)
    return i32(pad_start), be, i32(row0), i32(nsub), i32(nreal)


def kernel(x, p, w_in, conv_w, sg_ln_g, sg_ln_b, sg_w, sg_b, w_out, ln1_g, ln1_b, w_rg, b_rg, w_re, b_re,
           w_gate, w_up, w_down, ln2_g, ln2_b, w_pg, b_pg, w_pe):
    depth = w_in.shape[0]
    bsz, seq, d = x.shape
    t = bsz * seq
    alpha = (2 * depth) ** 0.25
    n_groups = w_rg.shape[-1]
    ne = w_re.shape[-1]
    per_group = ne // n_groups
    n_asg = t * TOP_K
    n_rows = ((n_asg + ne * (MOE_SUB - 1) + MOE_SUB - 1) // MOE_SUB) * MOE_SUB

    h = x.reshape(t, d)
    for i in range(depth):
        cdim = conv_w.shape[-1]
        nj = cdim // MIX_TN
        w_in_p = w_in[i].astype(BF16).reshape(d, 5, nj, MIX_TN).transpose(0, 2, 1, 3).reshape(d, 5 * cdim)
        w_out_p = w_out[i].astype(BF16).reshape(2, nj, MIX_TN, d).transpose(1, 0, 2, 3).reshape(2 * cdim, d)
        y = _mixer(h, w_in_p, conv_w[i], sg_ln_g[i].reshape(1, -1), sg_ln_b[i].reshape(1, -1),
                   sg_w[i], sg_b[i][:, :, None], seq)
        wr = jnp.concatenate([w_rg[i], w_re[i], jnp.zeros((d, LANE - n_groups - ne), F32)], axis=1).astype(BF16)
        br = jnp.concatenate([b_rg[i], b_re[i], jnp.zeros((LANE - n_groups - ne,), F32)]).reshape(1, LANE)
        h1, logits = _outproj(y, w_out_p, h, ln1_g[i].reshape(1, d), ln1_b[i].reshape(1, d), wr, br, alpha)
        pk, cnt = _route(logits, n_groups, per_group)
        pad_start, be, row0, nsub, nreal = _block_table(cnt[0, :ne].astype(jnp.int32), n_asg)
        eid = jnp.concatenate([pk[:, 0], pk[:, 1]]).astype(jnp.int32)
        rank = jnp.concatenate([pk[:, 4], pk[:, 5]]).astype(jnp.int32)
        asg = _rowmap(eid, rank, pad_start, n_rows)
        y2 = _moe(be, row0, nsub, nreal, asg, h1, w_gate[i], w_up[i], w_down[i], n_asg)
        h2, h2b = _combine(h1, y2, pk, ln2_g[i].reshape(1, d), ln2_b[i].reshape(1, d), alpha)
        h = _gate(h2, h2b, w_pg[i].astype(BF16), b_pg[i].reshape(1, d), p[i].reshape(t, -1), w_pe[i].astype(BF16))
    return h.reshape(bsz, seq, d)
```

```python
import functools

import jax
import jax.numpy as jnp
from jax import lax
from jax.experimental import pallas as pl
from jax.experimental.pallas import tpu as pltpu

F32 = jnp.float32
BF16 = jnp.bfloat16

LANE = 128
SUBLANE = 8
MXU_COL = 256
VMEM_LIMIT = 56 * 1024 * 1024
MOE_VMEM_LIMIT = 60 * 1024 * 1024

CHUNK = 64
SG_BLOCK = 128
LN_EPS = 1e-5
TOP_K = 2

MIX_TM = 512
MIX_TN = MXU_COL
OUT_TN = 512
ROUTE_TM = 512
MOE_SUB = 128
MOE_ROWS = 512
MOE_TF = MXU_COL
ISSUE_UNROLL = 8
CAST_TK = 1024

_dot = functools.partial(jnp.dot, preferred_element_type=F32)


def _params(*sem, vmem=VMEM_LIMIT):
    return pltpu.CompilerParams(dimension_semantics=sem, vmem_limit_bytes=vmem)


def _layer_norm(z, g, b):
    mu = jnp.mean(z, axis=-1, keepdims=True)
    zc = z - mu
    var = jnp.mean(zc * zc, axis=-1, keepdims=True)
    return zc * lax.rsqrt(var + LN_EPS) * g + b


def _cast_kernel(w_ref, o_ref):
    o_ref[...] = w_ref[...].astype(o_ref.dtype)


def _regroup_cols(w, groups, nj, tn):
    d = w.shape[0]
    tk = min(CAST_TK, d)
    assert d % tk == 0 and w.shape[1] == groups * nj * tn
    return pl.pallas_call(
        _cast_kernel,
        grid=(nj, groups, d // tk),
        in_specs=[pl.BlockSpec((tk, tn), lambda j, s, k: (k, s * nj + j))],
        out_specs=pl.BlockSpec((tk, tn), lambda j, s, k: (k, j * groups + s)),
        out_shape=jax.ShapeDtypeStruct(w.shape, BF16),
        compiler_params=_params("arbitrary", "arbitrary", "arbitrary"),
        name="regroup_cols",
    )(w)


def _regroup_rows(w, groups, nj, tn):
    d = w.shape[1]
    assert w.shape[0] == groups * nj * tn
    return pl.pallas_call(
        _cast_kernel,
        grid=(nj, groups),
        in_specs=[pl.BlockSpec((tn, d), lambda j, s: (s * nj + j, 0))],
        out_specs=pl.BlockSpec((tn, d), lambda j, s: (j * groups + s, 0)),
        out_shape=jax.ShapeDtypeStruct(w.shape, BF16),
        compiler_params=_params("arbitrary", "arbitrary"),
        name="regroup_rows",
    )(w)


def _mixer_kernel(x_ref, w_ref, cw_ref, lng_ref, lnb_ref, sgw_ref, sgb_ref, y_ref,
                  xb_ref, gbuf_ref, carry_ref, *, tm, tn, tiles_per_seq):
    i = pl.program_id(0)
    j = pl.program_id(1)

    @pl.when(j == 0)
    def _():
        xb_ref[...] = x_ref[...].astype(BF16)

    proj = _dot(xb_ref[...], w_ref[...])
    b, c, h, u, v = (proj[:, k * tn:(k + 1) * tn] for k in range(5))
    g = c * h
    first = (i % tiles_per_seq) == 0

    @pl.when(first)
    def _():
        gbuf_ref[0:SUBLANE, :] = jnp.zeros((SUBLANE, tn), F32)

    @pl.when(jnp.logical_not(first))
    def _():
        gbuf_ref[0:SUBLANE, :] = carry_ref[j]

    gbuf_ref[SUBLANE:, :] = g
    carry_ref[j] = g[tm - SUBLANE:, :]
    g1 = gbuf_ref[pl.ds(SUBLANE - 1, tm), :]
    g2 = gbuf_ref[pl.ds(SUBLANE - 2, tm), :]
    cw = cw_ref[...]
    conv = cw[0:1, :] * g2 + cw[1:2, :] * g1 + cw[2:3, :] * g
    y_ref[:, 0:tn] = (b * conv).astype(y_ref.dtype)

    pos_i = lax.broadcasted_iota(jnp.int32, (SG_BLOCK, SG_BLOCK), 0)
    pos_j = lax.broadcasted_iota(jnp.int32, (SG_BLOCK, SG_BLOCK), 1)
    mask = (pos_j // CHUNK) <= (pos_i // CHUNK)
    for hh in range(tn // LANE):
        sl = slice(hh * LANE, (hh + 1) * LANE)
        vn = _layer_norm(jax.nn.gelu(v[:, sl]), lng_ref[:, sl], lnb_ref[:, sl]).astype(BF16)
        gu = jax.nn.gelu(u[:, sl])
        ws = jnp.where(mask, sgw_ref[hh], 0.0).astype(BF16)
        bcol = sgb_ref[hh]
        for n in range(tm // SG_BLOCK):
            rows = slice(n * SG_BLOCK, (n + 1) * SG_BLOCK)
            sg = _dot(ws, vn[rows, :]) + bcol
            y_ref[rows, tn + hh * LANE:tn + (hh + 1) * LANE] = (gu[rows, :] * sg).astype(y_ref.dtype)


def _mixer(x2, w_in_p, conv_w, lng, lnb, sg_w, sg_bcol, seq):
    t, d = x2.shape
    cdim = conv_w.shape[1]
    tm, tn = MIX_TM, MIX_TN
    nj = cdim // tn
    assert seq % tm == 0 and t % tm == 0 and cdim % tn == 0 and w_in_p.shape[1] == 5 * cdim
    hp = tn // LANE
    return pl.pallas_call(
        functools.partial(_mixer_kernel, tm=tm, tn=tn, tiles_per_seq=seq // tm),
        grid=(t // tm, nj),
        in_specs=[
            pl.BlockSpec((tm, d), lambda i, j: (i, 0)),
            pl.BlockSpec((d, 5 * tn), lambda i, j: (0, j)),
            pl.BlockSpec((conv_w.shape[0], tn), lambda i, j: (0, j)),
            pl.BlockSpec((1, tn), lambda i, j: (0, j)),
            pl.BlockSpec((1, tn), lambda i, j: (0, j)),
            pl.BlockSpec((hp, SG_BLOCK, SG_BLOCK), lambda i, j: (j, 0, 0)),
            pl.BlockSpec((hp, SG_BLOCK, 1), lambda i, j: (j, 0, 0)),
        ],
        out_specs=pl.BlockSpec((tm, 2 * tn), lambda i, j: (i, j)),
        out_shape=jax.ShapeDtypeStruct((t, 2 * cdim), BF16),
        scratch_shapes=[pltpu.VMEM((tm, d), BF16),
                        pltpu.VMEM((tm + SUBLANE, tn), F32),
                        pltpu.VMEM((nj, SUBLANE, tn), F32)],
        compiler_params=_params("arbitrary", "arbitrary"),
        name="mixer",
    )(x2, w_in_p, conv_w, lng, lnb, sg_w, sg_bcol)


def _outproj_kernel(y_ref, w_ref, x_ref, g_ref, b_ref, wr_ref, br_ref, h1_ref, lg_ref, *, tn, nj, alpha):
    j = pl.program_id(1)
    z = alpha * x_ref[...] + _dot(y_ref[...], w_ref[...])
    for jj in range(nj):
        @pl.when(j == jj)
        def _(jj=jj):
            h1_ref[:, jj * tn:(jj + 1) * tn] = z

    @pl.when(j == nj - 1)
    def _():
        h = _layer_norm(h1_ref[...], g_ref[...], b_ref[...])
        h1_ref[...] = h
        lg_ref[...] = _dot(h.astype(BF16), wr_ref[...]) + br_ref[...]


def _outproj(y, w_out_p, x2, g, b, wr, br, alpha):
    t, d = x2.shape
    kdim = y.shape[1]
    tm, tn = MIX_TM, OUT_TN
    nj = d // tn
    assert t % tm == 0 and d % tn == 0 and w_out_p.shape[0] == kdim
    return pl.pallas_call(
        functools.partial(_outproj_kernel, tn=tn, nj=nj, alpha=alpha),
        grid=(t // tm, nj),
        in_specs=[
            pl.BlockSpec((tm, kdim), lambda i, j: (i, 0)),
            pl.BlockSpec((kdim, tn), lambda i, j: (0, j)),
            pl.BlockSpec((tm, tn), lambda i, j: (i, j)),
            pl.BlockSpec((1, d), lambda i, j: (0, 0)),
            pl.BlockSpec((1, d), lambda i, j: (0, 0)),
            pl.BlockSpec((d, LANE), lambda i, j: (0, 0)),
            pl.BlockSpec((1, LANE), lambda i, j: (0, 0)),
        ],
        out_specs=[pl.BlockSpec((tm, d), lambda i, j: (i, 0)),
                   pl.BlockSpec((tm, LANE), lambda i, j: (i, 0))],
        out_shape=[jax.ShapeDtypeStruct((t, d), F32), jax.ShapeDtypeStruct((t, LANE), F32)],
        compiler_params=_params("arbitrary", "arbitrary"),
        name="outproj",
    )(y, w_out_p, x2, g, b, wr, br)


def _route_kernel(lg_ref, pk_ref, cnt_ref, run_ref, *, n_groups, per_group):
    i = pl.program_id(0)

    @pl.when(i == 0)
    def _():
        run_ref[...] = jnp.zeros_like(run_ref)

    l = lg_ref[...]
    tm = l.shape[0]
    lane = lax.broadcasted_iota(jnp.int32, l.shape, 1)
    neg = jnp.float32(-jnp.inf)
    gmask = lane < n_groups
    gl = jnp.where(gmask, l, neg)
    gmax = jnp.max(gl, axis=-1, keepdims=True)
    gi = jnp.min(jnp.where(gl == gmax, lane, LANE), axis=-1, keepdims=True)
    gp = 1.0 / jnp.sum(jnp.where(gmask, jnp.exp(l - gmax), 0.0), axis=-1, keepdims=True)
    elane = lane - n_groups
    emask = (elane >= 0) & (elane // per_group == gi)
    el = jnp.where(emask, l, neg)
    m1 = jnp.max(el, axis=-1, keepdims=True)
    i1 = jnp.min(jnp.where(el == m1, lane, LANE), axis=-1, keepdims=True)
    el2 = jnp.where(lane == i1, neg, el)
    m2 = jnp.max(el2, axis=-1, keepdims=True)
    i2 = jnp.min(jnp.where(el2 == m2, lane, LANE), axis=-1, keepdims=True)
    t2 = jnp.exp(m2 - m1)
    w0 = gp / (1.0 + t2)
    w1 = gp * t2 / (1.0 + t2)
    e0 = i1 - n_groups
    e1 = i2 - n_groups
    oh0 = lane == e0
    oh1 = lane == e1
    oh = (oh0 | oh1).astype(BF16)
    row = lax.broadcasted_iota(jnp.int32, (tm, tm), 0)
    col = lax.broadcasted_iota(jnp.int32, (tm, tm), 1)
    tri = (row > col).astype(BF16)
    before = run_ref[...] + _dot(tri, oh)
    r0 = jnp.sum(jnp.where(oh0, before, 0.0), axis=-1, keepdims=True)
    r1 = jnp.sum(jnp.where(oh1, before, 0.0), axis=-1, keepdims=True)
    total = run_ref[...] + jnp.sum(oh.astype(F32), axis=0, keepdims=True)
    run_ref[...] = total
    cnt_ref[...] = total
    vals = (e0.astype(F32), e1.astype(F32), w0, w1, r0, r1)
    pk = jnp.zeros(l.shape, F32)
    for k, val in enumerate(vals):
        pk = jnp.where(lane == k, val, pk)
    pk_ref[...] = pk


def _route(logits, n_groups, per_group):
    t = logits.shape[0]
    tm = ROUTE_TM
    assert t % tm == 0 and n_groups * (1 + per_group) <= LANE
    return pl.pallas_call(
        functools.partial(_route_kernel, n_groups=n_groups, per_group=per_group),
        grid=(t // tm,),
        in_specs=[pl.BlockSpec((tm, LANE), lambda i: (i, 0))],
        out_specs=[pl.BlockSpec((tm, LANE), lambda i: (i, 0)),
                   pl.BlockSpec((1, LANE), lambda i: (0, 0))],
        out_shape=[jax.ShapeDtypeStruct((t, LANE), F32), jax.ShapeDtypeStruct((1, LANE), F32)],
        scratch_shapes=[pltpu.VMEM((1, LANE), F32)],
        compiler_params=_params("arbitrary"),
        name="route",
    )(logits)


def _rowmap_kernel(eid_ref, rank_ref, start_ref, asg_ref, *, n_rows, n_asg):
    def fill(r, c):
        asg_ref[r] = 0
        return c

    lax.fori_loop(0, n_rows, fill, 0, unroll=ISSUE_UNROLL)

    def place(a, c):
        asg_ref[start_ref[eid_ref[a]] + rank_ref[a]] = a
        return c

    lax.fori_loop(0, n_asg, place, 0, unroll=ISSUE_UNROLL)


def _rowmap(eid, rank, start, n_rows):
    n_asg = eid.shape[0]
    smem = pl.BlockSpec(memory_space=pltpu.SMEM)
    return pl.pallas_call(
        functools.partial(_rowmap_kernel, n_rows=n_rows, n_asg=n_asg),
        in_specs=[smem, smem, smem],
        out_specs=smem,
        out_shape=jax.ShapeDtypeStruct((n_rows,), jnp.int32),
        name="rowmap",
    )(eid, rank, start)


def _moe_kernel(be_ref, row0_ref, nsub_ref, nreal_ref, asg_ref, h1_hbm, wg_ref, wu_ref, wd_ref, y_hbm,
                xg_ref, xb_ref, yacc_ref, gsem, ssem, *, nb, nf, sub, tok_mask):
    b = pl.program_id(0)
    f = pl.program_id(1)
    nsub = nsub_ref[b]

    def gather_start(blk):
        base = row0_ref[blk]

        def row(r):
            tok = asg_ref[base + r] & tok_mask
            pltpu.make_async_copy(h1_hbm.at[pl.ds(tok, 1)], xg_ref.at[pl.ds(r, 1)], gsem).start()

        def body(q, c):
            for k in range(ISSUE_UNROLL):
                row(q * ISSUE_UNROLL + k)
            return c

        lax.fori_loop(0, nsub_ref[blk] * (sub // ISSUE_UNROLL), body, 0)

    def gather_wait(blk):
        def body(s, c):
            pltpu.make_async_copy(h1_hbm.at[pl.ds(0, sub)], xg_ref.at[pl.ds(s * sub, sub)], gsem).wait()
            return c

        lax.fori_loop(0, nsub_ref[blk], body, 0)

    def scatter_start(blk):
        base = row0_ref[blk]

        def row(r):
            dst = asg_ref[base + r]
            pltpu.make_async_copy(yacc_ref.at[pl.ds(r, 1)], y_hbm.at[pl.ds(dst, 1)], ssem).start()

        def body(q, c):
            for k in range(ISSUE_UNROLL):
                row(q * ISSUE_UNROLL + k)
            return c

        groups = nreal_ref[blk] // ISSUE_UNROLL
        lax.fori_loop(0, groups, body, 0)

        def tail(r, c):
            row(r)
            return c

        lax.fori_loop(groups * ISSUE_UNROLL, nreal_ref[blk], tail, 0)

    def scatter_wait(blk):
        full = nreal_ref[blk] // sub

        def tile(s, c):
            pltpu.make_async_copy(yacc_ref.at[pl.ds(s * sub, sub)], y_hbm.at[pl.ds(0, sub)], ssem).wait()
            return c

        lax.fori_loop(0, full, tile, 0)

        def row(r, c):
            pltpu.make_async_copy(yacc_ref.at[pl.ds(0, 1)], y_hbm.at[pl.ds(0, 1)], ssem).wait()
            return c

        lax.fori_loop(full * sub, nreal_ref[blk], row, 0)

    @pl.when((f == 0) & (b == 0))
    def _():
        yacc_ref[...] = jnp.zeros(yacc_ref.shape, F32)

    @pl.when((f == 0) & (b > 0))
    def _():
        scatter_wait(jnp.maximum(b - 1, 0))

    @pl.when(nsub > 0)
    def _():
        @pl.when(f == 0)
        def _():
            @pl.when(b == 0)
            def _():
                gather_start(0)

            gather_wait(b)

            def cast(s, c):
                rows = pl.ds(pl.multiple_of(s * sub, sub), sub)
                xb_ref[rows, :] = xg_ref[rows, :].astype(BF16)
                return c

            lax.fori_loop(0, nsub, cast, 0)

            @pl.when(b + 1 < nb)
            def _():
                gather_start(jnp.minimum(b + 1, nb - 1))

        def compute(m):
            rows = slice(0, m * sub)
            xs = xb_ref[rows, :]
            hb = (jax.nn.silu(_dot(xs, wg_ref[...])) * _dot(xs, wu_ref[...])).astype(BF16)
            yp = _dot(hb, wd_ref[...])
            yacc_ref[rows, :] = jnp.where(f == 0, yp, yacc_ref[rows, :] + yp)

        for m in range(1, xb_ref.shape[0] // sub + 1):
            pl.when(nsub == m)(functools.partial(compute, m))

        @pl.when(f == nf - 1)
        def _():
            scatter_start(b)

            @pl.when(b == nb - 1)
            def _():
                scatter_wait(b)


def _moe(be, row0, nsub, nreal, asg, h1, w_gate, w_up, w_down, n_out_rows):
    t, d = h1.shape
    ne, _, de = w_gate.shape
    nb = be.shape[0]
    nf = de // MOE_TF
    assert de % MOE_TF == 0 and t & (t - 1) == 0
    last = nf - 1
    fsel = lambda f, n: jnp.where(n > 0, f, last)
    grid_spec = pltpu.PrefetchScalarGridSpec(
        num_scalar_prefetch=5,
        grid=(nb, nf),
        in_specs=[
            pl.BlockSpec(memory_space=pl.ANY),
            pl.BlockSpec((None, d, MOE_TF), lambda b, f, be, r0, ns, nr, asg: (be[b], 0, fsel(f, ns[b]))),
            pl.BlockSpec((None, d, MOE_TF), lambda b, f, be, r0, ns, nr, asg: (be[b], 0, fsel(f, ns[b]))),
            pl.BlockSpec((None, MOE_TF, d), lambda b, f, be, r0, ns, nr, asg: (be[b], fsel(f, ns[b]), 0)),
        ],
        out_specs=pl.BlockSpec(memory_space=pl.ANY),
        scratch_shapes=[pltpu.VMEM((MOE_ROWS, d), F32),
                        pltpu.VMEM((MOE_ROWS, d), BF16),
                        pltpu.VMEM((MOE_ROWS, d), F32),
                        pltpu.SemaphoreType.DMA,
                        pltpu.SemaphoreType.DMA],
    )
    return pl.pallas_call(
        functools.partial(_moe_kernel, nb=nb, nf=nf, sub=MOE_SUB, tok_mask=t - 1),
        grid_spec=grid_spec,
        out_shape=jax.ShapeDtypeStruct((n_out_rows, d), F32),
        compiler_params=_params("arbitrary", "arbitrary", vmem=MOE_VMEM_LIMIT),
        name="moe",
    )(be, row0, nsub, nreal, asg, h1, w_gate, w_up, w_down)


def _combine_gate_kernel(h1_ref, y0_ref, y1_ref, pk_ref, g_ref, b_ref, wpg_ref, bpg_ref, p_ref, wpe_ref,
                         o_ref, z_ref, hb_ref, *, nj, tn, alpha):
    i = pl.program_id(0)
    j = pl.program_id(1)
    slot = i % 2
    prev = 1 - slot

    @pl.when((i == 0) & (j == 0))
    def _():
        z_ref[1] = jnp.zeros(z_ref.shape[1:], F32)
        hb_ref[1] = jnp.zeros(hb_ref.shape[1:], BF16)

    pk = pk_ref[...]
    ffn = pk[:, 2:3] * y0_ref[...] + pk[:, 3:4] * y1_ref[...]
    z_ref[slot, j] = alpha * h1_ref[...] + ffn

    gate = jax.nn.sigmoid(_dot(hb_ref[prev], wpg_ref[...]) + bpg_ref[...])
    ple = _dot(p_ref[...].astype(BF16), wpe_ref[...])
    o_ref[...] = z_ref[prev, j] + gate * ple

    @pl.when(j == nj - 1)
    def _():
        d = nj * tn
        mu = sum(jnp.sum(z_ref[slot, jj], axis=-1, keepdims=True) for jj in range(nj)) / d
        var = sum(jnp.sum(jnp.square(z_ref[slot, jj] - mu), axis=-1, keepdims=True) for jj in range(nj)) / d
        rstd = lax.rsqrt(var + LN_EPS)
        for jj in range(nj):
            cols = slice(jj * tn, (jj + 1) * tn)
            h = (z_ref[slot, jj] - mu) * rstd * g_ref[:, cols] + b_ref[:, cols]
            z_ref[slot, jj] = h
            hb_ref[slot, :, cols] = h.astype(BF16)


def _combine_gate(h1, y2, pk, g, b, w_pg_b, b_pg, p2, w_pe_b, alpha):
    t, d = h1.shape
    pd = p2.shape[1]
    tm, tn = MIX_TM, OUT_TN
    assert t % tm == 0 and d % tn == 0
    ni, nj = t // tm, d // tn
    row_in = lambda i: jnp.minimum(i, ni - 1)
    col_in = lambda i, j: jnp.where(i < ni, j, nj - 1)
    row_out = lambda i: jnp.maximum(i - 1, 0)
    return pl.pallas_call(
        functools.partial(_combine_gate_kernel, nj=nj, tn=tn, alpha=alpha),
        grid=(ni + 1, nj),
        in_specs=[
            pl.BlockSpec((tm, tn), lambda i, j: (row_in(i), col_in(i, j))),
            pl.BlockSpec((tm, tn), lambda i, j: (row_in(i), col_in(i, j))),
            pl.BlockSpec((tm, tn), lambda i, j: (row_in(i) + ni, col_in(i, j))),
            pl.BlockSpec((tm, LANE), lambda i, j: (row_in(i), 0)),
            pl.BlockSpec((1, d), lambda i, j: (0, 0)),
            pl.BlockSpec((1, d), lambda i, j: (0, 0)),
            pl.BlockSpec((d, tn), lambda i, j: (0, j)),
            pl.BlockSpec((1, tn), lambda i, j: (0, j)),
            pl.BlockSpec((tm, pd), lambda i, j: (row_out(i), 0)),
            pl.BlockSpec((pd, tn), lambda i, j: (0, j)),
        ],
        out_specs=pl.BlockSpec((tm, tn), lambda i, j: (row_out(i), jnp.where(i == 0, 0, j))),
        out_shape=jax.ShapeDtypeStruct((t, d), F32),
        scratch_shapes=[pltpu.VMEM((2, nj, tm, tn), F32), pltpu.VMEM((2, tm, d), BF16)],
        compiler_params=_params("arbitrary", "arbitrary"),
        name="combine_gate",
    )(h1, y2, y2, pk, g, b, w_pg_b, b_pg, p2, w_pe_b)


def _block_table(counts, n_asg):
    ne = counts.shape[0]
    padded = ((counts + MOE_SUB - 1) // MOE_SUB) * MOE_SUB
    pad_end = jnp.cumsum(padded)
    pad_start = pad_end - padded
    nblk = (padded + MOE_ROWS - 1) // MOE_ROWS
    blk_end = jnp.cumsum(nblk)
    blk_start = blk_end - nblk
    nb = ne + n_asg // MOE_ROWS
    bidx = jnp.arange(nb, dtype=jnp.int32)
    used = bidx < blk_end[-1]
    be = jnp.minimum(jnp.searchsorted(blk_end, jnp.minimum(bidx, blk_end[-1] - 1), side="right"), ne - 1)
    be = be.astype(jnp.int32)
    within = bidx - blk_start[be]
    row0 = pad_start[be] + within * MOE_ROWS
    nsub = jnp.clip((padded[be] - within * MOE_ROWS) // MOE_SUB, 0, MOE_ROWS // MOE_SUB)
    nsub = jnp.where(used, nsub, 0)
    row0 = jnp.where(used, row0, 0)
    nreal = jnp.where(used, jnp.clip(counts[be] - within * MOE_ROWS, 0, MOE_ROWS), 0)
    i32 = lambda a: a.astype(jnp.int32)
    return i32(pad_start), be, i32(row0), i32(nsub), i32(nreal)


def kernel(x, p, w_in, conv_w, sg_ln_g, sg_ln_b, sg_w, sg_b, w_out, ln1_g, ln1_b, w_rg, b_rg, w_re, b_re,
           w_gate, w_up, w_down, ln2_g, ln2_b, w_pg, b_pg, w_pe):
    depth = w_in.shape[0]
    bsz, seq, d = x.shape
    t = bsz * seq
    alpha = (2 * depth) ** 0.25
    n_groups = w_rg.shape[-1]
    ne = w_re.shape[-1]
    per_group = ne // n_groups
    n_asg = t * TOP_K
    n_rows = ((n_asg + ne * (MOE_SUB - 1) + MOE_SUB - 1) // MOE_SUB) * MOE_SUB

    h = x.reshape(t, d)
    for i in range(depth):
        cdim = conv_w.shape[-1]
        nj = cdim // MIX_TN
        w_in_p = _regroup_cols(w_in[i], 5, nj, MIX_TN)
        w_out_p = _regroup_rows(w_out[i], 2, nj, MIX_TN)
        y = _mixer(h, w_in_p, conv_w[i], sg_ln_g[i].reshape(1, -1), sg_ln_b[i].reshape(1, -1),
                   sg_w[i], sg_b[i][:, :, None], seq)
        wr = jnp.concatenate([w_rg[i], w_re[i], jnp.zeros((d, LANE - n_groups - ne), F32)], axis=1).astype(BF16)
        br = jnp.concatenate([b_rg[i], b_re[i], jnp.zeros((LANE - n_groups - ne,), F32)]).reshape(1, LANE)
        h1, logits = _outproj(y, w_out_p, h, ln1_g[i].reshape(1, d), ln1_b[i].reshape(1, d), wr, br, alpha)
        pk, cnt = _route(logits, n_groups, per_group)
        pad_start, be, row0, nsub, nreal = _block_table(cnt[0, :ne].astype(jnp.int32), n_asg)
        eid = jnp.concatenate([pk[:, 0], pk[:, 1]]).astype(jnp.int32)
        rank = jnp.concatenate([pk[:, 4], pk[:, 5]]).astype(jnp.int32)
        asg = _rowmap(eid, rank, pad_start, n_rows)
        y2 = _moe(be, row0, nsub, nreal, asg, h1, w_gate[i], w_up[i], w_down[i], n_asg)
        h = _combine_gate(h1, y2, pk, ln2_g[i].reshape(1, d), ln2_b[i].reshape(1, d), w_pg[i].astype(BF16),
                          b_pg[i].reshape(1, d), p[i].reshape(t, -1), w_pe[i].astype(BF16), alpha)
    return h.reshape(bsz, seq, d)
```

```python
import functools

import jax
import jax.numpy as jnp
from jax import lax
from jax.experimental import pallas as pl
from jax.experimental.pallas import tpu as pltpu

F32 = jnp.float32
BF16 = jnp.bfloat16

LANE = 128
SUBLANE = 8
MXU_COL = 256
VMEM_LIMIT = 56 * 1024 * 1024
MOE_VMEM_LIMIT = 60 * 1024 * 1024

CHUNK = 64
SG_BLOCK = 128
LN_EPS = 1e-5
TOP_K = 2

MIX_TM = 512
MIX_TN = MXU_COL
OUT_TN = 512
ROUTE_TM = 512
MOE_SUB = 128
MOE_ROWS = 768
MOE_TF = MXU_COL
ISSUE_UNROLL = 8
CAST_TK = 4096

_dot = functools.partial(jnp.dot, preferred_element_type=F32)


def _params(*sem, vmem=VMEM_LIMIT):
    return pltpu.CompilerParams(dimension_semantics=sem, vmem_limit_bytes=vmem)


def _layer_norm(z, g, b):
    mu = jnp.mean(z, axis=-1, keepdims=True)
    zc = z - mu
    var = jnp.mean(zc * zc, axis=-1, keepdims=True)
    return zc * lax.rsqrt(var + LN_EPS) * g + b


def _cast_kernel(w_ref, o_ref):
    o_ref[...] = w_ref[...].astype(o_ref.dtype)


def _regroup_cols(w, groups, nj, tn):
    d = w.shape[0]
    tk = min(CAST_TK, d)
    assert d % tk == 0 and w.shape[1] == groups * nj * tn
    return pl.pallas_call(
        _cast_kernel,
        grid=(nj, groups, d // tk),
        in_specs=[pl.BlockSpec((tk, tn), lambda j, s, k: (k, s * nj + j))],
        out_specs=pl.BlockSpec((tk, tn), lambda j, s, k: (k, j * groups + s)),
        out_shape=jax.ShapeDtypeStruct(w.shape, BF16),
        compiler_params=_params("arbitrary", "arbitrary", "arbitrary"),
        name="regroup_cols",
    )(w)


def _regroup_rows(w, groups, nj, tn):
    d = w.shape[1]
    assert w.shape[0] == groups * nj * tn
    return pl.pallas_call(
        _cast_kernel,
        grid=(nj, groups),
        in_specs=[pl.BlockSpec((tn, d), lambda j, s: (s * nj + j, 0))],
        out_specs=pl.BlockSpec((tn, d), lambda j, s: (j * groups + s, 0)),
        out_shape=jax.ShapeDtypeStruct(w.shape, BF16),
        compiler_params=_params("arbitrary", "arbitrary"),
        name="regroup_rows",
    )(w)


def _mixer_kernel(x_ref, w_ref, cw_ref, lng_ref, lnb_ref, sgw_ref, sgb_ref, y_ref,
                  xb_ref, pa_ref, pb_ref, gbuf_ref, carry_ref, *, tm, tn, nj, n_steps, tiles_per_seq):
    n = pl.program_id(0)

    @pl.when(n == 0)
    def _():
        pb_ref[...] = jnp.zeros(pb_ref.shape, F32)
        carry_ref[...] = jnp.zeros(carry_ref.shape, F32)

    @pl.when((n % nj == 0) & (n < n_steps))
    def _():
        xb_ref[...] = x_ref[...].astype(BF16)

    m = jnp.maximum(n - 1, 0)
    jp = m % nj
    first = ((m // nj) % tiles_per_seq) == 0

    def mix(proj_ref):
        b, c, h, u, v = (proj_ref[:, k * tn:(k + 1) * tn] for k in range(5))
        g = c * h
        gbuf_ref[0:SUBLANE, :] = jnp.where(first, 0.0, carry_ref[jp])
        gbuf_ref[SUBLANE:, :] = g
        carry_ref[jp] = g[tm - SUBLANE:, :]
        g1 = gbuf_ref[pl.ds(SUBLANE - 1, tm), :]
        g2 = gbuf_ref[pl.ds(SUBLANE - 2, tm), :]
        cw = cw_ref[...]
        conv = cw[0:1, :] * g2 + cw[1:2, :] * g1 + cw[2:3, :] * g
        y_ref[:, 0:tn] = (b * conv).astype(y_ref.dtype)

        pos_i = lax.broadcasted_iota(jnp.int32, (SG_BLOCK, SG_BLOCK), 0)
        pos_j = lax.broadcasted_iota(jnp.int32, (SG_BLOCK, SG_BLOCK), 1)
        mask = (pos_j // CHUNK) <= (pos_i // CHUNK)
        for hh in range(tn // LANE):
            sl = slice(hh * LANE, (hh + 1) * LANE)
            vn = _layer_norm(jax.nn.gelu(v[:, sl]), lng_ref[:, sl], lnb_ref[:, sl]).astype(BF16)
            gu = jax.nn.gelu(u[:, sl])
            ws = jnp.where(mask, sgw_ref[hh], 0.0).astype(BF16)
            bcol = sgb_ref[hh]
            for r in range(tm // SG_BLOCK):
                rows = slice(r * SG_BLOCK, (r + 1) * SG_BLOCK)
                sg = _dot(ws, vn[rows, :]) + bcol
                y_ref[rows, tn + hh * LANE:tn + (hh + 1) * LANE] = (gu[rows, :] * sg).astype(y_ref.dtype)

    def step(store_ref, load_ref):
        store_ref[...] = _dot(xb_ref[...], w_ref[...])
        mix(load_ref)

    pl.when(n % 2 == 0)(functools.partial(step, pa_ref, pb_ref))
    pl.when(n % 2 == 1)(functools.partial(step, pb_ref, pa_ref))


def _mixer(x2, w_in_p, conv_w, lng, lnb, sg_w, sg_bcol, seq):
    t, d = x2.shape
    cdim = conv_w.shape[1]
    tm, tn = MIX_TM, MIX_TN
    nj = cdim // tn
    assert seq % tm == 0 and t % tm == 0 and cdim % tn == 0 and w_in_p.shape[1] == 5 * cdim
    hp = tn // LANE
    n_steps = (t // tm) * nj
    cur = lambda n: jnp.minimum(n, n_steps - 1)
    prv = lambda n: jnp.maximum(n - 1, 0)
    return pl.pallas_call(
        functools.partial(_mixer_kernel, tm=tm, tn=tn, nj=nj, n_steps=n_steps, tiles_per_seq=seq // tm),
        grid=(n_steps + 1,),
        in_specs=[
            pl.BlockSpec((tm, d), lambda n: (cur(n) // nj, 0)),
            pl.BlockSpec((d, 5 * tn), lambda n: (0, cur(n) % nj)),
            pl.BlockSpec((conv_w.shape[0], tn), lambda n: (0, prv(n) % nj)),
            pl.BlockSpec((1, tn), lambda n: (0, prv(n) % nj)),
            pl.BlockSpec((1, tn), lambda n: (0, prv(n) % nj)),
            pl.BlockSpec((hp, SG_BLOCK, SG_BLOCK), lambda n: (prv(n) % nj, 0, 0)),
            pl.BlockSpec((hp, SG_BLOCK, 1), lambda n: (prv(n) % nj, 0, 0)),
        ],
        out_specs=pl.BlockSpec((tm, 2 * tn), lambda n: (prv(n) // nj, prv(n) % nj)),
        out_shape=jax.ShapeDtypeStruct((t, 2 * cdim), BF16),
        scratch_shapes=[pltpu.VMEM((tm, d), BF16),
                        pltpu.VMEM((tm, 5 * tn), F32),
                        pltpu.VMEM((tm, 5 * tn), F32),
                        pltpu.VMEM((tm + SUBLANE, tn), F32),
                        pltpu.VMEM((nj, SUBLANE, tn), F32)],
        compiler_params=_params("arbitrary"),
        name="mixer",
    )(x2, w_in_p, conv_w, lng, lnb, sg_w, sg_bcol)


def _outproj_kernel(y_ref, w_ref, x_ref, g_ref, b_ref, wr_ref, br_ref, h1_ref, lg_ref, z_ref, *, tn, nj, alpha):
    j = pl.program_id(1)
    z_ref[j] = alpha * x_ref[...] + _dot(y_ref[...], w_ref[...])

    @pl.when(j == nj - 1)
    def _():
        d = nj * tn
        mu = sum(jnp.sum(z_ref[jj], axis=-1, keepdims=True) for jj in range(nj)) / d
        var = sum(jnp.sum(jnp.square(z_ref[jj] - mu), axis=-1, keepdims=True) for jj in range(nj)) / d
        rstd = lax.rsqrt(var + LN_EPS)
        logits = br_ref[...]
        for jj in range(nj):
            cols = slice(jj * tn, (jj + 1) * tn)
            h = (z_ref[jj] - mu) * rstd * g_ref[:, cols] + b_ref[:, cols]
            h1_ref[:, cols] = h
            logits = logits + _dot(h.astype(BF16), wr_ref[cols, :])
        lg_ref[...] = logits


def _outproj(y, w_out_p, x2, g, b, wr, br, alpha):
    t, d = x2.shape
    kdim = y.shape[1]
    tm, tn = MIX_TM, OUT_TN
    nj = d // tn
    assert t % tm == 0 and d % tn == 0 and w_out_p.shape[0] == kdim
    return pl.pallas_call(
        functools.partial(_outproj_kernel, tn=tn, nj=nj, alpha=alpha),
        grid=(t // tm, nj),
        in_specs=[
            pl.BlockSpec((tm, kdim), lambda i, j: (i, 0)),
            pl.BlockSpec((kdim, tn), lambda i, j: (0, j)),
            pl.BlockSpec((tm, tn), lambda i, j: (i, j)),
            pl.BlockSpec((1, d), lambda i, j: (0, 0)),
            pl.BlockSpec((1, d), lambda i, j: (0, 0)),
            pl.BlockSpec((d, LANE), lambda i, j: (0, 0)),
            pl.BlockSpec((1, LANE), lambda i, j: (0, 0)),
        ],
        out_specs=[pl.BlockSpec((tm, d), lambda i, j: (i, 0)),
                   pl.BlockSpec((tm, LANE), lambda i, j: (i, 0))],
        out_shape=[jax.ShapeDtypeStruct((t, d), F32), jax.ShapeDtypeStruct((t, LANE), F32)],
        scratch_shapes=[pltpu.VMEM((nj, tm, tn), F32)],
        compiler_params=_params("arbitrary", "arbitrary"),
        name="outproj",
    )(y, w_out_p, x2, g, b, wr, br)


def _route_kernel(lg_ref, pk_ref, cnt_ref, run_ref, *, n_groups, per_group):
    i = pl.program_id(0)

    @pl.when(i == 0)
    def _():
        run_ref[...] = jnp.zeros_like(run_ref)

    l = lg_ref[...]
    tm = l.shape[0]
    lane = lax.broadcasted_iota(jnp.int32, l.shape, 1)
    neg = jnp.float32(-jnp.inf)
    gmask = lane < n_groups
    gl = jnp.where(gmask, l, neg)
    gmax = jnp.max(gl, axis=-1, keepdims=True)
    gi = jnp.min(jnp.where(gl == gmax, lane, LANE), axis=-1, keepdims=True)
    gp = 1.0 / jnp.sum(jnp.where(gmask, jnp.exp(l - gmax), 0.0), axis=-1, keepdims=True)
    elane = lane - n_groups
    emask = (elane >= 0) & (elane // per_group == gi)
    el = jnp.where(emask, l, neg)
    m1 = jnp.max(el, axis=-1, keepdims=True)
    i1 = jnp.min(jnp.where(el == m1, lane, LANE), axis=-1, keepdims=True)
    el2 = jnp.where(lane == i1, neg, el)
    m2 = jnp.max(el2, axis=-1, keepdims=True)
    i2 = jnp.min(jnp.where(el2 == m2, lane, LANE), axis=-1, keepdims=True)
    t2 = jnp.exp(m2 - m1)
    w0 = gp / (1.0 + t2)
    w1 = gp * t2 / (1.0 + t2)
    e0 = i1 - n_groups
    e1 = i2 - n_groups
    oh0 = lane == e0
    oh1 = lane == e1
    oh = (oh0 | oh1).astype(BF16)
    row = lax.broadcasted_iota(jnp.int32, (tm, tm), 0)
    col = lax.broadcasted_iota(jnp.int32, (tm, tm), 1)
    tri = (row > col).astype(BF16)
    before = run_ref[...] + _dot(tri, oh)
    r0 = jnp.sum(jnp.where(oh0, before, 0.0), axis=-1, keepdims=True)
    r1 = jnp.sum(jnp.where(oh1, before, 0.0), axis=-1, keepdims=True)
    total = run_ref[...] + jnp.sum(oh.astype(F32), axis=0, keepdims=True)
    run_ref[...] = total
    cnt_ref[...] = total
    vals = (e0.astype(F32), e1.astype(F32), w0, w1, r0, r1)
    pk = jnp.zeros(l.shape, F32)
    for k, val in enumerate(vals):
        pk = jnp.where(lane == k, val, pk)
    pk_ref[...] = pk


def _route(logits, n_groups, per_group):
    t = logits.shape[0]
    tm = ROUTE_TM
    assert t % tm == 0 and n_groups * (1 + per_group) <= LANE
    return pl.pallas_call(
        functools.partial(_route_kernel, n_groups=n_groups, per_group=per_group),
        grid=(t // tm,),
        in_specs=[pl.BlockSpec((tm, LANE), lambda i: (i, 0))],
        out_specs=[pl.BlockSpec((tm, LANE), lambda i: (i, 0)),
                   pl.BlockSpec((1, LANE), lambda i: (0, 0))],
        out_shape=[jax.ShapeDtypeStruct((t, LANE), F32), jax.ShapeDtypeStruct((1, LANE), F32)],
        scratch_shapes=[pltpu.VMEM((1, LANE), F32)],
        compiler_params=_params("arbitrary"),
        name="route",
    )(logits)


def _rowmap_kernel(eid_ref, rank_ref, start_ref, asg_ref, *, n_rows, n_asg):
    def fill(r, c):
        asg_ref[r] = 0
        return c

    lax.fori_loop(0, n_rows, fill, 0, unroll=ISSUE_UNROLL)

    def place(a, c):
        asg_ref[start_ref[eid_ref[a]] + rank_ref[a]] = a
        return c

    lax.fori_loop(0, n_asg, place, 0, unroll=ISSUE_UNROLL)


def _rowmap(eid, rank, start, n_rows):
    n_asg = eid.shape[0]
    smem = pl.BlockSpec(memory_space=pltpu.SMEM)
    return pl.pallas_call(
        functools.partial(_rowmap_kernel, n_rows=n_rows, n_asg=n_asg),
        in_specs=[smem, smem, smem],
        out_specs=smem,
        out_shape=jax.ShapeDtypeStruct((n_rows,), jnp.int32),
        name="rowmap",
    )(eid, rank, start)


def _moe_kernel(be_ref, row0_ref, nsub_ref, nreal_ref, asg_ref, h1_hbm, wg_ref, wu_ref, wd_ref, y_hbm,
                xg_ref, xb_ref, yacc_ref, gsem, ssem, *, nb, nf, sub, tok_mask):
    b = pl.program_id(0)
    f = pl.program_id(1)
    nsub = nsub_ref[b]

    def gather_start(blk):
        base = row0_ref[blk]

        def row(r):
            tok = asg_ref[base + r] & tok_mask
            pltpu.make_async_copy(h1_hbm.at[pl.ds(tok, 1)], xg_ref.at[pl.ds(r, 1)], gsem).start()

        def body(q, c):
            for k in range(ISSUE_UNROLL):
                row(q * ISSUE_UNROLL + k)
            return c

        lax.fori_loop(0, nsub_ref[blk] * (sub // ISSUE_UNROLL), body, 0)

    def gather_wait(blk):
        def body(s, c):
            pltpu.make_async_copy(h1_hbm.at[pl.ds(0, sub)], xg_ref.at[pl.ds(s * sub, sub)], gsem).wait()
            return c

        lax.fori_loop(0, nsub_ref[blk], body, 0)

    def scatter_start(blk):
        base = row0_ref[blk]

        def row(r):
            dst = asg_ref[base + r]
            pltpu.make_async_copy(yacc_ref.at[pl.ds(r, 1)], y_hbm.at[pl.ds(dst, 1)], ssem).start()

        def body(q, c):
            for k in range(ISSUE_UNROLL):
                row(q * ISSUE_UNROLL + k)
            return c

        groups = nreal_ref[blk] // ISSUE_UNROLL
        lax.fori_loop(0, groups, body, 0)

        def tail(r, c):
            row(r)
            return c

        lax.fori_loop(groups * ISSUE_UNROLL, nreal_ref[blk], tail, 0)

    def scatter_wait(blk):
        full = nreal_ref[blk] // sub

        def tile(s, c):
            pltpu.make_async_copy(yacc_ref.at[pl.ds(s * sub, sub)], y_hbm.at[pl.ds(0, sub)], ssem).wait()
            return c

        lax.fori_loop(0, full, tile, 0)

        def row(r, c):
            pltpu.make_async_copy(yacc_ref.at[pl.ds(0, 1)], y_hbm.at[pl.ds(0, 1)], ssem).wait()
            return c

        lax.fori_loop(full * sub, nreal_ref[blk], row, 0)

    @pl.when((f == 0) & (b == 0))
    def _():
        yacc_ref[...] = jnp.zeros(yacc_ref.shape, F32)

    @pl.when((f == 0) & (b > 0) & (nsub == 0))
    def _():
        scatter_wait(jnp.maximum(b - 1, 0))

    @pl.when(nsub > 0)
    def _():
        @pl.when(f == 0)
        def _():
            @pl.when(b == 0)
            def _():
                gather_start(0)

            gather_wait(b)

            def cast(s, c):
                rows = pl.ds(pl.multiple_of(s * sub, sub), sub)
                xb_ref[rows, :] = xg_ref[rows, :].astype(BF16)
                return c

            lax.fori_loop(0, nsub, cast, 0)

            @pl.when(b + 1 < nb)
            def _():
                gather_start(jnp.minimum(b + 1, nb - 1))

        def compute(m):
            rows = slice(0, m * sub)
            xs = xb_ref[rows, :]
            hb = (jax.nn.silu(_dot(xs, wg_ref[...])) * _dot(xs, wu_ref[...])).astype(BF16)

            @pl.when((f == 0) & (b > 0))
            def _():
                scatter_wait(jnp.maximum(b - 1, 0))

            yp = _dot(hb, wd_ref[...])
            yacc_ref[rows, :] = jnp.where(f == 0, yp, yacc_ref[rows, :] + yp)

        for m in range(1, xb_ref.shape[0] // sub + 1):
            pl.when(nsub == m)(functools.partial(compute, m))

        @pl.when(f == nf - 1)
        def _():
            scatter_start(b)

            @pl.when(b == nb - 1)
            def _():
                scatter_wait(b)


def _moe(be, row0, nsub, nreal, asg, h1, w_gate, w_up, w_down, n_out_rows):
    t, d = h1.shape
    ne, _, de = w_gate.shape
    nb = be.shape[0]
    nf = de // MOE_TF
    assert de % MOE_TF == 0 and t & (t - 1) == 0
    last = nf - 1
    fsel = lambda f, n: jnp.where(n > 0, f, last)
    grid_spec = pltpu.PrefetchScalarGridSpec(
        num_scalar_prefetch=5,
        grid=(nb, nf),
        in_specs=[
            pl.BlockSpec(memory_space=pl.ANY),
            pl.BlockSpec((None, d, MOE_TF), lambda b, f, be, r0, ns, nr, asg: (be[b], 0, fsel(f, ns[b]))),
            pl.BlockSpec((None, d, MOE_TF), lambda b, f, be, r0, ns, nr, asg: (be[b], 0, fsel(f, ns[b]))),
            pl.BlockSpec((None, MOE_TF, d), lambda b, f, be, r0, ns, nr, asg: (be[b], fsel(f, ns[b]), 0)),
        ],
        out_specs=pl.BlockSpec(memory_space=pl.ANY),
        scratch_shapes=[pltpu.VMEM((MOE_ROWS, d), F32),
                        pltpu.VMEM((MOE_ROWS, d), BF16),
                        pltpu.VMEM((MOE_ROWS, d), F32),
                        pltpu.SemaphoreType.DMA,
                        pltpu.SemaphoreType.DMA],
    )
    return pl.pallas_call(
        functools.partial(_moe_kernel, nb=nb, nf=nf, sub=MOE_SUB, tok_mask=t - 1),
        grid_spec=grid_spec,
        out_shape=jax.ShapeDtypeStruct((n_out_rows, d), F32),
        compiler_params=_params("arbitrary", "arbitrary", vmem=MOE_VMEM_LIMIT),
        name="moe",
    )(be, row0, nsub, nreal, asg, h1, w_gate, w_up, w_down)


def _combine_gate_kernel(h1_ref, y0_ref, y1_ref, pk_ref, g_ref, b_ref, wpg_ref, bpg_ref, p_ref, wpe_ref,
                         o_ref, z_ref, hb_ref, *, nj, tn, alpha):
    i = pl.program_id(0)
    j = pl.program_id(1)
    slot = i % 2
    prev = 1 - slot

    @pl.when((i == 0) & (j == 0))
    def _():
        z_ref[1] = jnp.zeros(z_ref.shape[1:], F32)
        hb_ref[1] = jnp.zeros(hb_ref.shape[1:], BF16)

    pk = pk_ref[...]
    ffn = pk[:, 2:3] * y0_ref[...] + pk[:, 3:4] * y1_ref[...]
    z_ref[slot, j] = alpha * h1_ref[...] + ffn

    gate = jax.nn.sigmoid(_dot(hb_ref[prev], wpg_ref[...]) + bpg_ref[...])
    ple = _dot(p_ref[...].astype(BF16), wpe_ref[...])
    o_ref[...] = z_ref[prev, j] + gate * ple

    @pl.when(j == nj - 1)
    def _():
        d = nj * tn
        mu = sum(jnp.sum(z_ref[slot, jj], axis=-1, keepdims=True) for jj in range(nj)) / d
        var = sum(jnp.sum(jnp.square(z_ref[slot, jj] - mu), axis=-1, keepdims=True) for jj in range(nj)) / d
        rstd = lax.rsqrt(var + LN_EPS)
        for jj in range(nj):
            cols = slice(jj * tn, (jj + 1) * tn)
            h = (z_ref[slot, jj] - mu) * rstd * g_ref[:, cols] + b_ref[:, cols]
            z_ref[slot, jj] = h
            hb_ref[slot, :, cols] = h.astype(BF16)


def _combine_gate(h1, y2, pk, g, b, w_pg_b, b_pg, p2, w_pe_b, alpha):
    t, d = h1.shape
    pd = p2.shape[1]
    tm, tn = MIX_TM, OUT_TN
    assert t % tm == 0 and d % tn == 0
    ni, nj = t // tm, d // tn
    row_in = lambda i: jnp.minimum(i, ni - 1)
    col_in = lambda i, j: jnp.where(i < ni, j, nj - 1)
    row_out = lambda i: jnp.maximum(i - 1, 0)
    return pl.pallas_call(
        functools.partial(_combine_gate_kernel, nj=nj, tn=tn, alpha=alpha),
        grid=(ni + 1, nj),
        in_specs=[
            pl.BlockSpec((tm, tn), lambda i, j: (row_in(i), col_in(i, j))),
            pl.BlockSpec((tm, tn), lambda i, j: (row_in(i), col_in(i, j))),
            pl.BlockSpec((tm, tn), lambda i, j: (row_in(i) + ni, col_in(i, j))),
            pl.BlockSpec((tm, LANE), lambda i, j: (row_in(i), 0)),
            pl.BlockSpec((1, d), lambda i, j: (0, 0)),
            pl.BlockSpec((1, d), lambda i, j: (0, 0)),
            pl.BlockSpec((d, tn), lambda i, j: (0, j)),
            pl.BlockSpec((1, tn), lambda i, j: (0, j)),
            pl.BlockSpec((tm, pd), lambda i, j: (row_out(i), 0)),
            pl.BlockSpec((pd, tn), lambda i, j: (0, j)),
        ],
        out_specs=pl.BlockSpec((tm, tn), lambda i, j: (row_out(i), jnp.where(i == 0, 0, j))),
        out_shape=jax.ShapeDtypeStruct((t, d), F32),
        scratch_shapes=[pltpu.VMEM((2, nj, tm, tn), F32), pltpu.VMEM((2, tm, d), BF16)],
        compiler_params=_params("arbitrary", "arbitrary"),
        name="combine_gate",
    )(h1, y2, y2, pk, g, b, w_pg_b, b_pg, p2, w_pe_b)


def _block_table(counts, n_asg):
    ne = counts.shape[0]
    padded = ((counts + MOE_SUB - 1) // MOE_SUB) * MOE_SUB
    pad_end = jnp.cumsum(padded)
    pad_start = pad_end - padded
    nblk = (padded + MOE_ROWS - 1) // MOE_ROWS
    blk_end = jnp.cumsum(nblk)
    blk_start = blk_end - nblk
    nb = ne + -(-n_asg // MOE_ROWS)
    bidx = jnp.arange(nb, dtype=jnp.int32)
    used = bidx < blk_end[-1]
    be = jnp.minimum(jnp.searchsorted(blk_end, jnp.minimum(bidx, blk_end[-1] - 1), side="right"), ne - 1)
    be = be.astype(jnp.int32)
    within = bidx - blk_start[be]
    row0 = pad_start[be] + within * MOE_ROWS
    nsub = jnp.clip((padded[be] - within * MOE_ROWS) // MOE_SUB, 0, MOE_ROWS // MOE_SUB)
    nsub = jnp.where(used, nsub, 0)
    row0 = jnp.where(used, row0, 0)
    nreal = jnp.where(used, jnp.clip(counts[be] - within * MOE_ROWS, 0, MOE_ROWS), 0)
    i32 = lambda a: a.astype(jnp.int32)
    return i32(pad_start), be, i32(row0), i32(nsub), i32(nreal)


def kernel(x, p, w_in, conv_w, sg_ln_g, sg_ln_b, sg_w, sg_b, w_out, ln1_g, ln1_b, w_rg, b_rg, w_re, b_re,
           w_gate, w_up, w_down, ln2_g, ln2_b, w_pg, b_pg, w_pe):
    depth = w_in.shape[0]
    bsz, seq, d = x.shape
    t = bsz * seq
    alpha = (2 * depth) ** 0.25
    n_groups = w_rg.shape[-1]
    ne = w_re.shape[-1]
    per_group = ne // n_groups
    n_asg = t * TOP_K
    n_rows = ((n_asg + ne * (MOE_SUB - 1) + MOE_SUB - 1) // MOE_SUB) * MOE_SUB

    h = x.reshape(t, d)
    for i in range(depth):
        cdim = conv_w.shape[-1]
        nj = cdim // MIX_TN
        w_in_p = _regroup_cols(w_in[i], 5, nj, MIX_TN)
        w_out_p = _regroup_rows(w_out[i], 2, nj, MIX_TN)
        y = _mixer(h, w_in_p, conv_w[i], sg_ln_g[i].reshape(1, -1), sg_ln_b[i].reshape(1, -1),
                   sg_w[i], sg_b[i][:, :, None], seq)
        wr = jnp.concatenate([w_rg[i], w_re[i], jnp.zeros((d, LANE - n_groups - ne), F32)], axis=1).astype(BF16)
        br = jnp.concatenate([b_rg[i], b_re[i], jnp.zeros((LANE - n_groups - ne,), F32)]).reshape(1, LANE)
        h1, logits = _outproj(y, w_out_p, h, ln1_g[i].reshape(1, d), ln1_b[i].reshape(1, d), wr, br, alpha)
        pk, cnt = _route(logits, n_groups, per_group)
        pad_start, be, row0, nsub, nreal = _block_table(cnt[0, :ne].astype(jnp.int32), n_asg)
        eid = jnp.concatenate([pk[:, 0], pk[:, 1]]).astype(jnp.int32)
        rank = jnp.concatenate([pk[:, 4], pk[:, 5]]).astype(jnp.int32)
        asg = _rowmap(eid, rank, pad_start, n_rows)
        y2 = _moe(be, row0, nsub, nreal, asg, h1, w_gate[i], w_up[i], w_down[i], n_asg)
        h = _combine_gate(h1, y2, pk, ln2_g[i].reshape(1, d), ln2_b[i].reshape(1, d), w_pg[i].astype(BF16),
                          b_pg[i].reshape(1, d), p[i].reshape(t, -1), w_pe[i].astype(BF16), alpha)
    return h.reshape(bsz, seq, d)
```

```python
import functools

import jax
import jax.numpy as jnp
from jax import lax
from jax.experimental import pallas as pl
from jax.experimental.pallas import tpu as pltpu

F32 = jnp.float32
BF16 = jnp.bfloat16

LANE = 128
SUBLANE = 8
MXU_COL = 256
VMEM_LIMIT = 56 * 1024 * 1024
MOE_VMEM_LIMIT = 60 * 1024 * 1024

CHUNK = 64
SG_BLOCK = 128
LN_EPS = 1e-5
TOP_K = 2

MIX_TM = 512
MIX_TN = MXU_COL
OUT_TN = 512
ROUTE_TM = 512
MOE_SUB = 128
MOE_ROWS = 768
MOE_TF = MXU_COL
ISSUE_UNROLL = 8
ROW_DMA_PRIORITY = 1
CAST_TK = 4096

_dot = functools.partial(jnp.dot, preferred_element_type=F32)


def _params(*sem, vmem=VMEM_LIMIT):
    return pltpu.CompilerParams(dimension_semantics=sem, vmem_limit_bytes=vmem)


def _layer_norm(z, g, b):
    mu = jnp.mean(z, axis=-1, keepdims=True)
    zc = z - mu
    var = jnp.mean(zc * zc, axis=-1, keepdims=True)
    return zc * lax.rsqrt(var + LN_EPS) * g + b


def _cast_kernel(w_ref, o_ref):
    o_ref[...] = w_ref[...].astype(o_ref.dtype)


def _regroup_cols(w, groups, nj, tn):
    d = w.shape[0]
    tk = min(CAST_TK, d)
    assert d % tk == 0 and w.shape[1] == groups * nj * tn
    return pl.pallas_call(
        _cast_kernel,
        grid=(nj, groups, d // tk),
        in_specs=[pl.BlockSpec((tk, tn), lambda j, s, k: (k, s * nj + j))],
        out_specs=pl.BlockSpec((tk, tn), lambda j, s, k: (k, j * groups + s)),
        out_shape=jax.ShapeDtypeStruct(w.shape, BF16),
        compiler_params=_params("arbitrary", "arbitrary", "arbitrary"),
        name="regroup_cols",
    )(w)


def _regroup_rows(w, groups, nj, tn):
    d = w.shape[1]
    assert w.shape[0] == groups * nj * tn
    return pl.pallas_call(
        _cast_kernel,
        grid=(nj, groups),
        in_specs=[pl.BlockSpec((tn, d), lambda j, s: (s * nj + j, 0))],
        out_specs=pl.BlockSpec((tn, d), lambda j, s: (j * groups + s, 0)),
        out_shape=jax.ShapeDtypeStruct(w.shape, BF16),
        compiler_params=_params("arbitrary", "arbitrary"),
        name="regroup_rows",
    )(w)


def _mixer_kernel(x_ref, w_ref, cw_ref, lng_ref, lnb_ref, sgw_ref, sgb_ref, y_ref,
                  xb_ref, pa_ref, pb_ref, gbuf_ref, carry_ref, *, tm, tn, nj, n_steps, tiles_per_seq):
    n = pl.program_id(0)

    @pl.when(n == 0)
    def _():
        pb_ref[...] = jnp.zeros(pb_ref.shape, F32)
        carry_ref[...] = jnp.zeros(carry_ref.shape, F32)

    @pl.when((n % nj == 0) & (n < n_steps))
    def _():
        xb_ref[...] = x_ref[...].astype(BF16)

    m = jnp.maximum(n - 1, 0)
    jp = m % nj
    first = ((m // nj) % tiles_per_seq) == 0

    def mix(proj_ref):
        b, c, h, u, v = (proj_ref[:, k * tn:(k + 1) * tn] for k in range(5))
        g = c * h
        gbuf_ref[0:SUBLANE, :] = jnp.where(first, 0.0, carry_ref[jp])
        gbuf_ref[SUBLANE:, :] = g
        carry_ref[jp] = g[tm - SUBLANE:, :]
        g1 = gbuf_ref[pl.ds(SUBLANE - 1, tm), :]
        g2 = gbuf_ref[pl.ds(SUBLANE - 2, tm), :]
        cw = cw_ref[...]
        conv = cw[0:1, :] * g2 + cw[1:2, :] * g1 + cw[2:3, :] * g
        y_ref[:, 0:tn] = (b * conv).astype(y_ref.dtype)

        pos_i = lax.broadcasted_iota(jnp.int32, (SG_BLOCK, SG_BLOCK), 0)
        pos_j = lax.broadcasted_iota(jnp.int32, (SG_BLOCK, SG_BLOCK), 1)
        mask = (pos_j // CHUNK) <= (pos_i // CHUNK)
        for hh in range(tn // LANE):
            sl = slice(hh * LANE, (hh + 1) * LANE)
            vn = _layer_norm(jax.nn.gelu(v[:, sl]), lng_ref[:, sl], lnb_ref[:, sl]).astype(BF16)
            gu = jax.nn.gelu(u[:, sl])
            ws = jnp.where(mask, sgw_ref[hh], 0.0).astype(BF16)
            bcol = sgb_ref[hh]
            for r in range(tm // SG_BLOCK):
                rows = slice(r * SG_BLOCK, (r + 1) * SG_BLOCK)
                sg = _dot(ws, vn[rows, :]) + bcol
                y_ref[rows, tn + hh * LANE:tn + (hh + 1) * LANE] = (gu[rows, :] * sg).astype(y_ref.dtype)

    def step(store_ref, load_ref):
        store_ref[...] = _dot(xb_ref[...], w_ref[...])
        mix(load_ref)

    pl.when(n % 2 == 0)(functools.partial(step, pa_ref, pb_ref))
    pl.when(n % 2 == 1)(functools.partial(step, pb_ref, pa_ref))


def _mixer(x2, w_in_p, conv_w, lng, lnb, sg_w, sg_bcol, seq):
    t, d = x2.shape
    cdim = conv_w.shape[1]
    tm, tn = MIX_TM, MIX_TN
    nj = cdim // tn
    assert seq % tm == 0 and t % tm == 0 and cdim % tn == 0 and w_in_p.shape[1] == 5 * cdim
    hp = tn // LANE
    n_steps = (t // tm) * nj
    cur = lambda n: jnp.minimum(n, n_steps - 1)
    prv = lambda n: jnp.maximum(n - 1, 0)
    return pl.pallas_call(
        functools.partial(_mixer_kernel, tm=tm, tn=tn, nj=nj, n_steps=n_steps, tiles_per_seq=seq // tm),
        grid=(n_steps + 1,),
        in_specs=[
            pl.BlockSpec((tm, d), lambda n: (cur(n) // nj, 0)),
            pl.BlockSpec((d, 5 * tn), lambda n: (0, cur(n) % nj)),
            pl.BlockSpec((conv_w.shape[0], tn), lambda n: (0, prv(n) % nj)),
            pl.BlockSpec((1, tn), lambda n: (0, prv(n) % nj)),
            pl.BlockSpec((1, tn), lambda n: (0, prv(n) % nj)),
            pl.BlockSpec((hp, SG_BLOCK, SG_BLOCK), lambda n: (prv(n) % nj, 0, 0)),
            pl.BlockSpec((hp, SG_BLOCK, 1), lambda n: (prv(n) % nj, 0, 0)),
        ],
        out_specs=pl.BlockSpec((tm, 2 * tn), lambda n: (prv(n) // nj, prv(n) % nj)),
        out_shape=jax.ShapeDtypeStruct((t, 2 * cdim), BF16),
        scratch_shapes=[pltpu.VMEM((tm, d), BF16),
                        pltpu.VMEM((tm, 5 * tn), F32),
                        pltpu.VMEM((tm, 5 * tn), F32),
                        pltpu.VMEM((tm + SUBLANE, tn), F32),
                        pltpu.VMEM((nj, SUBLANE, tn), F32)],
        compiler_params=_params("arbitrary"),
        name="mixer",
    )(x2, w_in_p, conv_w, lng, lnb, sg_w, sg_bcol)


def _outproj_kernel(y_ref, w_ref, x_ref, g_ref, b_ref, wr_ref, br_ref, h1_ref, lg_ref, z_ref, *, tn, nj, alpha):
    j = pl.program_id(1)
    z_ref[j] = alpha * x_ref[...] + _dot(y_ref[...], w_ref[...])

    @pl.when(j == nj - 1)
    def _():
        d = nj * tn
        mu = sum(jnp.sum(z_ref[jj], axis=-1, keepdims=True) for jj in range(nj)) / d
        var = sum(jnp.sum(jnp.square(z_ref[jj] - mu), axis=-1, keepdims=True) for jj in range(nj)) / d
        rstd = lax.rsqrt(var + LN_EPS)
        logits = br_ref[...]
        for jj in range(nj):
            cols = slice(jj * tn, (jj + 1) * tn)
            h = (z_ref[jj] - mu) * rstd * g_ref[:, cols] + b_ref[:, cols]
            h1_ref[:, cols] = h
            logits = logits + _dot(h.astype(BF16), wr_ref[cols, :])
        lg_ref[...] = logits


def _outproj(y, w_out_p, x2, g, b, wr, br, alpha):
    t, d = x2.shape
    kdim = y.shape[1]
    tm, tn = MIX_TM, OUT_TN
    nj = d // tn
    assert t % tm == 0 and d % tn == 0 and w_out_p.shape[0] == kdim
    return pl.pallas_call(
        functools.partial(_outproj_kernel, tn=tn, nj=nj, alpha=alpha),
        grid=(t // tm, nj),
        in_specs=[
            pl.BlockSpec((tm, kdim), lambda i, j: (i, 0)),
            pl.BlockSpec((kdim, tn), lambda i, j: (0, j)),
            pl.BlockSpec((tm, tn), lambda i, j: (i, j)),
            pl.BlockSpec((1, d), lambda i, j: (0, 0)),
            pl.BlockSpec((1, d), lambda i, j: (0, 0)),
            pl.BlockSpec((d, LANE), lambda i, j: (0, 0)),
            pl.BlockSpec((1, LANE), lambda i, j: (0, 0)),
        ],
        out_specs=[pl.BlockSpec((tm, d), lambda i, j: (i, 0)),
                   pl.BlockSpec((tm, LANE), lambda i, j: (i, 0))],
        out_shape=[jax.ShapeDtypeStruct((t, d), F32), jax.ShapeDtypeStruct((t, LANE), F32)],
        scratch_shapes=[pltpu.VMEM((nj, tm, tn), F32)],
        compiler_params=_params("arbitrary", "arbitrary"),
        name="outproj",
    )(y, w_out_p, x2, g, b, wr, br)


def _route_kernel(lg_ref, pk_ref, cnt_ref, run_ref, *, n_groups, per_group):
    i = pl.program_id(0)

    @pl.when(i == 0)
    def _():
        run_ref[...] = jnp.zeros_like(run_ref)

    l = lg_ref[...]
    tm = l.shape[0]
    lane = lax.broadcasted_iota(jnp.int32, l.shape, 1)
    neg = jnp.float32(-jnp.inf)
    gmask = lane < n_groups
    gl = jnp.where(gmask, l, neg)
    gmax = jnp.max(gl, axis=-1, keepdims=True)
    gi = jnp.min(jnp.where(gl == gmax, lane, LANE), axis=-1, keepdims=True)
    gp = 1.0 / jnp.sum(jnp.where(gmask, jnp.exp(l - gmax), 0.0), axis=-1, keepdims=True)
    elane = lane - n_groups
    emask = (elane >= 0) & (elane // per_group == gi)
    el = jnp.where(emask, l, neg)
    m1 = jnp.max(el, axis=-1, keepdims=True)
    i1 = jnp.min(jnp.where(el == m1, lane, LANE), axis=-1, keepdims=True)
    el2 = jnp.where(lane == i1, neg, el)
    m2 = jnp.max(el2, axis=-1, keepdims=True)
    i2 = jnp.min(jnp.where(el2 == m2, lane, LANE), axis=-1, keepdims=True)
    t2 = jnp.exp(m2 - m1)
    w0 = gp / (1.0 + t2)
    w1 = gp * t2 / (1.0 + t2)
    e0 = i1 - n_groups
    e1 = i2 - n_groups
    oh0 = lane == e0
    oh1 = lane == e1
    oh = (oh0 | oh1).astype(BF16)
    row = lax.broadcasted_iota(jnp.int32, (tm, tm), 0)
    col = lax.broadcasted_iota(jnp.int32, (tm, tm), 1)
    tri = (row > col).astype(BF16)
    before = run_ref[...] + _dot(tri, oh)
    r0 = jnp.sum(jnp.where(oh0, before, 0.0), axis=-1, keepdims=True)
    r1 = jnp.sum(jnp.where(oh1, before, 0.0), axis=-1, keepdims=True)
    total = run_ref[...] + jnp.sum(oh.astype(F32), axis=0, keepdims=True)
    run_ref[...] = total
    cnt_ref[...] = total
    vals = (e0.astype(F32), e1.astype(F32), w0, w1, r0, r1)
    pk = jnp.zeros(l.shape, F32)
    for k, val in enumerate(vals):
        pk = jnp.where(lane == k, val, pk)
    pk_ref[...] = pk


def _route(logits, n_groups, per_group):
    t = logits.shape[0]
    tm = ROUTE_TM
    assert t % tm == 0 and n_groups * (1 + per_group) <= LANE
    return pl.pallas_call(
        functools.partial(_route_kernel, n_groups=n_groups, per_group=per_group),
        grid=(t // tm,),
        in_specs=[pl.BlockSpec((tm, LANE), lambda i: (i, 0))],
        out_specs=[pl.BlockSpec((tm, LANE), lambda i: (i, 0)),
                   pl.BlockSpec((1, LANE), lambda i: (0, 0))],
        out_shape=[jax.ShapeDtypeStruct((t, LANE), F32), jax.ShapeDtypeStruct((1, LANE), F32)],
        scratch_shapes=[pltpu.VMEM((1, LANE), F32)],
        compiler_params=_params("arbitrary"),
        name="route",
    )(logits)


def _rowmap_kernel(eid_ref, rank_ref, start_ref, asg_ref, *, n_rows, n_asg):
    def fill(r, c):
        asg_ref[r] = 0
        return c

    lax.fori_loop(0, n_rows, fill, 0, unroll=ISSUE_UNROLL)

    def place(a, c):
        asg_ref[start_ref[eid_ref[a]] + rank_ref[a]] = a
        return c

    lax.fori_loop(0, n_asg, place, 0, unroll=ISSUE_UNROLL)


def _rowmap(eid, rank, start, n_rows):
    n_asg = eid.shape[0]
    smem = pl.BlockSpec(memory_space=pltpu.SMEM)
    return pl.pallas_call(
        functools.partial(_rowmap_kernel, n_rows=n_rows, n_asg=n_asg),
        in_specs=[smem, smem, smem],
        out_specs=smem,
        out_shape=jax.ShapeDtypeStruct((n_rows,), jnp.int32),
        name="rowmap",
    )(eid, rank, start)


def _moe_kernel(be_ref, row0_ref, nsub_ref, nreal_ref, asg_ref, h1_hbm, wg_ref, wu_ref, wd_ref, y_hbm,
                xg_ref, xb_ref, yacc_ref, issued_ref, gsem, ssem, *, nb, nf, sub, tok_mask):
    b = pl.program_id(0)
    f = pl.program_id(1)
    nsub = nsub_ref[b]

    def gather_row(base, r):
        tok = asg_ref[base + r] & tok_mask
        pltpu.make_async_copy(h1_hbm.at[pl.ds(tok, 1)], xg_ref.at[pl.ds(r, 1)], gsem).start(priority=ROW_DMA_PRIORITY)

    def scatter_row(base, r):
        dst = asg_ref[base + r]
        pltpu.make_async_copy(yacc_ref.at[pl.ds(r, 1)], y_hbm.at[pl.ds(dst, 1)], ssem).start(priority=ROW_DMA_PRIORITY)

    def issue_range(row_fn, base, lo, hi):
        groups = (hi - lo) // ISSUE_UNROLL

        def body(q, c):
            for k in range(ISSUE_UNROLL):
                row_fn(base, lo + q * ISSUE_UNROLL + k)
            return c

        lax.fori_loop(0, groups, body, 0)

        def tail(r, c):
            row_fn(base, r)
            return c

        lax.fori_loop(lo + groups * ISSUE_UNROLL, hi, tail, 0)

    def wait_rows(n, tile_copy, row_copy):
        def tile(s, c):
            tile_copy.wait()
            return c

        lax.fori_loop(0, n // sub, tile, 0)

        def row(r, c):
            row_copy.wait()
            return c

        lax.fori_loop((n // sub) * sub, n, row, 0)

    def gather_wait(n):
        wait_rows(n, pltpu.make_async_copy(h1_hbm.at[pl.ds(0, sub)], xg_ref.at[pl.ds(0, sub)], gsem),
                  pltpu.make_async_copy(h1_hbm.at[pl.ds(0, 1)], xg_ref.at[pl.ds(0, 1)], gsem))

    def scatter_wait(blk):
        wait_rows(nreal_ref[blk], pltpu.make_async_copy(yacc_ref.at[pl.ds(0, sub)], y_hbm.at[pl.ds(0, sub)], ssem),
                  pltpu.make_async_copy(yacc_ref.at[pl.ds(0, 1)], y_hbm.at[pl.ds(0, 1)], ssem))

    @pl.when((f == 0) & (b == 0))
    def _():
        yacc_ref[...] = jnp.zeros(yacc_ref.shape, F32)
        issued_ref[0] = 0

    @pl.when(f == 0)
    def _():
        have = issued_ref[0]
        need = nsub * sub

        @pl.when(need > have)
        def _():
            issue_range(gather_row, row0_ref[b], have, need)

        gather_wait(jnp.maximum(have, need))
        issued_ref[0] = 0

    @pl.when((f == 0) & (b > 0) & (nsub == 0))
    def _():
        scatter_wait(jnp.maximum(b - 1, 0))

    @pl.when(nsub > 0)
    def _():
        @pl.when(f == 0)
        def _():
            def cast(s, c):
                rows = pl.ds(pl.multiple_of(s * sub, sub), sub)
                xb_ref[rows, :] = xg_ref[rows, :].astype(BF16)
                return c

            lax.fori_loop(0, nsub, cast, 0)

        def compute(m):
            share = (m * sub // ISSUE_UNROLL // nf) * ISSUE_UNROLL
            nxt_base = row0_ref[jnp.minimum(b + 1, nb - 1)]
            first = f * share
            for r in range(share):
                gather_row(nxt_base, first + r)
            issued_ref[0] = issued_ref[0] + share

            rows = slice(0, m * sub)
            xs = xb_ref[rows, :]
            hb = (jax.nn.silu(_dot(xs, wg_ref[...])) * _dot(xs, wu_ref[...])).astype(BF16)

            @pl.when((f == 0) & (b > 0))
            def _():
                scatter_wait(jnp.maximum(b - 1, 0))

            yp = _dot(hb, wd_ref[...])
            yacc_ref[rows, :] = jnp.where(f == 0, yp, yacc_ref[rows, :] + yp)

        for m in range(1, xb_ref.shape[0] // sub + 1):
            pl.when(nsub == m)(functools.partial(compute, m))

        @pl.when(f == nf - 1)
        def _():
            issue_range(scatter_row, row0_ref[b], 0, nreal_ref[b])

            @pl.when(b == nb - 1)
            def _():
                scatter_wait(b)
                gather_wait(issued_ref[0])
                issued_ref[0] = 0


def _moe(be, row0, nsub, nreal, asg, h1, w_gate, w_up, w_down, n_out_rows):
    t, d = h1.shape
    ne, _, de = w_gate.shape
    nb = be.shape[0]
    nf = de // MOE_TF
    assert de % MOE_TF == 0 and t & (t - 1) == 0
    last = nf - 1
    fsel = lambda f, n: jnp.where(n > 0, f, last)
    grid_spec = pltpu.PrefetchScalarGridSpec(
        num_scalar_prefetch=5,
        grid=(nb, nf),
        in_specs=[
            pl.BlockSpec(memory_space=pl.ANY),
            pl.BlockSpec((None, d, MOE_TF), lambda b, f, be, r0, ns, nr, asg: (be[b], 0, fsel(f, ns[b]))),
            pl.BlockSpec((None, d, MOE_TF), lambda b, f, be, r0, ns, nr, asg: (be[b], 0, fsel(f, ns[b]))),
            pl.BlockSpec((None, MOE_TF, d), lambda b, f, be, r0, ns, nr, asg: (be[b], fsel(f, ns[b]), 0)),
        ],
        out_specs=pl.BlockSpec(memory_space=pl.ANY),
        scratch_shapes=[pltpu.VMEM((MOE_ROWS, d), F32),
                        pltpu.VMEM((MOE_ROWS, d), BF16),
                        pltpu.VMEM((MOE_ROWS, d), F32),
                        pltpu.SMEM((1,), jnp.int32),
                        pltpu.SemaphoreType.DMA,
                        pltpu.SemaphoreType.DMA],
    )
    return pl.pallas_call(
        functools.partial(_moe_kernel, nb=nb, nf=nf, sub=MOE_SUB, tok_mask=t - 1),
        grid_spec=grid_spec,
        out_shape=jax.ShapeDtypeStruct((n_out_rows, d), F32),
        compiler_params=_params("arbitrary", "arbitrary", vmem=MOE_VMEM_LIMIT),
        name="moe",
    )(be, row0, nsub, nreal, asg, h1, w_gate, w_up, w_down)


def _combine_gate_kernel(h1_ref, y0_ref, y1_ref, pk_ref, g_ref, b_ref, wpg_ref, bpg_ref, p_ref, wpe_ref,
                         o_ref, z_ref, hb_ref, *, nj, tn, alpha):
    i = pl.program_id(0)
    j = pl.program_id(1)
    slot = i % 2
    prev = 1 - slot

    @pl.when((i == 0) & (j == 0))
    def _():
        z_ref[1] = jnp.zeros(z_ref.shape[1:], F32)
        hb_ref[1] = jnp.zeros(hb_ref.shape[1:], BF16)

    pk = pk_ref[...]
    ffn = pk[:, 2:3] * y0_ref[...] + pk[:, 3:4] * y1_ref[...]
    z_ref[slot, j] = alpha * h1_ref[...] + ffn

    gate = jax.nn.sigmoid(_dot(hb_ref[prev], wpg_ref[...]) + bpg_ref[...])
    ple = _dot(p_ref[...].astype(BF16), wpe_ref[...])
    o_ref[...] = z_ref[prev, j] + gate * ple

    @pl.when(j == nj - 1)
    def _():
        d = nj * tn
        mu = sum(jnp.sum(z_ref[slot, jj], axis=-1, keepdims=True) for jj in range(nj)) / d
        var = sum(jnp.sum(jnp.square(z_ref[slot, jj] - mu), axis=-1, keepdims=True) for jj in range(nj)) / d
        rstd = lax.rsqrt(var + LN_EPS)
        for jj in range(nj):
            cols = slice(jj * tn, (jj + 1) * tn)
            h = (z_ref[slot, jj] - mu) * rstd * g_ref[:, cols] + b_ref[:, cols]
            z_ref[slot, jj] = h
            hb_ref[slot, :, cols] = h.astype(BF16)


def _combine_gate(h1, y2, pk, g, b, w_pg_b, b_pg, p2, w_pe_b, alpha):
    t, d = h1.shape
    pd = p2.shape[1]
    tm, tn = MIX_TM, OUT_TN
    assert t % tm == 0 and d % tn == 0
    ni, nj = t // tm, d // tn
    row_in = lambda i: jnp.minimum(i, ni - 1)
    col_in = lambda i, j: jnp.where(i < ni, j, nj - 1)
    row_out = lambda i: jnp.maximum(i - 1, 0)
    return pl.pallas_call(
        functools.partial(_combine_gate_kernel, nj=nj, tn=tn, alpha=alpha),
        grid=(ni + 1, nj),
        in_specs=[
            pl.BlockSpec((tm, tn), lambda i, j: (row_in(i), col_in(i, j))),
            pl.BlockSpec((tm, tn), lambda i, j: (row_in(i), col_in(i, j))),
            pl.BlockSpec((tm, tn), lambda i, j: (row_in(i) + ni, col_in(i, j))),
            pl.BlockSpec((tm, LANE), lambda i, j: (row_in(i), 0)),
            pl.BlockSpec((1, d), lambda i, j: (0, 0)),
            pl.BlockSpec((1, d), lambda i, j: (0, 0)),
            pl.BlockSpec((d, tn), lambda i, j: (0, j)),
            pl.BlockSpec((1, tn), lambda i, j: (0, j)),
            pl.BlockSpec((tm, pd), lambda i, j: (row_out(i), 0)),
            pl.BlockSpec((pd, tn), lambda i, j: (0, j)),
        ],
        out_specs=pl.BlockSpec((tm, tn), lambda i, j: (row_out(i), jnp.where(i == 0, 0, j))),
        out_shape=jax.ShapeDtypeStruct((t, d), F32),
        scratch_shapes=[pltpu.VMEM((2, nj, tm, tn), F32), pltpu.VMEM((2, tm, d), BF16)],
        compiler_params=_params("arbitrary", "arbitrary"),
        name="combine_gate",
    )(h1, y2, y2, pk, g, b, w_pg_b, b_pg, p2, w_pe_b)


def _block_table(counts, n_asg):
    ne = counts.shape[0]
    padded = ((counts + MOE_SUB - 1) // MOE_SUB) * MOE_SUB
    pad_end = jnp.cumsum(padded)
    pad_start = pad_end - padded
    nblk = (padded + MOE_ROWS - 1) // MOE_ROWS
    blk_end = jnp.cumsum(nblk)
    blk_start = blk_end - nblk
    nb = ne + -(-n_asg // MOE_ROWS)
    bidx = jnp.arange(nb, dtype=jnp.int32)
    used = bidx < blk_end[-1]
    be = jnp.minimum(jnp.searchsorted(blk_end, jnp.minimum(bidx, blk_end[-1] - 1), side="right"), ne - 1)
    be = be.astype(jnp.int32)
    within = bidx - blk_start[be]
    row0 = pad_start[be] + within * MOE_ROWS
    nsub = jnp.clip((padded[be] - within * MOE_ROWS) // MOE_SUB, 0, MOE_ROWS // MOE_SUB)
    nsub = jnp.where(used, nsub, 0)
    row0 = jnp.where(used, row0, 0)
    nreal = jnp.where(used, jnp.clip(counts[be] - within * MOE_ROWS, 0, MOE_ROWS), 0)
    i32 = lambda a: a.astype(jnp.int32)
    return i32(pad_start), be, i32(row0), i32(nsub), i32(nreal)


def kernel(x, p, w_in, conv_w, sg_ln_g, sg_ln_b, sg_w, sg_b, w_out, ln1_g, ln1_b, w_rg, b_rg, w_re, b_re,
           w_gate, w_up, w_down, ln2_g, ln2_b, w_pg, b_pg, w_pe):
    depth = w_in.shape[0]
    bsz, seq, d = x.shape
    t = bsz * seq
    alpha = (2 * depth) ** 0.25
    n_groups = w_rg.shape[-1]
    ne = w_re.shape[-1]
    per_group = ne // n_groups
    n_asg = t * TOP_K
    n_rows = ((n_asg + ne * (MOE_SUB - 1) + MOE_SUB - 1) // MOE_SUB) * MOE_SUB + MOE_ROWS

    h = x.reshape(t, d)
    for i in range(depth):
        cdim = conv_w.shape[-1]
        nj = cdim // MIX_TN
        w_in_p = _regroup_cols(w_in[i], 5, nj, MIX_TN)
        w_out_p = _regroup_rows(w_out[i], 2, nj, MIX_TN)
        y = _mixer(h, w_in_p, conv_w[i], sg_ln_g[i].reshape(1, -1), sg_ln_b[i].reshape(1, -1),
                   sg_w[i], sg_b[i][:, :, None], seq)
        wr = jnp.concatenate([w_rg[i], w_re[i], jnp.zeros((d, LANE - n_groups - ne), F32)], axis=1).astype(BF16)
        br = jnp.concatenate([b_rg[i], b_re[i], jnp.zeros((LANE - n_groups - ne,), F32)]).reshape(1, LANE)
        h1, logits = _outproj(y, w_out_p, h, ln1_g[i].reshape(1, d), ln1_b[i].reshape(1, d), wr, br, alpha)
        pk, cnt = _route(logits, n_groups, per_group)
        pad_start, be, row0, nsub, nreal = _block_table(cnt[0, :ne].astype(jnp.int32), n_asg)
        eid = jnp.concatenate([pk[:, 0], pk[:, 1]]).astype(jnp.int32)
        rank = jnp.concatenate([pk[:, 4], pk[:, 5]]).astype(jnp.int32)
        asg = _rowmap(eid, rank, pad_start, n_rows)
        y2 = _moe(be, row0, nsub, nreal, asg, h1, w_gate[i], w_up[i], w_down[i], n_asg)
        h = _combine_gate(h1, y2, pk, ln2_g[i].reshape(1, d), ln2_b[i].reshape(1, d), w_pg[i].astype(BF16),
                          b_pg[i].reshape(1, d), p[i].reshape(t, -1), w_pe[i].astype(BF16), alpha)
    return h.reshape(bsz, seq, d)
```

```python
import functools

import jax
import jax.numpy as jnp
from jax import lax
from jax.experimental import pallas as pl
from jax.experimental.pallas import tpu as pltpu

F32 = jnp.float32
BF16 = jnp.bfloat16

LANE = 128
SUBLANE = 8
MXU_COL = 256
VMEM_LIMIT = 56 * 1024 * 1024
MOE_VMEM_LIMIT = 60 * 1024 * 1024

CHUNK = 64
SG_BLOCK = 128
LN_EPS = 1e-5
TOP_K = 2

MIX_TM = 512
MIX_TN = MXU_COL
OUT_TN = 512
ROUTE_TM = 512
MOE_SUB = 128
MOE_ROWS = 768
MOE_TF = MXU_COL
ISSUE_UNROLL = 8
CAST_TK = 4096

_dot = functools.partial(jnp.dot, preferred_element_type=F32)


def _params(*sem, vmem=VMEM_LIMIT):
    return pltpu.CompilerParams(dimension_semantics=sem, vmem_limit_bytes=vmem)


def _layer_norm(z, g, b):
    mu = jnp.mean(z, axis=-1, keepdims=True)
    zc = z - mu
    var = jnp.mean(zc * zc, axis=-1, keepdims=True)
    return zc * lax.rsqrt(var + LN_EPS) * g + b


def _cast_kernel(w_ref, o_ref):
    o_ref[...] = w_ref[...].astype(o_ref.dtype)


def _regroup_cols(w, groups, nj, tn):
    d = w.shape[0]
    tk = min(CAST_TK, d)
    assert d % tk == 0 and w.shape[1] == groups * nj * tn
    return pl.pallas_call(
        _cast_kernel,
        grid=(nj, groups, d // tk),
        in_specs=[pl.BlockSpec((tk, tn), lambda j, s, k: (k, s * nj + j))],
        out_specs=pl.BlockSpec((tk, tn), lambda j, s, k: (k, j * groups + s)),
        out_shape=jax.ShapeDtypeStruct(w.shape, BF16),
        compiler_params=_params("arbitrary", "arbitrary", "arbitrary"),
        name="regroup_cols",
    )(w)


def _regroup_rows(w, groups, nj, tn):
    d = w.shape[1]
    assert w.shape[0] == groups * nj * tn
    return pl.pallas_call(
        _cast_kernel,
        grid=(nj, groups),
        in_specs=[pl.BlockSpec((tn, d), lambda j, s: (s * nj + j, 0))],
        out_specs=pl.BlockSpec((tn, d), lambda j, s: (j * groups + s, 0)),
        out_shape=jax.ShapeDtypeStruct(w.shape, BF16),
        compiler_params=_params("arbitrary", "arbitrary"),
        name="regroup_rows",
    )(w)


def _mixer_kernel(x_ref, w_ref, cw_ref, lng_ref, lnb_ref, sgw_ref, sgb_ref, y_ref,
                  xb_ref, pa_ref, pb_ref, gbuf_ref, carry_ref, *, tm, tn, nj, n_steps, tiles_per_seq):
    n = pl.program_id(0)

    @pl.when(n == 0)
    def _():
        pb_ref[...] = jnp.zeros(pb_ref.shape, F32)
        carry_ref[...] = jnp.zeros(carry_ref.shape, F32)

    @pl.when((n % nj == 0) & (n < n_steps))
    def _():
        xb_ref[...] = x_ref[...].astype(BF16)

    m = jnp.maximum(n - 1, 0)
    jp = m % nj
    first = ((m // nj) % tiles_per_seq) == 0

    def mix(proj_ref):
        b, c, h, u, v = (proj_ref[:, k * tn:(k + 1) * tn] for k in range(5))
        g = c * h
        gbuf_ref[0:SUBLANE, :] = jnp.where(first, 0.0, carry_ref[jp])
        gbuf_ref[SUBLANE:, :] = g
        carry_ref[jp] = g[tm - SUBLANE:, :]
        g1 = gbuf_ref[pl.ds(SUBLANE - 1, tm), :]
        g2 = gbuf_ref[pl.ds(SUBLANE - 2, tm), :]
        cw = cw_ref[...]
        conv = cw[0:1, :] * g2 + cw[1:2, :] * g1 + cw[2:3, :] * g
        y_ref[:, 0:tn] = (b * conv).astype(y_ref.dtype)

        pos_i = lax.broadcasted_iota(jnp.int32, (SG_BLOCK, SG_BLOCK), 0)
        pos_j = lax.broadcasted_iota(jnp.int32, (SG_BLOCK, SG_BLOCK), 1)
        mask = (pos_j // CHUNK) <= (pos_i // CHUNK)
        for hh in range(tn // LANE):
            sl = slice(hh * LANE, (hh + 1) * LANE)
            vn = _layer_norm(jax.nn.gelu(v[:, sl]), lng_ref[:, sl], lnb_ref[:, sl]).astype(BF16)
            gu = jax.nn.gelu(u[:, sl])
            ws = jnp.where(mask, sgw_ref[hh], 0.0).astype(BF16)
            bcol = sgb_ref[hh]
            for r in range(tm // SG_BLOCK):
                rows = slice(r * SG_BLOCK, (r + 1) * SG_BLOCK)
                sg = _dot(ws, vn[rows, :]) + bcol
                y_ref[rows, tn + hh * LANE:tn + (hh + 1) * LANE] = (gu[rows, :] * sg).astype(y_ref.dtype)

    def step(store_ref, load_ref):
        store_ref[...] = _dot(xb_ref[...], w_ref[...])
        mix(load_ref)

    pl.when(n % 2 == 0)(functools.partial(step, pa_ref, pb_ref))
    pl.when(n % 2 == 1)(functools.partial(step, pb_ref, pa_ref))


def _mixer(x2, w_in_p, conv_w, lng, lnb, sg_w, sg_bcol, seq):
    t, d = x2.shape
    cdim = conv_w.shape[1]
    tm, tn = MIX_TM, MIX_TN
    nj = cdim // tn
    assert seq % tm == 0 and t % tm == 0 and cdim % tn == 0 and w_in_p.shape[1] == 5 * cdim
    hp = tn // LANE
    n_steps = (t // tm) * nj
    cur = lambda n: jnp.minimum(n, n_steps - 1)
    prv = lambda n: jnp.maximum(n - 1, 0)
    return pl.pallas_call(
        functools.partial(_mixer_kernel, tm=tm, tn=tn, nj=nj, n_steps=n_steps, tiles_per_seq=seq // tm),
        grid=(n_steps + 1,),
        in_specs=[
            pl.BlockSpec((tm, d), lambda n: (cur(n) // nj, 0)),
            pl.BlockSpec((d, 5 * tn), lambda n: (0, cur(n) % nj)),
            pl.BlockSpec((conv_w.shape[0], tn), lambda n: (0, prv(n) % nj)),
            pl.BlockSpec((1, tn), lambda n: (0, prv(n) % nj)),
            pl.BlockSpec((1, tn), lambda n: (0, prv(n) % nj)),
            pl.BlockSpec((hp, SG_BLOCK, SG_BLOCK), lambda n: (prv(n) % nj, 0, 0)),
            pl.BlockSpec((hp, SG_BLOCK, 1), lambda n: (prv(n) % nj, 0, 0)),
        ],
        out_specs=pl.BlockSpec((tm, 2 * tn), lambda n: (prv(n) // nj, prv(n) % nj)),
        out_shape=jax.ShapeDtypeStruct((t, 2 * cdim), BF16),
        scratch_shapes=[pltpu.VMEM((tm, d), BF16),
                        pltpu.VMEM((tm, 5 * tn), F32),
                        pltpu.VMEM((tm, 5 * tn), F32),
                        pltpu.VMEM((tm + SUBLANE, tn), F32),
                        pltpu.VMEM((nj, SUBLANE, tn), F32)],
        compiler_params=_params("arbitrary"),
        name="mixer",
    )(x2, w_in_p, conv_w, lng, lnb, sg_w, sg_bcol)


def _outproj_kernel(y_ref, w_ref, x_ref, g_ref, b_ref, wr_ref, br_ref, h1_ref, lg_ref, z_ref, *, tn, nj, alpha):
    j = pl.program_id(1)
    z_ref[j] = alpha * x_ref[...] + _dot(y_ref[...], w_ref[...])

    @pl.when(j == nj - 1)
    def _():
        d = nj * tn
        mu = sum(jnp.sum(z_ref[jj], axis=-1, keepdims=True) for jj in range(nj)) / d
        var = sum(jnp.sum(jnp.square(z_ref[jj] - mu), axis=-1, keepdims=True) for jj in range(nj)) / d
        rstd = lax.rsqrt(var + LN_EPS)
        logits = br_ref[...]
        for jj in range(nj):
            cols = slice(jj * tn, (jj + 1) * tn)
            h = (z_ref[jj] - mu) * rstd * g_ref[:, cols] + b_ref[:, cols]
            h1_ref[:, cols] = h
            logits = logits + _dot(h.astype(BF16), wr_ref[cols, :])
        lg_ref[...] = logits


def _outproj(y, w_out_p, x2, g, b, wr, br, alpha):
    t, d = x2.shape
    kdim = y.shape[1]
    tm, tn = MIX_TM, OUT_TN
    nj = d // tn
    assert t % tm == 0 and d % tn == 0 and w_out_p.shape[0] == kdim
    return pl.pallas_call(
        functools.partial(_outproj_kernel, tn=tn, nj=nj, alpha=alpha),
        grid=(t // tm, nj),
        in_specs=[
            pl.BlockSpec((tm, kdim), lambda i, j: (i, 0)),
            pl.BlockSpec((kdim, tn), lambda i, j: (0, j)),
            pl.BlockSpec((tm, tn), lambda i, j: (i, j)),
            pl.BlockSpec((1, d), lambda i, j: (0, 0)),
            pl.BlockSpec((1, d), lambda i, j: (0, 0)),
            pl.BlockSpec((d, LANE), lambda i, j: (0, 0)),
            pl.BlockSpec((1, LANE), lambda i, j: (0, 0)),
        ],
        out_specs=[pl.BlockSpec((tm, d), lambda i, j: (i, 0)),
                   pl.BlockSpec((tm, LANE), lambda i, j: (i, 0))],
        out_shape=[jax.ShapeDtypeStruct((t, d), F32), jax.ShapeDtypeStruct((t, LANE), F32)],
        scratch_shapes=[pltpu.VMEM((nj, tm, tn), F32)],
        compiler_params=_params("arbitrary", "arbitrary"),
        name="outproj",
    )(y, w_out_p, x2, g, b, wr, br)


def _route_kernel(lg_ref, pk_ref, cnt_ref, run_ref, *, n_groups, per_group):
    i = pl.program_id(0)

    @pl.when(i == 0)
    def _():
        run_ref[...] = jnp.zeros_like(run_ref)

    l = lg_ref[...]
    tm = l.shape[0]
    lane = lax.broadcasted_iota(jnp.int32, l.shape, 1)
    neg = jnp.float32(-jnp.inf)
    gmask = lane < n_groups
    gl = jnp.where(gmask, l, neg)
    gmax = jnp.max(gl, axis=-1, keepdims=True)
    gi = jnp.min(jnp.where(gl == gmax, lane, LANE), axis=-1, keepdims=True)
    gp = 1.0 / jnp.sum(jnp.where(gmask, jnp.exp(l - gmax), 0.0), axis=-1, keepdims=True)
    elane = lane - n_groups
    emask = (elane >= 0) & (elane // per_group == gi)
    el = jnp.where(emask, l, neg)
    m1 = jnp.max(el, axis=-1, keepdims=True)
    i1 = jnp.min(jnp.where(el == m1, lane, LANE), axis=-1, keepdims=True)
    el2 = jnp.where(lane == i1, neg, el)
    m2 = jnp.max(el2, axis=-1, keepdims=True)
    i2 = jnp.min(jnp.where(el2 == m2, lane, LANE), axis=-1, keepdims=True)
    t2 = jnp.exp(m2 - m1)
    w0 = gp / (1.0 + t2)
    w1 = gp * t2 / (1.0 + t2)
    e0 = i1 - n_groups
    e1 = i2 - n_groups
    oh0 = lane == e0
    oh1 = lane == e1
    oh = (oh0 | oh1).astype(BF16)
    row = lax.broadcasted_iota(jnp.int32, (tm, tm), 0)
    col = lax.broadcasted_iota(jnp.int32, (tm, tm), 1)
    tri = (row > col).astype(BF16)
    before = run_ref[...] + _dot(tri, oh)
    r0 = jnp.sum(jnp.where(oh0, before, 0.0), axis=-1, keepdims=True)
    r1 = jnp.sum(jnp.where(oh1, before, 0.0), axis=-1, keepdims=True)
    total = run_ref[...] + jnp.sum(oh.astype(F32), axis=0, keepdims=True)
    run_ref[...] = total
    cnt_ref[...] = total
    vals = (e0.astype(F32), e1.astype(F32), w0, w1, r0, r1)
    pk = jnp.zeros(l.shape, F32)
    for k, val in enumerate(vals):
        pk = jnp.where(lane == k, val, pk)
    pk_ref[...] = pk


def _route(logits, n_groups, per_group):
    t = logits.shape[0]
    tm = ROUTE_TM
    assert t % tm == 0 and n_groups * (1 + per_group) <= LANE
    return pl.pallas_call(
        functools.partial(_route_kernel, n_groups=n_groups, per_group=per_group),
        grid=(t // tm,),
        in_specs=[pl.BlockSpec((tm, LANE), lambda i: (i, 0))],
        out_specs=[pl.BlockSpec((tm, LANE), lambda i: (i, 0)),
                   pl.BlockSpec((1, LANE), lambda i: (0, 0))],
        out_shape=[jax.ShapeDtypeStruct((t, LANE), F32), jax.ShapeDtypeStruct((1, LANE), F32)],
        scratch_shapes=[pltpu.VMEM((1, LANE), F32)],
        compiler_params=_params("arbitrary"),
        name="route",
    )(logits)


def _rowmap_kernel(eid_ref, rank_ref, start_ref, asg_ref, *, n_rows, n_asg):
    def fill(r, c):
        asg_ref[r] = 0
        return c

    lax.fori_loop(0, n_rows, fill, 0, unroll=ISSUE_UNROLL)

    def place(a, c):
        asg_ref[start_ref[eid_ref[a]] + rank_ref[a]] = a
        return c

    lax.fori_loop(0, n_asg, place, 0, unroll=ISSUE_UNROLL)


def _rowmap(eid, rank, start, n_rows):
    n_asg = eid.shape[0]
    smem = pl.BlockSpec(memory_space=pltpu.SMEM)
    return pl.pallas_call(
        functools.partial(_rowmap_kernel, n_rows=n_rows, n_asg=n_asg),
        in_specs=[smem, smem, smem],
        out_specs=smem,
        out_shape=jax.ShapeDtypeStruct((n_rows,), jnp.int32),
        name="rowmap",
    )(eid, rank, start)


def _moe_kernel(be_ref, row0_ref, nsub_ref, nreal_ref, asg_ref, h1_hbm, wg_ref, wu_ref, wd_ref, y_hbm,
                xg_ref, xb_ref, yacc_ref, gsem, ssem, *, nb, nf, sub, tok_mask):
    b = pl.program_id(0)
    f = pl.program_id(1)
    nsub = nsub_ref[b]
    max_tiles = xb_ref.shape[0] // sub

    def gather_row(base, r):
        tok = asg_ref[base + r] & tok_mask
        pltpu.make_async_copy(h1_hbm.at[pl.ds(tok, 1)], xg_ref.at[pl.ds(r, 1)], gsem).start()

    def scatter_row(base, r):
        dst = asg_ref[base + r]
        pltpu.make_async_copy(yacc_ref.at[pl.ds(r, 1)], y_hbm.at[pl.ds(dst, 1)], ssem).start()

    def issue_tiles(row_fn, base, n_tiles):
        for s in range(max_tiles):
            @pl.when(s < n_tiles)
            def _(s=s):
                for r in range(s * sub, (s + 1) * sub):
                    row_fn(base, r)

    def issue_range(row_fn, base, lo, hi):
        groups = (hi - lo) // ISSUE_UNROLL

        def body(q, c):
            for k in range(ISSUE_UNROLL):
                row_fn(base, lo + q * ISSUE_UNROLL + k)
            return c

        lax.fori_loop(0, groups, body, 0)

        def tail(r, c):
            row_fn(base, r)
            return c

        lax.fori_loop(lo + groups * ISSUE_UNROLL, hi, tail, 0)

    def wait_rows(n, tile_copy, row_copy):
        def tile(s, c):
            tile_copy.wait()
            return c

        lax.fori_loop(0, n // sub, tile, 0)

        def row(r, c):
            row_copy.wait()
            return c

        lax.fori_loop((n // sub) * sub, n, row, 0)

    def gather_wait(n):
        wait_rows(n, pltpu.make_async_copy(h1_hbm.at[pl.ds(0, sub)], xg_ref.at[pl.ds(0, sub)], gsem),
                  pltpu.make_async_copy(h1_hbm.at[pl.ds(0, 1)], xg_ref.at[pl.ds(0, 1)], gsem))

    def scatter_wait(blk):
        wait_rows(nreal_ref[blk], pltpu.make_async_copy(yacc_ref.at[pl.ds(0, sub)], y_hbm.at[pl.ds(0, sub)], ssem),
                  pltpu.make_async_copy(yacc_ref.at[pl.ds(0, 1)], y_hbm.at[pl.ds(0, 1)], ssem))

    @pl.when((f == 0) & (b == 0))
    def _():
        yacc_ref[...] = jnp.zeros(yacc_ref.shape, F32)

    @pl.when((f == 0) & (b > 0) & (nsub == 0))
    def _():
        scatter_wait(jnp.maximum(b - 1, 0))

    @pl.when(nsub > 0)
    def _():
        @pl.when(f == 0)
        def _():
            @pl.when(b == 0)
            def _():
                issue_range(gather_row, row0_ref[0], 0, nsub * sub)

            gather_wait(nsub * sub)

            def cast(s, c):
                rows = pl.ds(pl.multiple_of(s * sub, sub), sub)
                xb_ref[rows, :] = xg_ref[rows, :].astype(BF16)
                return c

            lax.fori_loop(0, nsub, cast, 0)

            @pl.when(b + 1 < nb)
            def _():
                nxt = jnp.minimum(b + 1, nb - 1)
                issue_tiles(gather_row, row0_ref[nxt], nsub_ref[nxt])

        def compute(m):
            rows = slice(0, m * sub)
            xs = xb_ref[rows, :]
            hb = (jax.nn.silu(_dot(xs, wg_ref[...])) * _dot(xs, wu_ref[...])).astype(BF16)

            @pl.when((f == 0) & (b > 0))
            def _():
                scatter_wait(jnp.maximum(b - 1, 0))

            yp = _dot(hb, wd_ref[...])
            yacc_ref[rows, :] = jnp.where(f == 0, yp, yacc_ref[rows, :] + yp)

        for m in range(1, xb_ref.shape[0] // sub + 1):
            pl.when(nsub == m)(functools.partial(compute, m))

        @pl.when(f == nf - 1)
        def _():
            full = nreal_ref[b] // sub
            issue_tiles(scatter_row, row0_ref[b], full)
            issue_range(scatter_row, row0_ref[b], full * sub, nreal_ref[b])

            @pl.when(b == nb - 1)
            def _():
                scatter_wait(b)


def _moe(be, row0, nsub, nreal, asg, h1, w_gate, w_up, w_down, n_out_rows):
    t, d = h1.shape
    ne, _, de = w_gate.shape
    nb = be.shape[0]
    nf = de // MOE_TF
    assert de % MOE_TF == 0 and t & (t - 1) == 0
    last = nf - 1
    fsel = lambda f, n: jnp.where(n > 0, f, last)
    grid_spec = pltpu.PrefetchScalarGridSpec(
        num_scalar_prefetch=5,
        grid=(nb, nf),
        in_specs=[
            pl.BlockSpec(memory_space=pl.ANY),
            pl.BlockSpec((None, d, MOE_TF), lambda b, f, be, r0, ns, nr, asg: (be[b], 0, fsel(f, ns[b]))),
            pl.BlockSpec((None, d, MOE_TF), lambda b, f, be, r0, ns, nr, asg: (be[b], 0, fsel(f, ns[b]))),
            pl.BlockSpec((None, MOE_TF, d), lambda b, f, be, r0, ns, nr, asg: (be[b], fsel(f, ns[b]), 0)),
        ],
        out_specs=pl.BlockSpec(memory_space=pl.ANY),
        scratch_shapes=[pltpu.VMEM((MOE_ROWS, d), F32),
                        pltpu.VMEM((MOE_ROWS, d), BF16),
                        pltpu.VMEM((MOE_ROWS, d), F32),
                        pltpu.SemaphoreType.DMA,
                        pltpu.SemaphoreType.DMA],
    )
    return pl.pallas_call(
        functools.partial(_moe_kernel, nb=nb, nf=nf, sub=MOE_SUB, tok_mask=t - 1),
        grid_spec=grid_spec,
        out_shape=jax.ShapeDtypeStruct((n_out_rows, d), F32),
        compiler_params=_params("arbitrary", "arbitrary", vmem=MOE_VMEM_LIMIT),
        name="moe",
    )(be, row0, nsub, nreal, asg, h1, w_gate, w_up, w_down)


def _combine_gate_kernel(h1_ref, y0_ref, y1_ref, pk_ref, g_ref, b_ref, wpg_ref, bpg_ref, p_ref, wpe_ref,
                         o_ref, z_ref, hb_ref, *, nj, tn, alpha):
    i = pl.program_id(0)
    j = pl.program_id(1)
    slot = i % 2
    prev = 1 - slot

    @pl.when((i == 0) & (j == 0))
    def _():
        z_ref[1] = jnp.zeros(z_ref.shape[1:], F32)
        hb_ref[1] = jnp.zeros(hb_ref.shape[1:], BF16)

    pk = pk_ref[...]
    ffn = pk[:, 2:3] * y0_ref[...] + pk[:, 3:4] * y1_ref[...]
    z_ref[slot, j] = alpha * h1_ref[...] + ffn

    gate = jax.nn.sigmoid(_dot(hb_ref[prev], wpg_ref[...]) + bpg_ref[...])
    ple = _dot(p_ref[...].astype(BF16), wpe_ref[...])
    o_ref[...] = z_ref[prev, j] + gate * ple

    @pl.when(j == nj - 1)
    def _():
        d = nj * tn
        mu = sum(jnp.sum(z_ref[slot, jj], axis=-1, keepdims=True) for jj in range(nj)) / d
        var = sum(jnp.sum(jnp.square(z_ref[slot, jj] - mu), axis=-1, keepdims=True) for jj in range(nj)) / d
        rstd = lax.rsqrt(var + LN_EPS)
        for jj in range(nj):
            cols = slice(jj * tn, (jj + 1) * tn)
            h = (z_ref[slot, jj] - mu) * rstd * g_ref[:, cols] + b_ref[:, cols]
            z_ref[slot, jj] = h
            hb_ref[slot, :, cols] = h.astype(BF16)


def _combine_gate(h1, y2, pk, g, b, w_pg_b, b_pg, p2, w_pe_b, alpha):
    t, d = h1.shape
    pd = p2.shape[1]
    tm, tn = MIX_TM, OUT_TN
    assert t % tm == 0 and d % tn == 0
    ni, nj = t // tm, d // tn
    row_in = lambda i: jnp.minimum(i, ni - 1)
    col_in = lambda i, j: jnp.where(i < ni, j, nj - 1)
    row_out = lambda i: jnp.maximum(i - 1, 0)
    return pl.pallas_call(
        functools.partial(_combine_gate_kernel, nj=nj, tn=tn, alpha=alpha),
        grid=(ni + 1, nj),
        in_specs=[
            pl.BlockSpec((tm, tn), lambda i, j: (row_in(i), col_in(i, j))),
            pl.BlockSpec((tm, tn), lambda i, j: (row_in(i), col_in(i, j))),
            pl.BlockSpec((tm, tn), lambda i, j: (row_in(i) + ni, col_in(i, j))),
            pl.BlockSpec((tm, LANE), lambda i, j: (row_in(i), 0)),
            pl.BlockSpec((1, d), lambda i, j: (0, 0)),
            pl.BlockSpec((1, d), lambda i, j: (0, 0)),
            pl.BlockSpec((d, tn), lambda i, j: (0, j)),
            pl.BlockSpec((1, tn), lambda i, j: (0, j)),
            pl.BlockSpec((tm, pd), lambda i, j: (row_out(i), 0)),
            pl.BlockSpec((pd, tn), lambda i, j: (0, j)),
        ],
        out_specs=pl.BlockSpec((tm, tn), lambda i, j: (row_out(i), jnp.where(i == 0, 0, j))),
        out_shape=jax.ShapeDtypeStruct((t, d), F32),
        scratch_shapes=[pltpu.VMEM((2, nj, tm, tn), F32), pltpu.VMEM((2, tm, d), BF16)],
        compiler_params=_params("arbitrary", "arbitrary"),
        name="combine_gate",
    )(h1, y2, y2, pk, g, b, w_pg_b, b_pg, p2, w_pe_b)


def _block_table(counts, n_asg):
    ne = counts.shape[0]
    padded = ((counts + MOE_SUB - 1) // MOE_SUB) * MOE_SUB
    pad_end = jnp.cumsum(padded)
    pad_start = pad_end - padded
    nblk = (padded + MOE_ROWS - 1) // MOE_ROWS
    blk_end = jnp.cumsum(nblk)
    blk_start = blk_end - nblk
    nb = ne + -(-n_asg // MOE_ROWS)
    bidx = jnp.arange(nb, dtype=jnp.int32)
    used = bidx < blk_end[-1]
    be = jnp.minimum(jnp.searchsorted(blk_end, jnp.minimum(bidx, blk_end[-1] - 1), side="right"), ne - 1)
    be = be.astype(jnp.int32)
    within = bidx - blk_start[be]
    row0 = pad_start[be] + within * MOE_ROWS
    nsub = jnp.clip((padded[be] - within * MOE_ROWS) // MOE_SUB, 0, MOE_ROWS // MOE_SUB)
    nsub = jnp.where(used, nsub, 0)
    row0 = jnp.where(used, row0, 0)
    nreal = jnp.where(used, jnp.clip(counts[be] - within * MOE_ROWS, 0, MOE_ROWS), 0)
    i32 = lambda a: a.astype(jnp.int32)
    return i32(pad_start), be, i32(row0), i32(nsub), i32(nreal)


def kernel(x, p, w_in, conv_w, sg_ln_g, sg_ln_b, sg_w, sg_b, w_out, ln1_g, ln1_b, w_rg, b_rg, w_re, b_re,
           w_gate, w_up, w_down, ln2_g, ln2_b, w_pg, b_pg, w_pe):
    depth = w_in.shape[0]
    bsz, seq, d = x.shape
    t = bsz * seq
    alpha = (2 * depth) ** 0.25
    n_groups = w_rg.shape[-1]
    ne = w_re.shape[-1]
    per_group = ne // n_groups
    n_asg = t * TOP_K
    n_rows = ((n_asg + ne * (MOE_SUB - 1) + MOE_SUB - 1) // MOE_SUB) * MOE_SUB

    h = x.reshape(t, d)
    for i in range(depth):
        cdim = conv_w.shape[-1]
        nj = cdim // MIX_TN
        w_in_p = _regroup_cols(w_in[i], 5, nj, MIX_TN)
        w_out_p = _regroup_rows(w_out[i], 2, nj, MIX_TN)
        y = _mixer(h, w_in_p, conv_w[i], sg_ln_g[i].reshape(1, -1), sg_ln_b[i].reshape(1, -1),
                   sg_w[i], sg_b[i][:, :, None], seq)
        wr = jnp.concatenate([w_rg[i], w_re[i], jnp.zeros((d, LANE - n_groups - ne), F32)], axis=1).astype(BF16)
        br = jnp.concatenate([b_rg[i], b_re[i], jnp.zeros((LANE - n_groups - ne,), F32)]).reshape(1, LANE)
        h1, logits = _outproj(y, w_out_p, h, ln1_g[i].reshape(1, d), ln1_b[i].reshape(1, d), wr, br, alpha)
        pk, cnt = _route(logits, n_groups, per_group)
        pad_start, be, row0, nsub, nreal = _block_table(cnt[0, :ne].astype(jnp.int32), n_asg)
        eid = jnp.concatenate([pk[:, 0], pk[:, 1]]).astype(jnp.int32)
        rank = jnp.concatenate([pk[:, 4], pk[:, 5]]).astype(jnp.int32)
        asg = _rowmap(eid, rank, pad_start, n_rows)
        y2 = _moe(be, row0, nsub, nreal, asg, h1, w_gate[i], w_up[i], w_down[i], n_asg)
        h = _combine_gate(h1, y2, pk, ln2_g[i].reshape(1, d), ln2_b[i].reshape(1, d), w_pg[i].astype(BF16),
                          b_pg[i].reshape(1, d), p[i].reshape(t, -1), w_pe[i].astype(BF16), alpha)
    return h.reshape(bsz, seq, d)
```

```python
import functools

import jax
import jax.numpy as jnp
from jax import lax
from jax.experimental import pallas as pl
from jax.experimental.pallas import tpu as pltpu

F32 = jnp.float32
BF16 = jnp.bfloat16

LANE = 128
SUBLANE = 8
MXU_COL = 256
VMEM_LIMIT = 56 * 1024 * 1024
MOE_VMEM_LIMIT = 60 * 1024 * 1024

CHUNK = 64
SG_BLOCK = 128
LN_EPS = 1e-5
TOP_K = 2

MIX_TM = 512
MIX_TN = MXU_COL
OUT_TN = 512
ROUTE_TM = 512
MOE_SUB = 128
MOE_ROWS = 768
MOE_TF = MXU_COL
ISSUE_UNROLL = 8
CAST_TK = 4096

_dot = functools.partial(jnp.dot, preferred_element_type=F32)


def _params(*sem, vmem=VMEM_LIMIT):
    return pltpu.CompilerParams(dimension_semantics=sem, vmem_limit_bytes=vmem)


def _layer_norm(z, g, b):
    mu = jnp.mean(z, axis=-1, keepdims=True)
    zc = z - mu
    var = jnp.mean(zc * zc, axis=-1, keepdims=True)
    return zc * lax.rsqrt(var + LN_EPS) * g + b


def _cast_kernel(w_ref, o_ref):
    o_ref[...] = w_ref[...].astype(o_ref.dtype)


def _regroup_cols(w, groups, nj, tn):
    d = w.shape[0]
    tk = min(CAST_TK, d)
    assert d % tk == 0 and w.shape[1] == groups * nj * tn
    return pl.pallas_call(
        _cast_kernel,
        grid=(nj, groups, d // tk),
        in_specs=[pl.BlockSpec((tk, tn), lambda j, s, k: (k, s * nj + j))],
        out_specs=pl.BlockSpec((tk, tn), lambda j, s, k: (k, j * groups + s)),
        out_shape=jax.ShapeDtypeStruct(w.shape, BF16),
        compiler_params=_params("arbitrary", "arbitrary", "arbitrary"),
        name="regroup_cols",
    )(w)


def _regroup_rows(w, groups, nj, tn):
    d = w.shape[1]
    assert w.shape[0] == groups * nj * tn
    return pl.pallas_call(
        _cast_kernel,
        grid=(nj, groups),
        in_specs=[pl.BlockSpec((tn, d), lambda j, s: (s * nj + j, 0))],
        out_specs=pl.BlockSpec((tn, d), lambda j, s: (j * groups + s, 0)),
        out_shape=jax.ShapeDtypeStruct(w.shape, BF16),
        compiler_params=_params("arbitrary", "arbitrary"),
        name="regroup_rows",
    )(w)


def _mixer_kernel(x_ref, w_ref, cw_ref, lng_ref, lnb_ref, sgw_ref, sgb_ref, y_ref,
                  xb_ref, pa_ref, pb_ref, gbuf_ref, carry_ref, *, tm, tn, nj, n_steps, tiles_per_seq):
    n = pl.program_id(0)

    @pl.when(n == 0)
    def _():
        pb_ref[...] = jnp.zeros(pb_ref.shape, F32)
        carry_ref[...] = jnp.zeros(carry_ref.shape, F32)

    @pl.when((n % nj == 0) & (n < n_steps))
    def _():
        xb_ref[...] = x_ref[...].astype(BF16)

    m = jnp.maximum(n - 1, 0)
    jp = m % nj
    first = ((m // nj) % tiles_per_seq) == 0

    def mix(proj_ref):
        b, c, h, u, v = (proj_ref[:, k * tn:(k + 1) * tn] for k in range(5))
        g = c * h
        gbuf_ref[0:SUBLANE, :] = jnp.where(first, 0.0, carry_ref[jp])
        gbuf_ref[SUBLANE:, :] = g
        carry_ref[jp] = g[tm - SUBLANE:, :]
        g1 = gbuf_ref[pl.ds(SUBLANE - 1, tm), :]
        g2 = gbuf_ref[pl.ds(SUBLANE - 2, tm), :]
        cw = cw_ref[...]
        conv = cw[0:1, :] * g2 + cw[1:2, :] * g1 + cw[2:3, :] * g
        y_ref[:, 0:tn] = (b * conv).astype(y_ref.dtype)

        pos_i = lax.broadcasted_iota(jnp.int32, (SG_BLOCK, SG_BLOCK), 0)
        pos_j = lax.broadcasted_iota(jnp.int32, (SG_BLOCK, SG_BLOCK), 1)
        mask = (pos_j // CHUNK) <= (pos_i // CHUNK)
        for hh in range(tn // LANE):
            sl = slice(hh * LANE, (hh + 1) * LANE)
            vn = _layer_norm(jax.nn.gelu(v[:, sl]), lng_ref[:, sl], lnb_ref[:, sl]).astype(BF16)
            gu = jax.nn.gelu(u[:, sl])
            ws = jnp.where(mask, sgw_ref[hh], 0.0).astype(BF16)
            bcol = sgb_ref[hh]
            nblk = tm // SG_BLOCK
            vcat = jnp.concatenate([vn[r * SG_BLOCK:(r + 1) * SG_BLOCK, :] for r in range(nblk)], axis=1)
            sg = _dot(ws, vcat) + bcol
            for r in range(nblk):
                rows = slice(r * SG_BLOCK, (r + 1) * SG_BLOCK)
                y_ref[rows, tn + hh * LANE:tn + (hh + 1) * LANE] = (
                    gu[rows, :] * sg[:, r * LANE:(r + 1) * LANE]).astype(y_ref.dtype)

    def step(store_ref, load_ref):
        store_ref[...] = _dot(xb_ref[...], w_ref[...])
        mix(load_ref)

    pl.when(n % 2 == 0)(functools.partial(step, pa_ref, pb_ref))
    pl.when(n % 2 == 1)(functools.partial(step, pb_ref, pa_ref))


def _mixer(x2, w_in_p, conv_w, lng, lnb, sg_w, sg_bcol, seq):
    t, d = x2.shape
    cdim = conv_w.shape[1]
    tm, tn = MIX_TM, MIX_TN
    nj = cdim // tn
    assert seq % tm == 0 and t % tm == 0 and cdim % tn == 0 and w_in_p.shape[1] == 5 * cdim
    hp = tn // LANE
    n_steps = (t // tm) * nj
    cur = lambda n: jnp.minimum(n, n_steps - 1)
    prv = lambda n: jnp.maximum(n - 1, 0)
    return pl.pallas_call(
        functools.partial(_mixer_kernel, tm=tm, tn=tn, nj=nj, n_steps=n_steps, tiles_per_seq=seq // tm),
        grid=(n_steps + 1,),
        in_specs=[
            pl.BlockSpec((tm, d), lambda n: (cur(n) // nj, 0)),
            pl.BlockSpec((d, 5 * tn), lambda n: (0, cur(n) % nj)),
            pl.BlockSpec((conv_w.shape[0], tn), lambda n: (0, prv(n) % nj)),
            pl.BlockSpec((1, tn), lambda n: (0, prv(n) % nj)),
            pl.BlockSpec((1, tn), lambda n: (0, prv(n) % nj)),
            pl.BlockSpec((hp, SG_BLOCK, SG_BLOCK), lambda n: (prv(n) % nj, 0, 0)),
            pl.BlockSpec((hp, SG_BLOCK, 1), lambda n: (prv(n) % nj, 0, 0)),
        ],
        out_specs=pl.BlockSpec((tm, 2 * tn), lambda n: (prv(n) // nj, prv(n) % nj)),
        out_shape=jax.ShapeDtypeStruct((t, 2 * cdim), BF16),
        scratch_shapes=[pltpu.VMEM((tm, d), BF16),
                        pltpu.VMEM((tm, 5 * tn), F32),
                        pltpu.VMEM((tm, 5 * tn), F32),
                        pltpu.VMEM((tm + SUBLANE, tn), F32),
                        pltpu.VMEM((nj, SUBLANE, tn), F32)],
        compiler_params=_params("arbitrary"),
        name="mixer",
    )(x2, w_in_p, conv_w, lng, lnb, sg_w, sg_bcol)


def _outproj_kernel(y_ref, w_ref, x_ref, g_ref, b_ref, wr_ref, br_ref, h1_ref, lg_ref, z_ref, *, tn, nj, alpha):
    j = pl.program_id(1)
    z_ref[j] = alpha * x_ref[...] + _dot(y_ref[...], w_ref[...])

    @pl.when(j == nj - 1)
    def _():
        d = nj * tn
        mu = sum(jnp.sum(z_ref[jj], axis=-1, keepdims=True) for jj in range(nj)) / d
        var = sum(jnp.sum(jnp.square(z_ref[jj] - mu), axis=-1, keepdims=True) for jj in range(nj)) / d
        rstd = lax.rsqrt(var + LN_EPS)
        logits = br_ref[...]
        for jj in range(nj):
            cols = slice(jj * tn, (jj + 1) * tn)
            h = (z_ref[jj] - mu) * rstd * g_ref[:, cols] + b_ref[:, cols]
            h1_ref[:, cols] = h
            logits = logits + _dot(h.astype(BF16), wr_ref[cols, :])
        lg_ref[...] = logits


def _outproj(y, w_out_p, x2, g, b, wr, br, alpha):
    t, d = x2.shape
    kdim = y.shape[1]
    tm, tn = MIX_TM, OUT_TN
    nj = d // tn
    assert t % tm == 0 and d % tn == 0 and w_out_p.shape[0] == kdim
    return pl.pallas_call(
        functools.partial(_outproj_kernel, tn=tn, nj=nj, alpha=alpha),
        grid=(t // tm, nj),
        in_specs=[
            pl.BlockSpec((tm, kdim), lambda i, j: (i, 0)),
            pl.BlockSpec((kdim, tn), lambda i, j: (0, j)),
            pl.BlockSpec((tm, tn), lambda i, j: (i, j)),
            pl.BlockSpec((1, d), lambda i, j: (0, 0)),
            pl.BlockSpec((1, d), lambda i, j: (0, 0)),
            pl.BlockSpec((d, LANE), lambda i, j: (0, 0)),
            pl.BlockSpec((1, LANE), lambda i, j: (0, 0)),
        ],
        out_specs=[pl.BlockSpec((tm, d), lambda i, j: (i, 0)),
                   pl.BlockSpec((tm, LANE), lambda i, j: (i, 0))],
        out_shape=[jax.ShapeDtypeStruct((t, d), F32), jax.ShapeDtypeStruct((t, LANE), F32)],
        scratch_shapes=[pltpu.VMEM((nj, tm, tn), F32)],
        compiler_params=_params("arbitrary", "arbitrary"),
        name="outproj",
    )(y, w_out_p, x2, g, b, wr, br)


def _route_kernel(lg_ref, pk_ref, cnt_ref, run_ref, *, n_groups, per_group):
    i = pl.program_id(0)

    @pl.when(i == 0)
    def _():
        run_ref[...] = jnp.zeros_like(run_ref)

    l = lg_ref[...]
    tm = l.shape[0]
    lane = lax.broadcasted_iota(jnp.int32, l.shape, 1)
    neg = jnp.float32(-jnp.inf)
    gmask = lane < n_groups
    gl = jnp.where(gmask, l, neg)
    gmax = jnp.max(gl, axis=-1, keepdims=True)
    gi = jnp.min(jnp.where(gl == gmax, lane, LANE), axis=-1, keepdims=True)
    gp = 1.0 / jnp.sum(jnp.where(gmask, jnp.exp(l - gmax), 0.0), axis=-1, keepdims=True)
    elane = lane - n_groups
    emask = (elane >= 0) & (elane // per_group == gi)
    el = jnp.where(emask, l, neg)
    m1 = jnp.max(el, axis=-1, keepdims=True)
    i1 = jnp.min(jnp.where(el == m1, lane, LANE), axis=-1, keepdims=True)
    el2 = jnp.where(lane == i1, neg, el)
    m2 = jnp.max(el2, axis=-1, keepdims=True)
    i2 = jnp.min(jnp.where(el2 == m2, lane, LANE), axis=-1, keepdims=True)
    t2 = jnp.exp(m2 - m1)
    w0 = gp / (1.0 + t2)
    w1 = gp * t2 / (1.0 + t2)
    e0 = i1 - n_groups
    e1 = i2 - n_groups
    oh0 = lane == e0
    oh1 = lane == e1
    oh = (oh0 | oh1).astype(BF16)
    row = lax.broadcasted_iota(jnp.int32, (tm, tm), 0)
    col = lax.broadcasted_iota(jnp.int32, (tm, tm), 1)
    tri = (row > col).astype(BF16)
    before = run_ref[...] + _dot(tri, oh)
    r0 = jnp.sum(jnp.where(oh0, before, 0.0), axis=-1, keepdims=True)
    r1 = jnp.sum(jnp.where(oh1, before, 0.0), axis=-1, keepdims=True)
    total = run_ref[...] + jnp.sum(oh.astype(F32), axis=0, keepdims=True)
    run_ref[...] = total
    cnt_ref[...] = total
    vals = (e0.astype(F32), e1.astype(F32), w0, w1, r0, r1)
    pk = jnp.zeros(l.shape, F32)
    for k, val in enumerate(vals):
        pk = jnp.where(lane == k, val, pk)
    pk_ref[...] = pk


def _route(logits, n_groups, per_group):
    t = logits.shape[0]
    tm = ROUTE_TM
    assert t % tm == 0 and n_groups * (1 + per_group) <= LANE
    return pl.pallas_call(
        functools.partial(_route_kernel, n_groups=n_groups, per_group=per_group),
        grid=(t // tm,),
        in_specs=[pl.BlockSpec((tm, LANE), lambda i: (i, 0))],
        out_specs=[pl.BlockSpec((tm, LANE), lambda i: (i, 0)),
                   pl.BlockSpec((1, LANE), lambda i: (0, 0))],
        out_shape=[jax.ShapeDtypeStruct((t, LANE), F32), jax.ShapeDtypeStruct((1, LANE), F32)],
        scratch_shapes=[pltpu.VMEM((1, LANE), F32)],
        compiler_params=_params("arbitrary"),
        name="route",
    )(logits)


def _rowmap_kernel(eid_ref, rank_ref, start_ref, asg_ref, *, n_rows, n_asg):
    def fill(r, c):
        asg_ref[r] = 0
        return c

    lax.fori_loop(0, n_rows, fill, 0, unroll=ISSUE_UNROLL)

    def place(a, c):
        asg_ref[start_ref[eid_ref[a]] + rank_ref[a]] = a
        return c

    lax.fori_loop(0, n_asg, place, 0, unroll=ISSUE_UNROLL)


def _rowmap(eid, rank, start, n_rows):
    n_asg = eid.shape[0]
    smem = pl.BlockSpec(memory_space=pltpu.SMEM)
    return pl.pallas_call(
        functools.partial(_rowmap_kernel, n_rows=n_rows, n_asg=n_asg),
        in_specs=[smem, smem, smem],
        out_specs=smem,
        out_shape=jax.ShapeDtypeStruct((n_rows,), jnp.int32),
        name="rowmap",
    )(eid, rank, start)


def _moe_kernel(be_ref, row0_ref, nsub_ref, nreal_ref, asg_ref, h1_hbm, wg_ref, wu_ref, wd_ref, y_hbm,
                xg_ref, xb_ref, yacc_ref, gsem, ssem, *, nb, nf, sub, tok_mask):
    b = pl.program_id(0)
    f = pl.program_id(1)
    nsub = nsub_ref[b]
    max_tiles = xb_ref.shape[0] // sub

    def gather_row(base, r):
        tok = asg_ref[base + r] & tok_mask
        pltpu.make_async_copy(h1_hbm.at[pl.ds(tok, 1)], xg_ref.at[pl.ds(r, 1)], gsem).start()

    def scatter_row(base, r):
        dst = asg_ref[base + r]
        pltpu.make_async_copy(yacc_ref.at[pl.ds(r, 1)], y_hbm.at[pl.ds(dst, 1)], ssem).start()

    def issue_tiles(row_fn, base, n_tiles):
        for s in range(max_tiles):
            @pl.when(s < n_tiles)
            def _(s=s):
                for r in range(s * sub, (s + 1) * sub):
                    row_fn(base, r)

    def issue_range(row_fn, base, lo, hi):
        groups = (hi - lo) // ISSUE_UNROLL

        def body(q, c):
            for k in range(ISSUE_UNROLL):
                row_fn(base, lo + q * ISSUE_UNROLL + k)
            return c

        lax.fori_loop(0, groups, body, 0)

        def tail(r, c):
            row_fn(base, r)
            return c

        lax.fori_loop(lo + groups * ISSUE_UNROLL, hi, tail, 0)

    def wait_rows(n, tile_copy, row_copy):
        def tile(s, c):
            tile_copy.wait()
            return c

        lax.fori_loop(0, n // sub, tile, 0)

        def row(r, c):
            row_copy.wait()
            return c

        lax.fori_loop((n // sub) * sub, n, row, 0)

    def gather_wait(n):
        wait_rows(n, pltpu.make_async_copy(h1_hbm.at[pl.ds(0, sub)], xg_ref.at[pl.ds(0, sub)], gsem),
                  pltpu.make_async_copy(h1_hbm.at[pl.ds(0, 1)], xg_ref.at[pl.ds(0, 1)], gsem))

    def scatter_wait(blk):
        wait_rows(nreal_ref[blk], pltpu.make_async_copy(yacc_ref.at[pl.ds(0, sub)], y_hbm.at[pl.ds(0, sub)], ssem),
                  pltpu.make_async_copy(yacc_ref.at[pl.ds(0, 1)], y_hbm.at[pl.ds(0, 1)], ssem))

    @pl.when((f == 0) & (b == 0))
    def _():
        yacc_ref[...] = jnp.zeros(yacc_ref.shape, F32)

    @pl.when((f == 0) & (b > 0) & (nsub == 0))
    def _():
        scatter_wait(jnp.maximum(b - 1, 0))

    @pl.when(nsub > 0)
    def _():
        @pl.when(f == 0)
        def _():
            @pl.when(b == 0)
            def _():
                issue_range(gather_row, row0_ref[0], 0, nsub * sub)

            gather_wait(nsub * sub)

            def cast(s, c):
                rows = pl.ds(pl.multiple_of(s * sub, sub), sub)
                xb_ref[rows, :] = xg_ref[rows, :].astype(BF16)
                return c

            lax.fori_loop(0, nsub, cast, 0)

            @pl.when(b + 1 < nb)
            def _():
                nxt = jnp.minimum(b + 1, nb - 1)
                issue_tiles(gather_row, row0_ref[nxt], nsub_ref[nxt])

        def compute(m):
            rows = slice(0, m * sub)
            xs = xb_ref[rows, :]
            hb = (jax.nn.silu(_dot(xs, wg_ref[...])) * _dot(xs, wu_ref[...])).astype(BF16)

            @pl.when((f == 0) & (b > 0))
            def _():
                scatter_wait(jnp.maximum(b - 1, 0))

            yp = _dot(hb, wd_ref[...])
            yacc_ref[rows, :] = jnp.where(f == 0, yp, yacc_ref[rows, :] + yp)

        for m in range(1, xb_ref.shape[0] // sub + 1):
            pl.when(nsub == m)(functools.partial(compute, m))

        @pl.when(f == nf - 1)
        def _():
            full = nreal_ref[b] // sub
            issue_tiles(scatter_row, row0_ref[b], full)
            issue_range(scatter_row, row0_ref[b], full * sub, nreal_ref[b])

            @pl.when(b == nb - 1)
            def _():
                scatter_wait(b)


def _moe(be, row0, nsub, nreal, asg, h1, w_gate, w_up, w_down, n_out_rows):
    t, d = h1.shape
    ne, _, de = w_gate.shape
    nb = be.shape[0]
    nf = de // MOE_TF
    assert de % MOE_TF == 0 and t & (t - 1) == 0
    last = nf - 1
    fsel = lambda f, n: jnp.where(n > 0, f, last)
    grid_spec = pltpu.PrefetchScalarGridSpec(
        num_scalar_prefetch=5,
        grid=(nb, nf),
        in_specs=[
            pl.BlockSpec(memory_space=pl.ANY),
            pl.BlockSpec((None, d, MOE_TF), lambda b, f, be, r0, ns, nr, asg: (be[b], 0, fsel(f, ns[b]))),
            pl.BlockSpec((None, d, MOE_TF), lambda b, f, be, r0, ns, nr, asg: (be[b], 0, fsel(f, ns[b]))),
            pl.BlockSpec((None, MOE_TF, d), lambda b, f, be, r0, ns, nr, asg: (be[b], fsel(f, ns[b]), 0)),
        ],
        out_specs=pl.BlockSpec(memory_space=pl.ANY),
        scratch_shapes=[pltpu.VMEM((MOE_ROWS, d), F32),
                        pltpu.VMEM((MOE_ROWS, d), BF16),
                        pltpu.VMEM((MOE_ROWS, d), F32),
                        pltpu.SemaphoreType.DMA,
                        pltpu.SemaphoreType.DMA],
    )
    return pl.pallas_call(
        functools.partial(_moe_kernel, nb=nb, nf=nf, sub=MOE_SUB, tok_mask=t - 1),
        grid_spec=grid_spec,
        out_shape=jax.ShapeDtypeStruct((n_out_rows, d), F32),
        compiler_params=_params("arbitrary", "arbitrary", vmem=MOE_VMEM_LIMIT),
        name="moe",
    )(be, row0, nsub, nreal, asg, h1, w_gate, w_up, w_down)


def _combine_gate_kernel(h1_ref, y0_ref, y1_ref, pk_ref, g_ref, b_ref, wpg_ref, bpg_ref, p_ref, wpe_ref,
                         o_ref, za_ref, zb_ref, ha_ref, hb_ref, *, nj, tn, alpha):
    n = pl.program_id(0)
    j = n % nj
    parity = (n // nj) % 2

    @pl.when(n == 0)
    def _():
        for ref in (za_ref, zb_ref, ha_ref, hb_ref):
            ref[...] = jnp.zeros(ref.shape, ref.dtype)

    def layer_norm_tile(z_ref, h_ref):
        d = nj * tn
        mu = sum(jnp.sum(z_ref[jj], axis=-1, keepdims=True) for jj in range(nj)) / d
        var = sum(jnp.sum(jnp.square(z_ref[jj] - mu), axis=-1, keepdims=True) for jj in range(nj)) / d
        rstd = lax.rsqrt(var + LN_EPS)
        for jj in range(nj):
            cols = slice(jj * tn, (jj + 1) * tn)
            h = (z_ref[jj] - mu) * rstd * g_ref[:, cols] + b_ref[:, cols]
            z_ref[jj] = h
            h_ref[:, cols] = h.astype(BF16)

    def step(zw_ref, hw_ref, zo_ref, ho_ref, first):
        pk = pk_ref[...]
        ffn = pk[:, 2:3] * y0_ref[...] + pk[:, 3:4] * y1_ref[...]
        z = alpha * h1_ref[...] + ffn
        if first:
            zw_ref[0] = z
            layer_norm_tile(zo_ref, ho_ref)
            zg, hg = zw_ref[nj - 1], hw_ref[...]
        else:
            zw_ref[j] = z
            zg, hg = zo_ref[j - 1], ho_ref[...]
        gate = jax.nn.sigmoid(_dot(hg, wpg_ref[...]) + bpg_ref[...])
        ple = _dot(p_ref[...].astype(BF16), wpe_ref[...])
        o_ref[...] = zg + gate * ple

    slots = ((za_ref, ha_ref, zb_ref, hb_ref), (zb_ref, hb_ref, za_ref, ha_ref))
    for par in range(2):
        for first in (True, False):
            cond = (parity == par) & ((j == 0) if first else (j > 0))
            pl.when(cond)(functools.partial(step, *slots[par], first))


def _combine_gate(h1, y2, pk, g, b, w_pg_b, b_pg, p2, w_pe_b, alpha):
    t, d = h1.shape
    pd = p2.shape[1]
    tm, tn = MIX_TM, OUT_TN
    assert t % tm == 0 and d % tn == 0
    ni, nj = t // tm, d // tn
    n_comb = ni * nj
    cmb = lambda n: jnp.minimum(n, n_comb - 1)
    gat = lambda n: jnp.clip(n - nj - 1, 0, n_comb - 1)
    return pl.pallas_call(
        functools.partial(_combine_gate_kernel, nj=nj, tn=tn, alpha=alpha),
        grid=(n_comb + nj + 1,),
        in_specs=[
            pl.BlockSpec((tm, tn), lambda n: (cmb(n) // nj, cmb(n) % nj)),
            pl.BlockSpec((tm, tn), lambda n: (cmb(n) // nj, cmb(n) % nj)),
            pl.BlockSpec((tm, tn), lambda n: (cmb(n) // nj + ni, cmb(n) % nj)),
            pl.BlockSpec((tm, LANE), lambda n: (cmb(n) // nj, 0)),
            pl.BlockSpec((1, d), lambda n: (0, 0)),
            pl.BlockSpec((1, d), lambda n: (0, 0)),
            pl.BlockSpec((d, tn), lambda n: (0, gat(n) % nj)),
            pl.BlockSpec((1, tn), lambda n: (0, gat(n) % nj)),
            pl.BlockSpec((tm, pd), lambda n: (gat(n) // nj, 0)),
            pl.BlockSpec((pd, tn), lambda n: (0, gat(n) % nj)),
        ],
        out_specs=pl.BlockSpec((tm, tn), lambda n: (gat(n) // nj, gat(n) % nj)),
        out_shape=jax.ShapeDtypeStruct((t, d), F32),
        scratch_shapes=[pltpu.VMEM((nj, tm, tn), F32), pltpu.VMEM((nj, tm, tn), F32),
                        pltpu.VMEM((tm, d), BF16), pltpu.VMEM((tm, d), BF16)],
        compiler_params=_params("arbitrary"),
        name="combine_gate",
    )(h1, y2, y2, pk, g, b, w_pg_b, b_pg, p2, w_pe_b)


def _block_table(counts, n_asg):
    ne = counts.shape[0]
    padded = ((counts + MOE_SUB - 1) // MOE_SUB) * MOE_SUB
    pad_end = jnp.cumsum(padded)
    pad_start = pad_end - padded
    nblk = (padded + MOE_ROWS - 1) // MOE_ROWS
    blk_end = jnp.cumsum(nblk)
    blk_start = blk_end - nblk
    nb = ne + -(-n_asg // MOE_ROWS)
    bidx = jnp.arange(nb, dtype=jnp.int32)
    used = bidx < blk_end[-1]
    be = jnp.minimum(jnp.searchsorted(blk_end, jnp.minimum(bidx, blk_end[-1] - 1), side="right"), ne - 1)
    be = be.astype(jnp.int32)
    within = bidx - blk_start[be]
    row0 = pad_start[be] + within * MOE_ROWS
    nsub = jnp.clip((padded[be] - within * MOE_ROWS) // MOE_SUB, 0, MOE_ROWS // MOE_SUB)
    nsub = jnp.where(used, nsub, 0)
    row0 = jnp.where(used, row0, 0)
    nreal = jnp.where(used, jnp.clip(counts[be] - within * MOE_ROWS, 0, MOE_ROWS), 0)
    i32 = lambda a: a.astype(jnp.int32)
    return i32(pad_start), be, i32(row0), i32(nsub), i32(nreal)


def kernel(x, p, w_in, conv_w, sg_ln_g, sg_ln_b, sg_w, sg_b, w_out, ln1_g, ln1_b, w_rg, b_rg, w_re, b_re,
           w_gate, w_up, w_down, ln2_g, ln2_b, w_pg, b_pg, w_pe):
    depth = w_in.shape[0]
    bsz, seq, d = x.shape
    t = bsz * seq
    alpha = (2 * depth) ** 0.25
    n_groups = w_rg.shape[-1]
    ne = w_re.shape[-1]
    per_group = ne // n_groups
    n_asg = t * TOP_K
    n_rows = ((n_asg + ne * (MOE_SUB - 1) + MOE_SUB - 1) // MOE_SUB) * MOE_SUB

    h = x.reshape(t, d)
    for i in range(depth):
        cdim = conv_w.shape[-1]
        nj = cdim // MIX_TN
        w_in_p = _regroup_cols(w_in[i], 5, nj, MIX_TN)
        w_out_p = _regroup_rows(w_out[i], 2, nj, MIX_TN)
        y = _mixer(h, w_in_p, conv_w[i], sg_ln_g[i].reshape(1, -1), sg_ln_b[i].reshape(1, -1),
                   sg_w[i], sg_b[i][:, :, None], seq)
        wr = jnp.concatenate([w_rg[i], w_re[i], jnp.zeros((d, LANE - n_groups - ne), F32)], axis=1).astype(BF16)
        br = jnp.concatenate([b_rg[i], b_re[i], jnp.zeros((LANE - n_groups - ne,), F32)]).reshape(1, LANE)
        h1, logits = _outproj(y, w_out_p, h, ln1_g[i].reshape(1, d), ln1_b[i].reshape(1, d), wr, br, alpha)
        pk, cnt = _route(logits, n_groups, per_group)
        pad_start, be, row0, nsub, nreal = _block_table(cnt[0, :ne].astype(jnp.int32), n_asg)
        eid = jnp.concatenate([pk[:, 0], pk[:, 1]]).astype(jnp.int32)
        rank = jnp.concatenate([pk[:, 4], pk[:, 5]]).astype(jnp.int32)
        asg = _rowmap(eid, rank, pad_start, n_rows)
        y2 = _moe(be, row0, nsub, nreal, asg, h1, w_gate[i], w_up[i], w_down[i], n_asg)
        h = _combine_gate(h1, y2, pk, ln2_g[i].reshape(1, d), ln2_b[i].reshape(1, d), w_pg[i].astype(BF16),
                          b_pg[i].reshape(1, d), p[i].reshape(t, -1), w_pe[i].astype(BF16), alpha)
    return h.reshape(bsz, seq, d)
```

```python
import functools

import jax
import jax.numpy as jnp
from jax import lax
from jax.experimental import pallas as pl
from jax.experimental.pallas import tpu as pltpu

F32 = jnp.float32
BF16 = jnp.bfloat16

LANE = 128
SUBLANE = 8
MXU_COL = 256
VMEM_LIMIT = 56 * 1024 * 1024
MOE_VMEM_LIMIT = 60 * 1024 * 1024

CHUNK = 64
SG_BLOCK = 128
LN_EPS = 1e-5
TOP_K = 2

MIX_TM = 512
MIX_TN = MXU_COL
OUT_TN = 512
ROUTE_TM = 512
MOE_SUB = 128
MOE_ROWS = 768
MOE_TF = MXU_COL
ISSUE_UNROLL = 8
CAST_TK = 4096
CAST_ROWS = 16

_dot = functools.partial(jnp.dot, preferred_element_type=F32)


def _params(*sem, vmem=VMEM_LIMIT):
    return pltpu.CompilerParams(dimension_semantics=sem, vmem_limit_bytes=vmem)


def _layer_norm(z, g, b):
    mu = jnp.mean(z, axis=-1, keepdims=True)
    zc = z - mu
    var = jnp.mean(zc * zc, axis=-1, keepdims=True)
    return zc * lax.rsqrt(var + LN_EPS) * g + b


def _cast_kernel(w_ref, o_ref):
    o_ref[...] = w_ref[...].astype(o_ref.dtype)


def _regroup_cols(w, groups, nj, tn):
    d = w.shape[0]
    tk = min(CAST_TK, d)
    assert d % tk == 0 and w.shape[1] == groups * nj * tn
    return pl.pallas_call(
        _cast_kernel,
        grid=(nj, groups, d // tk),
        in_specs=[pl.BlockSpec((tk, tn), lambda j, s, k: (k, s * nj + j))],
        out_specs=pl.BlockSpec((tk, tn), lambda j, s, k: (k, j * groups + s)),
        out_shape=jax.ShapeDtypeStruct(w.shape, BF16),
        compiler_params=_params("arbitrary", "arbitrary", "arbitrary"),
        name="regroup_cols",
    )(w)


def _mixer_kernel(x_ref, w_ref, cw_ref, lng_ref, lnb_ref, sgw_ref, sgb_ref, wo_ref, wg_ref,
                  y_ref, wo_out_ref, wg_out_ref,
                  xb_ref, pa_ref, pb_ref, gbuf_ref, carry_ref, *, tm, tn, nj, n_steps, tiles_per_seq):
    n = pl.program_id(0)

    @pl.when(n == 0)
    def _():
        pb_ref[...] = jnp.zeros(pb_ref.shape, F32)
        carry_ref[...] = jnp.zeros(carry_ref.shape, F32)

    @pl.when((n % nj == 0) & (n < n_steps))
    def _():
        xb_ref[...] = x_ref[...].astype(BF16)

    m = jnp.maximum(n - 1, 0)
    jp = m % nj
    first = ((m // nj) % tiles_per_seq) == 0

    def mix(proj_ref):
        b, c, h, u, v = (proj_ref[:, k * tn:(k + 1) * tn] for k in range(5))
        g = c * h
        gbuf_ref[0:SUBLANE, :] = jnp.where(first, 0.0, carry_ref[jp])
        gbuf_ref[SUBLANE:, :] = g
        carry_ref[jp] = g[tm - SUBLANE:, :]
        g1 = gbuf_ref[pl.ds(SUBLANE - 1, tm), :]
        g2 = gbuf_ref[pl.ds(SUBLANE - 2, tm), :]
        cw = cw_ref[...]
        conv = cw[0:1, :] * g2 + cw[1:2, :] * g1 + cw[2:3, :] * g
        y_ref[:, 0:tn] = (b * conv).astype(y_ref.dtype)

        pos_i = lax.broadcasted_iota(jnp.int32, (SG_BLOCK, SG_BLOCK), 0)
        pos_j = lax.broadcasted_iota(jnp.int32, (SG_BLOCK, SG_BLOCK), 1)
        mask = (pos_j // CHUNK) <= (pos_i // CHUNK)
        for hh in range(tn // LANE):
            sl = slice(hh * LANE, (hh + 1) * LANE)
            vn = _layer_norm(jax.nn.gelu(v[:, sl]), lng_ref[:, sl], lnb_ref[:, sl]).astype(BF16)
            gu = jax.nn.gelu(u[:, sl])
            ws = jnp.where(mask, sgw_ref[hh], 0.0).astype(BF16)
            bcol = sgb_ref[hh]
            nblk = tm // SG_BLOCK
            vcat = jnp.concatenate([vn[r * SG_BLOCK:(r + 1) * SG_BLOCK, :] for r in range(nblk)], axis=1)
            sg = _dot(ws, vcat) + bcol
            for r in range(nblk):
                rows = slice(r * SG_BLOCK, (r + 1) * SG_BLOCK)
                y_ref[rows, tn + hh * LANE:tn + (hh + 1) * LANE] = (
                    gu[rows, :] * sg[:, r * LANE:(r + 1) * LANE]).astype(y_ref.dtype)

    def step(store_ref, load_ref):
        store_ref[...] = _dot(xb_ref[...], w_ref[...])
        mix(load_ref)
        wo_out_ref[...] = wo_ref[...].astype(BF16)
        wg_out_ref[...] = wg_ref[...].astype(BF16)

    pl.when(n % 2 == 0)(functools.partial(step, pa_ref, pb_ref))
    pl.when(n % 2 == 1)(functools.partial(step, pb_ref, pa_ref))


def _mixer(x2, w_in_p, conv_w, lng, lnb, sg_w, sg_bcol, w_out, w_pg, seq):
    t, d = x2.shape
    cdim = conv_w.shape[1]
    tm, tn = MIX_TM, MIX_TN
    nj = cdim // tn
    assert seq % tm == 0 and t % tm == 0 and cdim % tn == 0 and w_in_p.shape[1] == 5 * cdim
    hp = tn // LANE
    n_steps = (t // tm) * nj
    cur = lambda n: jnp.minimum(n, n_steps - 1)
    prv = lambda n: jnp.maximum(n - 1, 0)
    wrows = w_out.shape[0]
    cb = max(CAST_ROWS, -(-wrows // n_steps // CAST_ROWS) * CAST_ROWS)
    nblk = wrows // cb
    spb = n_steps // nblk
    bpt = tn // cb
    assert w_pg.shape == w_out.shape == (2 * cdim, d) and wrows % cb == 0 and n_steps % nblk == 0 and tn % cb == 0
    blk = lambda n: jnp.minimum(n // spb, nblk - 1)

    def wo_src(n):
        k = blk(n)
        tile, sub = k // bpt, k % bpt
        return ((tile % 2) * nj + tile // 2) * bpt + sub, 0

    return pl.pallas_call(
        functools.partial(_mixer_kernel, tm=tm, tn=tn, nj=nj, n_steps=n_steps, tiles_per_seq=seq // tm),
        grid=(n_steps + 1,),
        in_specs=[
            pl.BlockSpec((tm, d), lambda n: (cur(n) // nj, 0)),
            pl.BlockSpec((d, 5 * tn), lambda n: (0, cur(n) % nj)),
            pl.BlockSpec((conv_w.shape[0], tn), lambda n: (0, prv(n) % nj)),
            pl.BlockSpec((1, tn), lambda n: (0, prv(n) % nj)),
            pl.BlockSpec((1, tn), lambda n: (0, prv(n) % nj)),
            pl.BlockSpec((hp, SG_BLOCK, SG_BLOCK), lambda n: (prv(n) % nj, 0, 0)),
            pl.BlockSpec((hp, SG_BLOCK, 1), lambda n: (prv(n) % nj, 0, 0)),
            pl.BlockSpec((cb, d), wo_src),
            pl.BlockSpec((cb, d), lambda n: (blk(n), 0)),
        ],
        out_specs=[pl.BlockSpec((tm, 2 * tn), lambda n: (prv(n) // nj, prv(n) % nj)),
                   pl.BlockSpec((cb, d), lambda n: (blk(n), 0)),
                   pl.BlockSpec((cb, d), lambda n: (blk(n), 0))],
        out_shape=[jax.ShapeDtypeStruct((t, 2 * cdim), BF16),
                   jax.ShapeDtypeStruct(w_out.shape, BF16),
                   jax.ShapeDtypeStruct(w_pg.shape, BF16)],
        scratch_shapes=[pltpu.VMEM((tm, d), BF16),
                        pltpu.VMEM((tm, 5 * tn), F32),
                        pltpu.VMEM((tm, 5 * tn), F32),
                        pltpu.VMEM((tm + SUBLANE, tn), F32),
                        pltpu.VMEM((nj, SUBLANE, tn), F32)],
        compiler_params=_params("arbitrary"),
        name="mixer",
    )(x2, w_in_p, conv_w, lng, lnb, sg_w, sg_bcol, w_out, w_pg)


def _outproj_kernel(y_ref, w_ref, x_ref, g_ref, b_ref, wr_ref, br_ref, h1_ref, lg_ref, z_ref, *, tn, nj, alpha):
    j = pl.program_id(1)
    z_ref[j] = alpha * x_ref[...] + _dot(y_ref[...], w_ref[...])

    @pl.when(j == nj - 1)
    def _():
        d = nj * tn
        mu = sum(jnp.sum(z_ref[jj], axis=-1, keepdims=True) for jj in range(nj)) / d
        var = sum(jnp.sum(jnp.square(z_ref[jj] - mu), axis=-1, keepdims=True) for jj in range(nj)) / d
        rstd = lax.rsqrt(var + LN_EPS)
        logits = br_ref[...]
        for jj in range(nj):
            cols = slice(jj * tn, (jj + 1) * tn)
            h = (z_ref[jj] - mu) * rstd * g_ref[:, cols] + b_ref[:, cols]
            h1_ref[:, cols] = h
            logits = logits + _dot(h.astype(BF16), wr_ref[cols, :])
        lg_ref[...] = logits


def _outproj(y, w_out_p, x2, g, b, wr, br, alpha):
    t, d = x2.shape
    kdim = y.shape[1]
    tm, tn = MIX_TM, OUT_TN
    nj = d // tn
    assert t % tm == 0 and d % tn == 0 and w_out_p.shape[0] == kdim
    return pl.pallas_call(
        functools.partial(_outproj_kernel, tn=tn, nj=nj, alpha=alpha),
        grid=(t // tm, nj),
        in_specs=[
            pl.BlockSpec((tm, kdim), lambda i, j: (i, 0)),
            pl.BlockSpec((kdim, tn), lambda i, j: (0, j)),
            pl.BlockSpec((tm, tn), lambda i, j: (i, j)),
            pl.BlockSpec((1, d), lambda i, j: (0, 0)),
            pl.BlockSpec((1, d), lambda i, j: (0, 0)),
            pl.BlockSpec((d, LANE), lambda i, j: (0, 0)),
            pl.BlockSpec((1, LANE), lambda i, j: (0, 0)),
        ],
        out_specs=[pl.BlockSpec((tm, d), lambda i, j: (i, 0)),
                   pl.BlockSpec((tm, LANE), lambda i, j: (i, 0))],
        out_shape=[jax.ShapeDtypeStruct((t, d), F32), jax.ShapeDtypeStruct((t, LANE), F32)],
        scratch_shapes=[pltpu.VMEM((nj, tm, tn), F32)],
        compiler_params=_params("arbitrary", "arbitrary"),
        name="outproj",
    )(y, w_out_p, x2, g, b, wr, br)


def _route_kernel(lg_ref, pk_ref, cnt_ref, run_ref, *, n_groups, per_group):
    i = pl.program_id(0)

    @pl.when(i == 0)
    def _():
        run_ref[...] = jnp.zeros_like(run_ref)

    l = lg_ref[...]
    tm = l.shape[0]
    lane = lax.broadcasted_iota(jnp.int32, l.shape, 1)
    neg = jnp.float32(-jnp.inf)
    gmask = lane < n_groups
    gl = jnp.where(gmask, l, neg)
    gmax = jnp.max(gl, axis=-1, keepdims=True)
    gi = jnp.min(jnp.where(gl == gmax, lane, LANE), axis=-1, keepdims=True)
    gp = 1.0 / jnp.sum(jnp.where(gmask, jnp.exp(l - gmax), 0.0), axis=-1, keepdims=True)
    elane = lane - n_groups
    emask = (elane >= 0) & (elane // per_group == gi)
    el = jnp.where(emask, l, neg)
    m1 = jnp.max(el, axis=-1, keepdims=True)
    i1 = jnp.min(jnp.where(el == m1, lane, LANE), axis=-1, keepdims=True)
    el2 = jnp.where(lane == i1, neg, el)
    m2 = jnp.max(el2, axis=-1, keepdims=True)
    i2 = jnp.min(jnp.where(el2 == m2, lane, LANE), axis=-1, keepdims=True)
    t2 = jnp.exp(m2 - m1)
    w0 = gp / (1.0 + t2)
    w1 = gp * t2 / (1.0 + t2)
    e0 = i1 - n_groups
    e1 = i2 - n_groups
    oh0 = lane == e0
    oh1 = lane == e1
    oh = (oh0 | oh1).astype(BF16)
    row = lax.broadcasted_iota(jnp.int32, (tm, tm), 0)
    col = lax.broadcasted_iota(jnp.int32, (tm, tm), 1)
    tri = (row > col).astype(BF16)
    before = run_ref[...] + _dot(tri, oh)
    r0 = jnp.sum(jnp.where(oh0, before, 0.0), axis=-1, keepdims=True)
    r1 = jnp.sum(jnp.where(oh1, before, 0.0), axis=-1, keepdims=True)
    total = run_ref[...] + jnp.sum(oh.astype(F32), axis=0, keepdims=True)
    run_ref[...] = total
    cnt_ref[...] = total
    vals = (e0.astype(F32), e1.astype(F32), w0, w1, r0, r1)
    pk = jnp.zeros(l.shape, F32)
    for k, val in enumerate(vals):
        pk = jnp.where(lane == k, val, pk)
    pk_ref[...] = pk


def _route(logits, n_groups, per_group):
    t = logits.shape[0]
    tm = ROUTE_TM
    assert t % tm == 0 and n_groups * (1 + per_group) <= LANE
    return pl.pallas_call(
        functools.partial(_route_kernel, n_groups=n_groups, per_group=per_group),
        grid=(t // tm,),
        in_specs=[pl.BlockSpec((tm, LANE), lambda i: (i, 0))],
        out_specs=[pl.BlockSpec((tm, LANE), lambda i: (i, 0)),
                   pl.BlockSpec((1, LANE), lambda i: (0, 0))],
        out_shape=[jax.ShapeDtypeStruct((t, LANE), F32), jax.ShapeDtypeStruct((1, LANE), F32)],
        scratch_shapes=[pltpu.VMEM((1, LANE), F32)],
        compiler_params=_params("arbitrary"),
        name="route",
    )(logits)


def _rowmap_kernel(eid_ref, rank_ref, start_ref, asg_ref, *, n_rows, n_asg):
    def fill(r, c):
        asg_ref[r] = 0
        return c

    lax.fori_loop(0, n_rows, fill, 0, unroll=ISSUE_UNROLL)

    def place(a, c):
        asg_ref[start_ref[eid_ref[a]] + rank_ref[a]] = a
        return c

    lax.fori_loop(0, n_asg, place, 0, unroll=ISSUE_UNROLL)


def _rowmap(eid, rank, start, n_rows):
    n_asg = eid.shape[0]
    smem = pl.BlockSpec(memory_space=pltpu.SMEM)
    return pl.pallas_call(
        functools.partial(_rowmap_kernel, n_rows=n_rows, n_asg=n_asg),
        in_specs=[smem, smem, smem],
        out_specs=smem,
        out_shape=jax.ShapeDtypeStruct((n_rows,), jnp.int32),
        name="rowmap",
    )(eid, rank, start)


def _moe_kernel(be_ref, row0_ref, nsub_ref, nreal_ref, asg_ref, h1_hbm, wg_ref, wu_ref, wd_ref, y_hbm,
                xg_ref, xb_ref, yacc_ref, gsem, ssem, *, nb, nf, sub, tok_mask):
    b = pl.program_id(0)
    f = pl.program_id(1)
    nsub = nsub_ref[b]
    max_tiles = xb_ref.shape[0] // sub

    def gather_row(base, r):
        tok = asg_ref[base + r] & tok_mask
        pltpu.make_async_copy(h1_hbm.at[pl.ds(tok, 1)], xg_ref.at[pl.ds(r, 1)], gsem).start()

    def scatter_row(base, r):
        dst = asg_ref[base + r]
        pltpu.make_async_copy(yacc_ref.at[pl.ds(r, 1)], y_hbm.at[pl.ds(dst, 1)], ssem).start()

    def issue_tiles(row_fn, base, n_tiles):
        for s in range(max_tiles):
            @pl.when(s < n_tiles)
            def _(s=s):
                for r in range(s * sub, (s + 1) * sub):
                    row_fn(base, r)

    def issue_range(row_fn, base, lo, hi):
        groups = (hi - lo) // ISSUE_UNROLL

        def body(q, c):
            for k in range(ISSUE_UNROLL):
                row_fn(base, lo + q * ISSUE_UNROLL + k)
            return c

        lax.fori_loop(0, groups, body, 0)

        def tail(r, c):
            row_fn(base, r)
            return c

        lax.fori_loop(lo + groups * ISSUE_UNROLL, hi, tail, 0)

    def wait_rows(n, tile_copy, row_copy):
        def tile(s, c):
            tile_copy.wait()
            return c

        lax.fori_loop(0, n // sub, tile, 0)

        def row(r, c):
            row_copy.wait()
            return c

        lax.fori_loop((n // sub) * sub, n, row, 0)

    def gather_wait(n):
        wait_rows(n, pltpu.make_async_copy(h1_hbm.at[pl.ds(0, sub)], xg_ref.at[pl.ds(0, sub)], gsem),
                  pltpu.make_async_copy(h1_hbm.at[pl.ds(0, 1)], xg_ref.at[pl.ds(0, 1)], gsem))

    def scatter_wait(blk):
        wait_rows(nreal_ref[blk], pltpu.make_async_copy(yacc_ref.at[pl.ds(0, sub)], y_hbm.at[pl.ds(0, sub)], ssem),
                  pltpu.make_async_copy(yacc_ref.at[pl.ds(0, 1)], y_hbm.at[pl.ds(0, 1)], ssem))

    @pl.when((f == 0) & (b == 0))
    def _():
        yacc_ref[...] = jnp.zeros(yacc_ref.shape, F32)

    @pl.when((f == 0) & (b > 0) & (nsub == 0))
    def _():
        scatter_wait(jnp.maximum(b - 1, 0))

    @pl.when(nsub > 0)
    def _():
        @pl.when(f == 0)
        def _():
            @pl.when(b == 0)
            def _():
                issue_range(gather_row, row0_ref[0], 0, nsub * sub)

            gather_wait(nsub * sub)

            def cast(s, c):
                rows = pl.ds(pl.multiple_of(s * sub, sub), sub)
                xb_ref[rows, :] = xg_ref[rows, :].astype(BF16)
                return c

            lax.fori_loop(0, nsub, cast, 0)

            @pl.when(b + 1 < nb)
            def _():
                nxt = jnp.minimum(b + 1, nb - 1)
                issue_tiles(gather_row, row0_ref[nxt], nsub_ref[nxt])

        def compute(m):
            rows = slice(0, m * sub)
            xs = xb_ref[rows, :]
            hb = (jax.nn.silu(_dot(xs, wg_ref[...])) * _dot(xs, wu_ref[...])).astype(BF16)

            @pl.when((f == 0) & (b > 0))
            def _():
                scatter_wait(jnp.maximum(b - 1, 0))

            yp = _dot(hb, wd_ref[...])
            yacc_ref[rows, :] = jnp.where(f == 0, yp, yacc_ref[rows, :] + yp)

        for m in range(1, xb_ref.shape[0] // sub + 1):
            pl.when(nsub == m)(functools.partial(compute, m))

        @pl.when(f == nf - 1)
        def _():
            full = nreal_ref[b] // sub
            issue_tiles(scatter_row, row0_ref[b], full)
            issue_range(scatter_row, row0_ref[b], full * sub, nreal_ref[b])

            @pl.when(b == nb - 1)
            def _():
                scatter_wait(b)


def _moe(be, row0, nsub, nreal, asg, h1, w_gate, w_up, w_down, n_out_rows):
    t, d = h1.shape
    ne, _, de = w_gate.shape
    nb = be.shape[0]
    nf = de // MOE_TF
    assert de % MOE_TF == 0 and t & (t - 1) == 0
    last = nf - 1
    fsel = lambda f, n: jnp.where(n > 0, f, last)
    grid_spec = pltpu.PrefetchScalarGridSpec(
        num_scalar_prefetch=5,
        grid=(nb, nf),
        in_specs=[
            pl.BlockSpec(memory_space=pl.ANY),
            pl.BlockSpec((None, d, MOE_TF), lambda b, f, be, r0, ns, nr, asg: (be[b], 0, fsel(f, ns[b]))),
            pl.BlockSpec((None, d, MOE_TF), lambda b, f, be, r0, ns, nr, asg: (be[b], 0, fsel(f, ns[b]))),
            pl.BlockSpec((None, MOE_TF, d), lambda b, f, be, r0, ns, nr, asg: (be[b], fsel(f, ns[b]), 0)),
        ],
        out_specs=pl.BlockSpec(memory_space=pl.ANY),
        scratch_shapes=[pltpu.VMEM((MOE_ROWS, d), F32),
                        pltpu.VMEM((MOE_ROWS, d), BF16),
                        pltpu.VMEM((MOE_ROWS, d), F32),
                        pltpu.SemaphoreType.DMA,
                        pltpu.SemaphoreType.DMA],
    )
    return pl.pallas_call(
        functools.partial(_moe_kernel, nb=nb, nf=nf, sub=MOE_SUB, tok_mask=t - 1),
        grid_spec=grid_spec,
        out_shape=jax.ShapeDtypeStruct((n_out_rows, d), F32),
        compiler_params=_params("arbitrary", "arbitrary", vmem=MOE_VMEM_LIMIT),
        name="moe",
    )(be, row0, nsub, nreal, asg, h1, w_gate, w_up, w_down)


def _combine_gate_kernel(h1_ref, y0_ref, y1_ref, pk_ref, g_ref, b_ref, wpg_ref, bpg_ref, p_ref, wpe_ref,
                         o_ref, za_ref, zb_ref, ha_ref, hb_ref, *, nj, tn, alpha):
    n = pl.program_id(0)
    j = n % nj
    parity = (n // nj) % 2

    @pl.when(n == 0)
    def _():
        for ref in (za_ref, zb_ref, ha_ref, hb_ref):
            ref[...] = jnp.zeros(ref.shape, ref.dtype)

    def layer_norm_tile(z_ref, h_ref):
        d = nj * tn
        mu = sum(jnp.sum(z_ref[jj], axis=-1, keepdims=True) for jj in range(nj)) / d
        var = sum(jnp.sum(jnp.square(z_ref[jj] - mu), axis=-1, keepdims=True) for jj in range(nj)) / d
        rstd = lax.rsqrt(var + LN_EPS)
        for jj in range(nj):
            cols = slice(jj * tn, (jj + 1) * tn)
            h = (z_ref[jj] - mu) * rstd * g_ref[:, cols] + b_ref[:, cols]
            z_ref[jj] = h
            h_ref[:, cols] = h.astype(BF16)

    def step(zw_ref, hw_ref, zo_ref, ho_ref, first):
        pk = pk_ref[...]
        ffn = pk[:, 2:3] * y0_ref[...] + pk[:, 3:4] * y1_ref[...]
        z = alpha * h1_ref[...] + ffn
        if first:
            zw_ref[0] = z
            layer_norm_tile(zo_ref, ho_ref)
            zg, hg = zw_ref[nj - 1], hw_ref[...]
        else:
            zw_ref[j] = z
            zg, hg = zo_ref[j - 1], ho_ref[...]
        gate = jax.nn.sigmoid(_dot(hg, wpg_ref[...]) + bpg_ref[...])
        ple = _dot(p_ref[...].astype(BF16), wpe_ref[...])
        o_ref[...] = zg + gate * ple

    slots = ((za_ref, ha_ref, zb_ref, hb_ref), (zb_ref, hb_ref, za_ref, ha_ref))
    for par in range(2):
        for first in (True, False):
            cond = (parity == par) & ((j == 0) if first else (j > 0))
            pl.when(cond)(functools.partial(step, *slots[par], first))


def _combine_gate(h1, y2, pk, g, b, w_pg_b, b_pg, p2, w_pe_b, alpha):
    t, d = h1.shape
    pd = p2.shape[1]
    tm, tn = MIX_TM, OUT_TN
    assert t % tm == 0 and d % tn == 0
    ni, nj = t // tm, d // tn
    n_comb = ni * nj
    cmb = lambda n: jnp.minimum(n, n_comb - 1)
    gat = lambda n: jnp.clip(n - nj - 1, 0, n_comb - 1)
    return pl.pallas_call(
        functools.partial(_combine_gate_kernel, nj=nj, tn=tn, alpha=alpha),
        grid=(n_comb + nj + 1,),
        in_specs=[
            pl.BlockSpec((tm, tn), lambda n: (cmb(n) // nj, cmb(n) % nj)),
            pl.BlockSpec((tm, tn), lambda n: (cmb(n) // nj, cmb(n) % nj)),
            pl.BlockSpec((tm, tn), lambda n: (cmb(n) // nj + ni, cmb(n) % nj)),
            pl.BlockSpec((tm, LANE), lambda n: (cmb(n) // nj, 0)),
            pl.BlockSpec((1, d), lambda n: (0, 0)),
            pl.BlockSpec((1, d), lambda n: (0, 0)),
            pl.BlockSpec((d, tn), lambda n: (0, gat(n) % nj)),
            pl.BlockSpec((1, tn), lambda n: (0, gat(n) % nj)),
            pl.BlockSpec((tm, pd), lambda n: (gat(n) // nj, 0)),
            pl.BlockSpec((pd, tn), lambda n: (0, gat(n) % nj)),
        ],
        out_specs=pl.BlockSpec((tm, tn), lambda n: (gat(n) // nj, gat(n) % nj)),
        out_shape=jax.ShapeDtypeStruct((t, d), F32),
        scratch_shapes=[pltpu.VMEM((nj, tm, tn), F32), pltpu.VMEM((nj, tm, tn), F32),
                        pltpu.VMEM((tm, d), BF16), pltpu.VMEM((tm, d), BF16)],
        compiler_params=_params("arbitrary"),
        name="combine_gate",
    )(h1, y2, y2, pk, g, b, w_pg_b, b_pg, p2, w_pe_b)


def _block_table(counts, n_asg):
    ne = counts.shape[0]
    padded = ((counts + MOE_SUB - 1) // MOE_SUB) * MOE_SUB
    pad_end = jnp.cumsum(padded)
    pad_start = pad_end - padded
    nblk = (padded + MOE_ROWS - 1) // MOE_ROWS
    blk_end = jnp.cumsum(nblk)
    blk_start = blk_end - nblk
    nb = ne + -(-n_asg // MOE_ROWS)
    bidx = jnp.arange(nb, dtype=jnp.int32)
    used = bidx < blk_end[-1]
    be = jnp.minimum(jnp.searchsorted(blk_end, jnp.minimum(bidx, blk_end[-1] - 1), side="right"), ne - 1)
    be = be.astype(jnp.int32)
    within = bidx - blk_start[be]
    row0 = pad_start[be] + within * MOE_ROWS
    nsub = jnp.clip((padded[be] - within * MOE_ROWS) // MOE_SUB, 0, MOE_ROWS // MOE_SUB)
    nsub = jnp.where(used, nsub, 0)
    row0 = jnp.where(used, row0, 0)
    nreal = jnp.where(used, jnp.clip(counts[be] - within * MOE_ROWS, 0, MOE_ROWS), 0)
    i32 = lambda a: a.astype(jnp.int32)
    return i32(pad_start), be, i32(row0), i32(nsub), i32(nreal)


def kernel(x, p, w_in, conv_w, sg_ln_g, sg_ln_b, sg_w, sg_b, w_out, ln1_g, ln1_b, w_rg, b_rg, w_re, b_re,
           w_gate, w_up, w_down, ln2_g, ln2_b, w_pg, b_pg, w_pe):
    depth = w_in.shape[0]
    bsz, seq, d = x.shape
    t = bsz * seq
    alpha = (2 * depth) ** 0.25
    n_groups = w_rg.shape[-1]
    ne = w_re.shape[-1]
    per_group = ne // n_groups
    n_asg = t * TOP_K
    n_rows = ((n_asg + ne * (MOE_SUB - 1) + MOE_SUB - 1) // MOE_SUB) * MOE_SUB

    h = x.reshape(t, d)
    for i in range(depth):
        cdim = conv_w.shape[-1]
        nj = cdim // MIX_TN
        w_in_p = _regroup_cols(w_in[i], 5, nj, MIX_TN)
        y, w_out_p, w_pg_b = _mixer(h, w_in_p, conv_w[i], sg_ln_g[i].reshape(1, -1), sg_ln_b[i].reshape(1, -1),
                                    sg_w[i], sg_b[i][:, :, None], w_out[i], w_pg[i], seq)
        wr = jnp.concatenate([w_rg[i], w_re[i], jnp.zeros((d, LANE - n_groups - ne), F32)], axis=1).astype(BF16)
        br = jnp.concatenate([b_rg[i], b_re[i], jnp.zeros((LANE - n_groups - ne,), F32)]).reshape(1, LANE)
        h1, logits = _outproj(y, w_out_p, h, ln1_g[i].reshape(1, d), ln1_b[i].reshape(1, d), wr, br, alpha)
        pk, cnt = _route(logits, n_groups, per_group)
        pad_start, be, row0, nsub, nreal = _block_table(cnt[0, :ne].astype(jnp.int32), n_asg)
        eid = jnp.concatenate([pk[:, 0], pk[:, 1]]).astype(jnp.int32)
        rank = jnp.concatenate([pk[:, 4], pk[:, 5]]).astype(jnp.int32)
        asg = _rowmap(eid, rank, pad_start, n_rows)
        y2 = _moe(be, row0, nsub, nreal, asg, h1, w_gate[i], w_up[i], w_down[i], n_asg)
        h = _combine_gate(h1, y2, pk, ln2_g[i].reshape(1, d), ln2_b[i].reshape(1, d), w_pg_b,
                          b_pg[i].reshape(1, d), p[i].reshape(t, -1), w_pe[i].astype(BF16), alpha)
    return h.reshape(bsz, seq, d)
```

```python
import functools

import jax
import jax.numpy as jnp
from jax import lax
from jax.experimental import pallas as pl
from jax.experimental.pallas import tpu as pltpu

F32 = jnp.float32
BF16 = jnp.bfloat16

LANE = 128
SUBLANE = 8
MXU_COL = 256
VMEM_LIMIT = 56 * 1024 * 1024
MOE_VMEM_LIMIT = 60 * 1024 * 1024

CHUNK = 64
SG_BLOCK = 128
LN_EPS = 1e-5
TOP_K = 2

MIX_TM = 512
MIX_TN = MXU_COL
OUT_TN = 512
ROUTE_TM = 512
MOE_SUB = 128
MOE_ROWS = 768
MOE_TF = MXU_COL
ISSUE_UNROLL = 8
CAST_TK = 4096
CAST_ROWS = 16

_dot = functools.partial(jnp.dot, preferred_element_type=F32)


def _params(*sem, vmem=VMEM_LIMIT):
    return pltpu.CompilerParams(dimension_semantics=sem, vmem_limit_bytes=vmem)


def _layer_norm(z, g, b):
    mu = jnp.mean(z, axis=-1, keepdims=True)
    zc = z - mu
    var = jnp.mean(zc * zc, axis=-1, keepdims=True)
    return zc * lax.rsqrt(var + LN_EPS) * g + b


def _cast_kernel(w_ref, o_ref):
    o_ref[...] = w_ref[...].astype(o_ref.dtype)


def _regroup_cols(w, groups, nj, tn):
    d = w.shape[0]
    tk = min(CAST_TK, d)
    assert d % tk == 0 and w.shape[1] == groups * nj * tn
    return pl.pallas_call(
        _cast_kernel,
        grid=(nj, groups, d // tk),
        in_specs=[pl.BlockSpec((tk, tn), lambda j, s, k: (k, s * nj + j))],
        out_specs=pl.BlockSpec((tk, tn), lambda j, s, k: (k, j * groups + s)),
        out_shape=jax.ShapeDtypeStruct(w.shape, BF16),
        compiler_params=_params("arbitrary", "arbitrary", "arbitrary"),
        name="regroup_cols",
    )(w)


def _mixer_kernel(x_ref, w_ref, cw_ref, lng_ref, lnb_ref, sgw_ref, sgb_ref, wo_ref, wg_ref,
                  y_ref, wo_out_ref, wg_out_ref,
                  xb_ref, pa_ref, pb_ref, gbuf_ref, carry_ref, *, tm, tn, nj, n_steps, tiles_per_seq):
    n = pl.program_id(0)

    @pl.when(n == 0)
    def _():
        pb_ref[...] = jnp.zeros(pb_ref.shape, F32)
        carry_ref[...] = jnp.zeros(carry_ref.shape, F32)

    @pl.when((n % nj == 0) & (n < n_steps))
    def _():
        xb_ref[...] = x_ref[...].astype(BF16)

    m = jnp.maximum(n - 1, 0)
    jp = m % nj
    first = ((m // nj) % tiles_per_seq) == 0

    def mix(proj_ref):
        b, c, h, u, v = (proj_ref[:, k * tn:(k + 1) * tn] for k in range(5))
        g = c * h
        gbuf_ref[0:SUBLANE, :] = jnp.where(first, 0.0, carry_ref[jp])
        gbuf_ref[SUBLANE:, :] = g
        carry_ref[jp] = g[tm - SUBLANE:, :]
        g1 = gbuf_ref[pl.ds(SUBLANE - 1, tm), :]
        g2 = gbuf_ref[pl.ds(SUBLANE - 2, tm), :]
        cw = cw_ref[...]
        conv = cw[0:1, :] * g2 + cw[1:2, :] * g1 + cw[2:3, :] * g
        y_ref[:, 0:tn] = (b * conv).astype(y_ref.dtype)

        pos_i = lax.broadcasted_iota(jnp.int32, (SG_BLOCK, SG_BLOCK), 0)
        pos_j = lax.broadcasted_iota(jnp.int32, (SG_BLOCK, SG_BLOCK), 1)
        mask = (pos_j // CHUNK) <= (pos_i // CHUNK)
        for hh in range(tn // LANE):
            sl = slice(hh * LANE, (hh + 1) * LANE)
            vn = _layer_norm(jax.nn.gelu(v[:, sl]), lng_ref[:, sl], lnb_ref[:, sl]).astype(BF16)
            gu = jax.nn.gelu(u[:, sl])
            ws = jnp.where(mask, sgw_ref[hh], 0.0).astype(BF16)
            bcol = sgb_ref[hh]
            nblk = tm // SG_BLOCK
            vcat = jnp.concatenate([vn[r * SG_BLOCK:(r + 1) * SG_BLOCK, :] for r in range(nblk)], axis=1)
            sg = _dot(ws, vcat) + bcol
            for r in range(nblk):
                rows = slice(r * SG_BLOCK, (r + 1) * SG_BLOCK)
                y_ref[rows, tn + hh * LANE:tn + (hh + 1) * LANE] = (
                    gu[rows, :] * sg[:, r * LANE:(r + 1) * LANE]).astype(y_ref.dtype)

    def step(store_ref, load_ref):
        store_ref[...] = _dot(xb_ref[...], w_ref[...])
        mix(load_ref)
        wo_out_ref[...] = wo_ref[...].astype(BF16)
        wg_out_ref[...] = wg_ref[...].astype(BF16)

    pl.when(n % 2 == 0)(functools.partial(step, pa_ref, pb_ref))
    pl.when(n % 2 == 1)(functools.partial(step, pb_ref, pa_ref))


def _mixer(x2, w_in_p, conv_w, lng, lnb, sg_w, sg_bcol, w_out, w_pg, seq):
    t, d = x2.shape
    cdim = conv_w.shape[1]
    tm, tn = MIX_TM, MIX_TN
    nj = cdim // tn
    assert seq % tm == 0 and t % tm == 0 and cdim % tn == 0 and w_in_p.shape[1] == 5 * cdim
    hp = tn // LANE
    n_steps = (t // tm) * nj
    cur = lambda n: jnp.minimum(n, n_steps - 1)
    prv = lambda n: jnp.maximum(n - 1, 0)
    wrows = w_out.shape[0]
    cb = max(CAST_ROWS, -(-wrows // n_steps // CAST_ROWS) * CAST_ROWS)
    nblk = wrows // cb
    spb = n_steps // nblk
    bpt = tn // cb
    assert w_pg.shape == w_out.shape == (2 * cdim, d) and wrows % cb == 0 and n_steps % nblk == 0 and tn % cb == 0
    blk = lambda n: jnp.minimum(n // spb, nblk - 1)

    def wo_src(n):
        k = blk(n)
        tile, sub = k // bpt, k % bpt
        return ((tile % 2) * nj + tile // 2) * bpt + sub, 0

    return pl.pallas_call(
        functools.partial(_mixer_kernel, tm=tm, tn=tn, nj=nj, n_steps=n_steps, tiles_per_seq=seq // tm),
        grid=(n_steps + 1,),
        in_specs=[
            pl.BlockSpec((tm, d), lambda n: (cur(n) // nj, 0)),
            pl.BlockSpec((d, 5 * tn), lambda n: (0, cur(n) % nj)),
            pl.BlockSpec((conv_w.shape[0], tn), lambda n: (0, prv(n) % nj)),
            pl.BlockSpec((1, tn), lambda n: (0, prv(n) % nj)),
            pl.BlockSpec((1, tn), lambda n: (0, prv(n) % nj)),
            pl.BlockSpec((hp, SG_BLOCK, SG_BLOCK), lambda n: (prv(n) % nj, 0, 0)),
            pl.BlockSpec((hp, SG_BLOCK, 1), lambda n: (prv(n) % nj, 0, 0)),
            pl.BlockSpec((cb, d), wo_src),
            pl.BlockSpec((cb, d), lambda n: (blk(n), 0)),
        ],
        out_specs=[pl.BlockSpec((tm, 2 * tn), lambda n: (prv(n) // nj, prv(n) % nj)),
                   pl.BlockSpec((cb, d), lambda n: (blk(n), 0)),
                   pl.BlockSpec((cb, d), lambda n: (blk(n), 0))],
        out_shape=[jax.ShapeDtypeStruct((t, 2 * cdim), BF16),
                   jax.ShapeDtypeStruct(w_out.shape, BF16),
                   jax.ShapeDtypeStruct(w_pg.shape, BF16)],
        scratch_shapes=[pltpu.VMEM((tm, d), BF16),
                        pltpu.VMEM((tm, 5 * tn), F32),
                        pltpu.VMEM((tm, 5 * tn), F32),
                        pltpu.VMEM((tm + SUBLANE, tn), F32),
                        pltpu.VMEM((nj, SUBLANE, tn), F32)],
        compiler_params=_params("arbitrary"),
        name="mixer",
    )(x2, w_in_p, conv_w, lng, lnb, sg_w, sg_bcol, w_out, w_pg)


def _outproj_kernel(y_ref, w_ref, x_ref, g_ref, b_ref, wr_ref, br_ref, h1_ref, lg_ref, z_ref, *, tn, nj, alpha):
    j = pl.program_id(1)
    z_ref[j] = alpha * x_ref[...] + _dot(y_ref[...], w_ref[...])

    @pl.when(j == nj - 1)
    def _():
        d = nj * tn
        mu = sum(jnp.sum(z_ref[jj], axis=-1, keepdims=True) for jj in range(nj)) / d
        var = sum(jnp.sum(jnp.square(z_ref[jj] - mu), axis=-1, keepdims=True) for jj in range(nj)) / d
        rstd = lax.rsqrt(var + LN_EPS)
        logits = br_ref[...]
        for jj in range(nj):
            cols = slice(jj * tn, (jj + 1) * tn)
            h = (z_ref[jj] - mu) * rstd * g_ref[:, cols] + b_ref[:, cols]
            h1_ref[:, cols] = h
            logits = logits + _dot(h.astype(BF16), wr_ref[cols, :])
        lg_ref[...] = logits


def _outproj(y, w_out_p, x2, g, b, wr, br, alpha):
    t, d = x2.shape
    kdim = y.shape[1]
    tm, tn = MIX_TM, OUT_TN
    nj = d // tn
    assert t % tm == 0 and d % tn == 0 and w_out_p.shape[0] == kdim
    return pl.pallas_call(
        functools.partial(_outproj_kernel, tn=tn, nj=nj, alpha=alpha),
        grid=(t // tm, nj),
        in_specs=[
            pl.BlockSpec((tm, kdim), lambda i, j: (i, 0)),
            pl.BlockSpec((kdim, tn), lambda i, j: (0, j)),
            pl.BlockSpec((tm, tn), lambda i, j: (i, j)),
            pl.BlockSpec((1, d), lambda i, j: (0, 0)),
            pl.BlockSpec((1, d), lambda i, j: (0, 0)),
            pl.BlockSpec((d, LANE), lambda i, j: (0, 0)),
            pl.BlockSpec((1, LANE), lambda i, j: (0, 0)),
        ],
        out_specs=[pl.BlockSpec((tm, d), lambda i, j: (i, 0)),
                   pl.BlockSpec((tm, LANE), lambda i, j: (i, 0))],
        out_shape=[jax.ShapeDtypeStruct((t, d), F32), jax.ShapeDtypeStruct((t, LANE), F32)],
        scratch_shapes=[pltpu.VMEM((nj, tm, tn), F32)],
        compiler_params=_params("arbitrary", "arbitrary"),
        name="outproj",
    )(y, w_out_p, x2, g, b, wr, br)


def _route_kernel(lg_ref, pk_ref, cnt_ref, run_ref, *, n_groups, per_group):
    i = pl.program_id(0)

    @pl.when(i == 0)
    def _():
        run_ref[...] = jnp.zeros_like(run_ref)

    l = lg_ref[...]
    tm = l.shape[0]
    lane = lax.broadcasted_iota(jnp.int32, l.shape, 1)
    neg = jnp.float32(-jnp.inf)
    gmask = lane < n_groups
    gl = jnp.where(gmask, l, neg)
    gmax = jnp.max(gl, axis=-1, keepdims=True)
    gi = jnp.min(jnp.where(gl == gmax, lane, LANE), axis=-1, keepdims=True)
    gp = 1.0 / jnp.sum(jnp.where(gmask, jnp.exp(l - gmax), 0.0), axis=-1, keepdims=True)
    elane = lane - n_groups
    emask = (elane >= 0) & (elane // per_group == gi)
    el = jnp.where(emask, l, neg)
    m1 = jnp.max(el, axis=-1, keepdims=True)
    i1 = jnp.min(jnp.where(el == m1, lane, LANE), axis=-1, keepdims=True)
    el2 = jnp.where(lane == i1, neg, el)
    m2 = jnp.max(el2, axis=-1, keepdims=True)
    i2 = jnp.min(jnp.where(el2 == m2, lane, LANE), axis=-1, keepdims=True)
    t2 = jnp.exp(m2 - m1)
    w0 = gp / (1.0 + t2)
    w1 = gp * t2 / (1.0 + t2)
    e0 = i1 - n_groups
    e1 = i2 - n_groups
    oh0 = lane == e0
    oh1 = lane == e1
    oh = (oh0 | oh1).astype(BF16)
    row = lax.broadcasted_iota(jnp.int32, (tm, tm), 0)
    col = lax.broadcasted_iota(jnp.int32, (tm, tm), 1)
    tri = (row > col).astype(BF16)
    before = run_ref[...] + _dot(tri, oh)
    r0 = jnp.sum(jnp.where(oh0, before, 0.0), axis=-1, keepdims=True)
    r1 = jnp.sum(jnp.where(oh1, before, 0.0), axis=-1, keepdims=True)
    total = run_ref[...] + jnp.sum(oh.astype(F32), axis=0, keepdims=True)
    run_ref[...] = total
    cnt_ref[...] = total
    vals = (e0.astype(F32), e1.astype(F32), w0, w1, r0, r1)
    pk = jnp.zeros(l.shape, F32)
    for k, val in enumerate(vals):
        pk = jnp.where(lane == k, val, pk)
    pk_ref[...] = pk


def _route(logits, n_groups, per_group):
    t = logits.shape[0]
    tm = ROUTE_TM
    assert t % tm == 0 and n_groups * (1 + per_group) <= LANE
    return pl.pallas_call(
        functools.partial(_route_kernel, n_groups=n_groups, per_group=per_group),
        grid=(t // tm,),
        in_specs=[pl.BlockSpec((tm, LANE), lambda i: (i, 0))],
        out_specs=[pl.BlockSpec((tm, LANE), lambda i: (i, 0)),
                   pl.BlockSpec((1, LANE), lambda i: (0, 0))],
        out_shape=[jax.ShapeDtypeStruct((t, LANE), F32), jax.ShapeDtypeStruct((1, LANE), F32)],
        scratch_shapes=[pltpu.VMEM((1, LANE), F32)],
        compiler_params=_params("arbitrary"),
        name="route",
    )(logits)


def _rowmap_kernel(dest_ref, asg_ref, *, n_rows, n_asg):
    def fill(r, c):
        asg_ref[r] = 0
        return c

    lax.fori_loop(0, n_rows, fill, 0, unroll=ISSUE_UNROLL)

    def place(a, c):
        asg_ref[dest_ref[a]] = a
        return c

    lax.fori_loop(0, n_asg, place, 0, unroll=ISSUE_UNROLL)


def _rowmap(dest, n_rows):
    n_asg = dest.shape[0]
    smem = pl.BlockSpec(memory_space=pltpu.SMEM)
    return pl.pallas_call(
        functools.partial(_rowmap_kernel, n_rows=n_rows, n_asg=n_asg),
        in_specs=[smem],
        out_specs=smem,
        out_shape=jax.ShapeDtypeStruct((n_rows,), jnp.int32),
        name="rowmap",
    )(dest)


def _moe_kernel(be_ref, row0_ref, nsub_ref, nreal_ref, asg_ref, h1_hbm, wg_ref, wu_ref, wd_ref, y_hbm,
                xg_ref, xb_ref, yacc_ref, gsem, ssem, *, nb, nf, sub, tok_mask):
    b = pl.program_id(0)
    f = pl.program_id(1)
    nsub = nsub_ref[b]
    max_tiles = xb_ref.shape[0] // sub

    def gather_row(base, r):
        tok = asg_ref[base + r] & tok_mask
        pltpu.make_async_copy(h1_hbm.at[pl.ds(tok, 1)], xg_ref.at[pl.ds(r, 1)], gsem).start()

    def scatter_row(base, r):
        dst = asg_ref[base + r]
        pltpu.make_async_copy(yacc_ref.at[pl.ds(r, 1)], y_hbm.at[pl.ds(dst, 1)], ssem).start()

    def issue_tiles(row_fn, base, n_tiles):
        for s in range(max_tiles):
            @pl.when(s < n_tiles)
            def _(s=s):
                for r in range(s * sub, (s + 1) * sub):
                    row_fn(base, r)

    def issue_range(row_fn, base, lo, hi):
        groups = (hi - lo) // ISSUE_UNROLL

        def body(q, c):
            for k in range(ISSUE_UNROLL):
                row_fn(base, lo + q * ISSUE_UNROLL + k)
            return c

        lax.fori_loop(0, groups, body, 0)

        def tail(r, c):
            row_fn(base, r)
            return c

        lax.fori_loop(lo + groups * ISSUE_UNROLL, hi, tail, 0)

    def wait_rows(n, tile_copy, row_copy):
        def tile(s, c):
            tile_copy.wait()
            return c

        lax.fori_loop(0, n // sub, tile, 0)

        def row(r, c):
            row_copy.wait()
            return c

        lax.fori_loop((n // sub) * sub, n, row, 0)

    def gather_wait(n):
        wait_rows(n, pltpu.make_async_copy(h1_hbm.at[pl.ds(0, sub)], xg_ref.at[pl.ds(0, sub)], gsem),
                  pltpu.make_async_copy(h1_hbm.at[pl.ds(0, 1)], xg_ref.at[pl.ds(0, 1)], gsem))

    def scatter_wait(blk):
        wait_rows(nreal_ref[blk], pltpu.make_async_copy(yacc_ref.at[pl.ds(0, sub)], y_hbm.at[pl.ds(0, sub)], ssem),
                  pltpu.make_async_copy(yacc_ref.at[pl.ds(0, 1)], y_hbm.at[pl.ds(0, 1)], ssem))

    @pl.when((f == 0) & (b == 0))
    def _():
        yacc_ref[...] = jnp.zeros(yacc_ref.shape, F32)

    @pl.when((f == 0) & (b > 0) & (nsub == 0))
    def _():
        scatter_wait(jnp.maximum(b - 1, 0))

    @pl.when(nsub > 0)
    def _():
        @pl.when(f == 0)
        def _():
            @pl.when(b == 0)
            def _():
                issue_range(gather_row, row0_ref[0], 0, nsub * sub)

            gather_wait(nsub * sub)

            def cast(s, c):
                rows = pl.ds(pl.multiple_of(s * sub, sub), sub)
                xb_ref[rows, :] = xg_ref[rows, :].astype(BF16)
                return c

            lax.fori_loop(0, nsub, cast, 0)

            @pl.when(b + 1 < nb)
            def _():
                nxt = jnp.minimum(b + 1, nb - 1)
                issue_tiles(gather_row, row0_ref[nxt], nsub_ref[nxt])

        def compute(m):
            rows = slice(0, m * sub)
            xs = xb_ref[rows, :]
            hb = (jax.nn.silu(_dot(xs, wg_ref[...])) * _dot(xs, wu_ref[...])).astype(BF16)

            @pl.when((f == 0) & (b > 0))
            def _():
                scatter_wait(jnp.maximum(b - 1, 0))

            yp = _dot(hb, wd_ref[...])
            yacc_ref[rows, :] = jnp.where(f == 0, yp, yacc_ref[rows, :] + yp)

        for m in range(1, xb_ref.shape[0] // sub + 1):
            pl.when(nsub == m)(functools.partial(compute, m))

        @pl.when(f == nf - 1)
        def _():
            full = nreal_ref[b] // sub
            issue_tiles(scatter_row, row0_ref[b], full)
            issue_range(scatter_row, row0_ref[b], full * sub, nreal_ref[b])

            @pl.when(b == nb - 1)
            def _():
                scatter_wait(b)


def _moe(be, row0, nsub, nreal, asg, h1, w_gate, w_up, w_down, n_out_rows):
    t, d = h1.shape
    ne, _, de = w_gate.shape
    nb = be.shape[0]
    nf = de // MOE_TF
    assert de % MOE_TF == 0 and t & (t - 1) == 0
    last = nf - 1
    fsel = lambda f, n: jnp.where(n > 0, f, last)
    grid_spec = pltpu.PrefetchScalarGridSpec(
        num_scalar_prefetch=5,
        grid=(nb, nf),
        in_specs=[
            pl.BlockSpec(memory_space=pl.ANY),
            pl.BlockSpec((None, d, MOE_TF), lambda b, f, be, r0, ns, nr, asg: (be[b], 0, fsel(f, ns[b]))),
            pl.BlockSpec((None, d, MOE_TF), lambda b, f, be, r0, ns, nr, asg: (be[b], 0, fsel(f, ns[b]))),
            pl.BlockSpec((None, MOE_TF, d), lambda b, f, be, r0, ns, nr, asg: (be[b], fsel(f, ns[b]), 0)),
        ],
        out_specs=pl.BlockSpec(memory_space=pl.ANY),
        scratch_shapes=[pltpu.VMEM((MOE_ROWS, d), F32),
                        pltpu.VMEM((MOE_ROWS, d), BF16),
                        pltpu.VMEM((MOE_ROWS, d), F32),
                        pltpu.SemaphoreType.DMA,
                        pltpu.SemaphoreType.DMA],
    )
    return pl.pallas_call(
        functools.partial(_moe_kernel, nb=nb, nf=nf, sub=MOE_SUB, tok_mask=t - 1),
        grid_spec=grid_spec,
        out_shape=jax.ShapeDtypeStruct((n_out_rows, d), F32),
        compiler_params=_params("arbitrary", "arbitrary", vmem=MOE_VMEM_LIMIT),
        name="moe",
    )(be, row0, nsub, nreal, asg, h1, w_gate, w_up, w_down)


def _combine_gate_kernel(h1_ref, y0_ref, y1_ref, pk_ref, g_ref, b_ref, wpg_ref, bpg_ref, p_ref, wpe_ref,
                         o_ref, za_ref, zb_ref, ha_ref, hb_ref, *, nj, tn, alpha):
    n = pl.program_id(0)
    j = n % nj
    parity = (n // nj) % 2

    @pl.when(n == 0)
    def _():
        for ref in (za_ref, zb_ref, ha_ref, hb_ref):
            ref[...] = jnp.zeros(ref.shape, ref.dtype)

    def layer_norm_tile(z_ref, h_ref):
        d = nj * tn
        mu = sum(jnp.sum(z_ref[jj], axis=-1, keepdims=True) for jj in range(nj)) / d
        var = sum(jnp.sum(jnp.square(z_ref[jj] - mu), axis=-1, keepdims=True) for jj in range(nj)) / d
        rstd = lax.rsqrt(var + LN_EPS)
        for jj in range(nj):
            cols = slice(jj * tn, (jj + 1) * tn)
            h = (z_ref[jj] - mu) * rstd * g_ref[:, cols] + b_ref[:, cols]
            z_ref[jj] = h
            h_ref[:, cols] = h.astype(BF16)

    def step(zw_ref, hw_ref, zo_ref, ho_ref, first):
        pk = pk_ref[...]
        ffn = pk[:, 2:3] * y0_ref[...] + pk[:, 3:4] * y1_ref[...]
        z = alpha * h1_ref[...] + ffn
        if first:
            zw_ref[0] = z
            layer_norm_tile(zo_ref, ho_ref)
            zg, hg = zw_ref[nj - 1], hw_ref[...]
        else:
            zw_ref[j] = z
            zg, hg = zo_ref[j - 1], ho_ref[...]
        gate = jax.nn.sigmoid(_dot(hg, wpg_ref[...]) + bpg_ref[...])
        ple = _dot(p_ref[...].astype(BF16), wpe_ref[...])
        o_ref[...] = zg + gate * ple

    slots = ((za_ref, ha_ref, zb_ref, hb_ref), (zb_ref, hb_ref, za_ref, ha_ref))
    for par in range(2):
        for first in (True, False):
            cond = (parity == par) & ((j == 0) if first else (j > 0))
            pl.when(cond)(functools.partial(step, *slots[par], first))


def _combine_gate(h1, y2, pk, g, b, w_pg_b, b_pg, p2, w_pe_b, alpha):
    t, d = h1.shape
    pd = p2.shape[1]
    tm, tn = MIX_TM, OUT_TN
    assert t % tm == 0 and d % tn == 0
    ni, nj = t // tm, d // tn
    n_comb = ni * nj
    cmb = lambda n: jnp.minimum(n, n_comb - 1)
    gat = lambda n: jnp.clip(n - nj - 1, 0, n_comb - 1)
    return pl.pallas_call(
        functools.partial(_combine_gate_kernel, nj=nj, tn=tn, alpha=alpha),
        grid=(n_comb + nj + 1,),
        in_specs=[
            pl.BlockSpec((tm, tn), lambda n: (cmb(n) // nj, cmb(n) % nj)),
            pl.BlockSpec((tm, tn), lambda n: (cmb(n) // nj, cmb(n) % nj)),
            pl.BlockSpec((tm, tn), lambda n: (cmb(n) // nj + ni, cmb(n) % nj)),
            pl.BlockSpec((tm, LANE), lambda n: (cmb(n) // nj, 0)),
            pl.BlockSpec((1, d), lambda n: (0, 0)),
            pl.BlockSpec((1, d), lambda n: (0, 0)),
            pl.BlockSpec((d, tn), lambda n: (0, gat(n) % nj)),
            pl.BlockSpec((1, tn), lambda n: (0, gat(n) % nj)),
            pl.BlockSpec((tm, pd), lambda n: (gat(n) // nj, 0)),
            pl.BlockSpec((pd, tn), lambda n: (0, gat(n) % nj)),
        ],
        out_specs=pl.BlockSpec((tm, tn), lambda n: (gat(n) // nj, gat(n) % nj)),
        out_shape=jax.ShapeDtypeStruct((t, d), F32),
        scratch_shapes=[pltpu.VMEM((nj, tm, tn), F32), pltpu.VMEM((nj, tm, tn), F32),
                        pltpu.VMEM((tm, d), BF16), pltpu.VMEM((tm, d), BF16)],
        compiler_params=_params("arbitrary"),
        name="combine_gate",
    )(h1, y2, y2, pk, g, b, w_pg_b, b_pg, p2, w_pe_b)


def _block_table(counts, n_asg):
    ne = counts.shape[0]
    padded = ((counts + MOE_SUB - 1) // MOE_SUB) * MOE_SUB
    pad_end = jnp.cumsum(padded)
    pad_start = pad_end - padded
    nblk = (padded + MOE_ROWS - 1) // MOE_ROWS
    blk_end = jnp.cumsum(nblk)
    blk_start = blk_end - nblk
    nb = ne + -(-n_asg // MOE_ROWS)
    bidx = jnp.arange(nb, dtype=jnp.int32)
    used = bidx < blk_end[-1]
    be = jnp.minimum(jnp.searchsorted(blk_end, jnp.minimum(bidx, blk_end[-1] - 1), side="right"), ne - 1)
    be = be.astype(jnp.int32)
    within = bidx - blk_start[be]
    row0 = pad_start[be] + within * MOE_ROWS
    nsub = jnp.clip((padded[be] - within * MOE_ROWS) // MOE_SUB, 0, MOE_ROWS // MOE_SUB)
    nsub = jnp.where(used, nsub, 0)
    row0 = jnp.where(used, row0, 0)
    nreal = jnp.where(used, jnp.clip(counts[be] - within * MOE_ROWS, 0, MOE_ROWS), 0)
    i32 = lambda a: a.astype(jnp.int32)
    return i32(pad_start), be, i32(row0), i32(nsub), i32(nreal)


def kernel(x, p, w_in, conv_w, sg_ln_g, sg_ln_b, sg_w, sg_b, w_out, ln1_g, ln1_b, w_rg, b_rg, w_re, b_re,
           w_gate, w_up, w_down, ln2_g, ln2_b, w_pg, b_pg, w_pe):
    depth = w_in.shape[0]
    bsz, seq, d = x.shape
    t = bsz * seq
    alpha = (2 * depth) ** 0.25
    n_groups = w_rg.shape[-1]
    ne = w_re.shape[-1]
    per_group = ne // n_groups
    n_asg = t * TOP_K
    n_rows = ((n_asg + ne * (MOE_SUB - 1) + MOE_SUB - 1) // MOE_SUB) * MOE_SUB

    h = x.reshape(t, d)
    for i in range(depth):
        cdim = conv_w.shape[-1]
        nj = cdim // MIX_TN
        w_in_p = _regroup_cols(w_in[i], 5, nj, MIX_TN)
        y, w_out_p, w_pg_b = _mixer(h, w_in_p, conv_w[i], sg_ln_g[i].reshape(1, -1), sg_ln_b[i].reshape(1, -1),
                                    sg_w[i], sg_b[i][:, :, None], w_out[i], w_pg[i], seq)
        wr = jnp.concatenate([w_rg[i], w_re[i], jnp.zeros((d, LANE - n_groups - ne), F32)], axis=1).astype(BF16)
        br = jnp.concatenate([b_rg[i], b_re[i], jnp.zeros((LANE - n_groups - ne,), F32)]).reshape(1, LANE)
        h1, logits = _outproj(y, w_out_p, h, ln1_g[i].reshape(1, d), ln1_b[i].reshape(1, d), wr, br, alpha)
        pk, cnt = _route(logits, n_groups, per_group)
        pad_start, be, row0, nsub, nreal = _block_table(cnt[0, :ne].astype(jnp.int32), n_asg)
        eid = jnp.concatenate([pk[:, 0], pk[:, 1]]).astype(jnp.int32)
        rank = jnp.concatenate([pk[:, 4], pk[:, 5]]).astype(jnp.int32)
        asg = _rowmap(pad_start[eid] + rank, n_rows)
        y2 = _moe(be, row0, nsub, nreal, asg, h1, w_gate[i], w_up[i], w_down[i], n_asg)
        h = _combine_gate(h1, y2, pk, ln2_g[i].reshape(1, d), ln2_b[i].reshape(1, d), w_pg_b,
                          b_pg[i].reshape(1, d), p[i].reshape(t, -1), w_pe[i].astype(BF16), alpha)
    return h.reshape(bsz, seq, d)
```

```python
import functools

import jax
import jax.numpy as jnp
from jax import lax
from jax.experimental import pallas as pl
from jax.experimental.pallas import tpu as pltpu
from jax.experimental.pallas import tpu_sc as plsc

F32 = jnp.float32
BF16 = jnp.bfloat16

LANE = 128
SUBLANE = 8
MXU_COL = 256
SC_CORES = 2
SC_SUBCORES = 16
SC_LANES = 16
VMEM_LIMIT = 56 * 1024 * 1024
MOE_VMEM_LIMIT = 60 * 1024 * 1024

CHUNK = 64
SG_BLOCK = 128
LN_EPS = 1e-5
TOP_K = 2

MIX_TM = 512
MIX_TN = MXU_COL
OUT_TN = 512
ROUTE_TM = 512
MOE_SUB = 128
MOE_ROWS = 768
MOE_TF = MXU_COL
ISSUE_UNROLL = 8
ROWMAP_CHUNK = 4096
CAST_TK = 4096
CAST_ROWS = 16

_dot = functools.partial(jnp.dot, preferred_element_type=F32)


def _params(*sem, vmem=VMEM_LIMIT):
    return pltpu.CompilerParams(dimension_semantics=sem, vmem_limit_bytes=vmem)


def _layer_norm(z, g, b):
    mu = jnp.mean(z, axis=-1, keepdims=True)
    zc = z - mu
    var = jnp.mean(zc * zc, axis=-1, keepdims=True)
    return zc * lax.rsqrt(var + LN_EPS) * g + b


def _cast_kernel(w_ref, o_ref):
    o_ref[...] = w_ref[...].astype(o_ref.dtype)


def _regroup_cols(w, groups, nj, tn):
    d = w.shape[0]
    tk = min(CAST_TK, d)
    assert d % tk == 0 and w.shape[1] == groups * nj * tn
    return pl.pallas_call(
        _cast_kernel,
        grid=(nj, groups, d // tk),
        in_specs=[pl.BlockSpec((tk, tn), lambda j, s, k: (k, s * nj + j))],
        out_specs=pl.BlockSpec((tk, tn), lambda j, s, k: (k, j * groups + s)),
        out_shape=jax.ShapeDtypeStruct(w.shape, BF16),
        compiler_params=_params("arbitrary", "arbitrary", "arbitrary"),
        name="regroup_cols",
    )(w)


def _mixer_kernel(x_ref, w_ref, cw_ref, lng_ref, lnb_ref, sgw_ref, sgb_ref, wo_ref, wg_ref,
                  y_ref, wo_out_ref, wg_out_ref,
                  xb_ref, pa_ref, pb_ref, gbuf_ref, carry_ref, *, tm, tn, nj, n_steps, tiles_per_seq):
    n = pl.program_id(0)

    @pl.when(n == 0)
    def _():
        pb_ref[...] = jnp.zeros(pb_ref.shape, F32)
        carry_ref[...] = jnp.zeros(carry_ref.shape, F32)

    @pl.when((n % nj == 0) & (n < n_steps))
    def _():
        xb_ref[...] = x_ref[...].astype(BF16)

    m = jnp.maximum(n - 1, 0)
    jp = m % nj
    first = ((m // nj) % tiles_per_seq) == 0

    def mix(proj_ref):
        b, c, h, u, v = (proj_ref[:, k * tn:(k + 1) * tn] for k in range(5))
        g = c * h
        gbuf_ref[0:SUBLANE, :] = jnp.where(first, 0.0, carry_ref[jp])
        gbuf_ref[SUBLANE:, :] = g
        carry_ref[jp] = g[tm - SUBLANE:, :]
        g1 = gbuf_ref[pl.ds(SUBLANE - 1, tm), :]
        g2 = gbuf_ref[pl.ds(SUBLANE - 2, tm), :]
        cw = cw_ref[...]
        conv = cw[0:1, :] * g2 + cw[1:2, :] * g1 + cw[2:3, :] * g
        y_ref[:, 0:tn] = (b * conv).astype(y_ref.dtype)

        pos_i = lax.broadcasted_iota(jnp.int32, (SG_BLOCK, SG_BLOCK), 0)
        pos_j = lax.broadcasted_iota(jnp.int32, (SG_BLOCK, SG_BLOCK), 1)
        mask = (pos_j // CHUNK) <= (pos_i // CHUNK)
        for hh in range(tn // LANE):
            sl = slice(hh * LANE, (hh + 1) * LANE)
            vn = _layer_norm(jax.nn.gelu(v[:, sl]), lng_ref[:, sl], lnb_ref[:, sl]).astype(BF16)
            gu = jax.nn.gelu(u[:, sl])
            ws = jnp.where(mask, sgw_ref[hh], 0.0).astype(BF16)
            bcol = sgb_ref[hh]
            nblk = tm // SG_BLOCK
            vcat = jnp.concatenate([vn[r * SG_BLOCK:(r + 1) * SG_BLOCK, :] for r in range(nblk)], axis=1)
            sg = _dot(ws, vcat) + bcol
            for r in range(nblk):
                rows = slice(r * SG_BLOCK, (r + 1) * SG_BLOCK)
                y_ref[rows, tn + hh * LANE:tn + (hh + 1) * LANE] = (
                    gu[rows, :] * sg[:, r * LANE:(r + 1) * LANE]).astype(y_ref.dtype)

    def step(store_ref, load_ref):
        store_ref[...] = _dot(xb_ref[...], w_ref[...])
        mix(load_ref)
        wo_out_ref[...] = wo_ref[...].astype(BF16)
        wg_out_ref[...] = wg_ref[...].astype(BF16)

    pl.when(n % 2 == 0)(functools.partial(step, pa_ref, pb_ref))
    pl.when(n % 2 == 1)(functools.partial(step, pb_ref, pa_ref))


def _mixer(x2, w_in_p, conv_w, lng, lnb, sg_w, sg_bcol, w_out, w_pg, seq):
    t, d = x2.shape
    cdim = conv_w.shape[1]
    tm, tn = MIX_TM, MIX_TN
    nj = cdim // tn
    assert seq % tm == 0 and t % tm == 0 and cdim % tn == 0 and w_in_p.shape[1] == 5 * cdim
    hp = tn // LANE
    n_steps = (t // tm) * nj
    cur = lambda n: jnp.minimum(n, n_steps - 1)
    prv = lambda n: jnp.maximum(n - 1, 0)
    wrows = w_out.shape[0]
    cb = max(CAST_ROWS, -(-wrows // n_steps // CAST_ROWS) * CAST_ROWS)
    nblk = wrows // cb
    spb = n_steps // nblk
    bpt = tn // cb
    assert w_pg.shape == w_out.shape == (2 * cdim, d) and wrows % cb == 0 and n_steps % nblk == 0 and tn % cb == 0
    blk = lambda n: jnp.minimum(n // spb, nblk - 1)

    def wo_src(n):
        k = blk(n)
        tile, sub = k // bpt, k % bpt
        return ((tile % 2) * nj + tile // 2) * bpt + sub, 0

    return pl.pallas_call(
        functools.partial(_mixer_kernel, tm=tm, tn=tn, nj=nj, n_steps=n_steps, tiles_per_seq=seq // tm),
        grid=(n_steps + 1,),
        in_specs=[
            pl.BlockSpec((tm, d), lambda n: (cur(n) // nj, 0)),
            pl.BlockSpec((d, 5 * tn), lambda n: (0, cur(n) % nj)),
            pl.BlockSpec((conv_w.shape[0], tn), lambda n: (0, prv(n) % nj)),
            pl.BlockSpec((1, tn), lambda n: (0, prv(n) % nj)),
            pl.BlockSpec((1, tn), lambda n: (0, prv(n) % nj)),
            pl.BlockSpec((hp, SG_BLOCK, SG_BLOCK), lambda n: (prv(n) % nj, 0, 0)),
            pl.BlockSpec((hp, SG_BLOCK, 1), lambda n: (prv(n) % nj, 0, 0)),
            pl.BlockSpec((cb, d), wo_src),
            pl.BlockSpec((cb, d), lambda n: (blk(n), 0)),
        ],
        out_specs=[pl.BlockSpec((tm, 2 * tn), lambda n: (prv(n) // nj, prv(n) % nj)),
                   pl.BlockSpec((cb, d), lambda n: (blk(n), 0)),
                   pl.BlockSpec((cb, d), lambda n: (blk(n), 0))],
        out_shape=[jax.ShapeDtypeStruct((t, 2 * cdim), BF16),
                   jax.ShapeDtypeStruct(w_out.shape, BF16),
                   jax.ShapeDtypeStruct(w_pg.shape, BF16)],
        scratch_shapes=[pltpu.VMEM((tm, d), BF16),
                        pltpu.VMEM((tm, 5 * tn), F32),
                        pltpu.VMEM((tm, 5 * tn), F32),
                        pltpu.VMEM((tm + SUBLANE, tn), F32),
                        pltpu.VMEM((nj, SUBLANE, tn), F32)],
        compiler_params=_params("arbitrary"),
        name="mixer",
    )(x2, w_in_p, conv_w, lng, lnb, sg_w, sg_bcol, w_out, w_pg)


def _outproj_kernel(y_ref, w_ref, x_ref, g_ref, b_ref, wr_ref, br_ref, h1_ref, lg_ref, z_ref, *, tn, nj, alpha):
    j = pl.program_id(1)
    z_ref[j] = alpha * x_ref[...] + _dot(y_ref[...], w_ref[...])

    @pl.when(j == nj - 1)
    def _():
        d = nj * tn
        mu = sum(jnp.sum(z_ref[jj], axis=-1, keepdims=True) for jj in range(nj)) / d
        var = sum(jnp.sum(jnp.square(z_ref[jj] - mu), axis=-1, keepdims=True) for jj in range(nj)) / d
        rstd = lax.rsqrt(var + LN_EPS)
        logits = br_ref[...]
        for jj in range(nj):
            cols = slice(jj * tn, (jj + 1) * tn)
            h = (z_ref[jj] - mu) * rstd * g_ref[:, cols] + b_ref[:, cols]
            h1_ref[:, cols] = h
            logits = logits + _dot(h.astype(BF16), wr_ref[cols, :])
        lg_ref[...] = logits


def _outproj(y, w_out_p, x2, g, b, wr, br, alpha):
    t, d = x2.shape
    kdim = y.shape[1]
    tm, tn = MIX_TM, OUT_TN
    nj = d // tn
    assert t % tm == 0 and d % tn == 0 and w_out_p.shape[0] == kdim
    return pl.pallas_call(
        functools.partial(_outproj_kernel, tn=tn, nj=nj, alpha=alpha),
        grid=(t // tm, nj),
        in_specs=[
            pl.BlockSpec((tm, kdim), lambda i, j: (i, 0)),
            pl.BlockSpec((kdim, tn), lambda i, j: (0, j)),
            pl.BlockSpec((tm, tn), lambda i, j: (i, j)),
            pl.BlockSpec((1, d), lambda i, j: (0, 0)),
            pl.BlockSpec((1, d), lambda i, j: (0, 0)),
            pl.BlockSpec((d, LANE), lambda i, j: (0, 0)),
            pl.BlockSpec((1, LANE), lambda i, j: (0, 0)),
        ],
        out_specs=[pl.BlockSpec((tm, d), lambda i, j: (i, 0)),
                   pl.BlockSpec((tm, LANE), lambda i, j: (i, 0))],
        out_shape=[jax.ShapeDtypeStruct((t, d), F32), jax.ShapeDtypeStruct((t, LANE), F32)],
        scratch_shapes=[pltpu.VMEM((nj, tm, tn), F32)],
        compiler_params=_params("arbitrary", "arbitrary"),
        name="outproj",
    )(y, w_out_p, x2, g, b, wr, br)


def _route_kernel(lg_ref, pk_ref, cnt_ref, run_ref, *, n_groups, per_group):
    i = pl.program_id(0)

    @pl.when(i == 0)
    def _():
        run_ref[...] = jnp.zeros_like(run_ref)

    l = lg_ref[...]
    tm = l.shape[0]
    lane = lax.broadcasted_iota(jnp.int32, l.shape, 1)
    neg = jnp.float32(-jnp.inf)
    gmask = lane < n_groups
    gl = jnp.where(gmask, l, neg)
    gmax = jnp.max(gl, axis=-1, keepdims=True)
    gi = jnp.min(jnp.where(gl == gmax, lane, LANE), axis=-1, keepdims=True)
    gp = 1.0 / jnp.sum(jnp.where(gmask, jnp.exp(l - gmax), 0.0), axis=-1, keepdims=True)
    elane = lane - n_groups
    emask = (elane >= 0) & (elane // per_group == gi)
    el = jnp.where(emask, l, neg)
    m1 = jnp.max(el, axis=-1, keepdims=True)
    i1 = jnp.min(jnp.where(el == m1, lane, LANE), axis=-1, keepdims=True)
    el2 = jnp.where(lane == i1, neg, el)
    m2 = jnp.max(el2, axis=-1, keepdims=True)
    i2 = jnp.min(jnp.where(el2 == m2, lane, LANE), axis=-1, keepdims=True)
    t2 = jnp.exp(m2 - m1)
    w0 = gp / (1.0 + t2)
    w1 = gp * t2 / (1.0 + t2)
    e0 = i1 - n_groups
    e1 = i2 - n_groups
    oh0 = lane == e0
    oh1 = lane == e1
    oh = (oh0 | oh1).astype(BF16)
    row = lax.broadcasted_iota(jnp.int32, (tm, tm), 0)
    col = lax.broadcasted_iota(jnp.int32, (tm, tm), 1)
    tri = (row > col).astype(BF16)
    before = run_ref[...] + _dot(tri, oh)
    r0 = jnp.sum(jnp.where(oh0, before, 0.0), axis=-1, keepdims=True)
    r1 = jnp.sum(jnp.where(oh1, before, 0.0), axis=-1, keepdims=True)
    total = run_ref[...] + jnp.sum(oh.astype(F32), axis=0, keepdims=True)
    run_ref[...] = total
    cnt_ref[...] = total
    vals = (e0.astype(F32), e1.astype(F32), w0, w1, r0, r1)
    pk = jnp.zeros(l.shape, F32)
    for k, val in enumerate(vals):
        pk = jnp.where(lane == k, val, pk)
    pk_ref[...] = pk


def _route(logits, n_groups, per_group):
    t = logits.shape[0]
    tm = ROUTE_TM
    assert t % tm == 0 and n_groups * (1 + per_group) <= LANE
    return pl.pallas_call(
        functools.partial(_route_kernel, n_groups=n_groups, per_group=per_group),
        grid=(t // tm,),
        in_specs=[pl.BlockSpec((tm, LANE), lambda i: (i, 0))],
        out_specs=[pl.BlockSpec((tm, LANE), lambda i: (i, 0)),
                   pl.BlockSpec((1, LANE), lambda i: (0, 0))],
        out_shape=[jax.ShapeDtypeStruct((t, LANE), F32), jax.ShapeDtypeStruct((1, LANE), F32)],
        scratch_shapes=[pltpu.VMEM((1, LANE), F32)],
        compiler_params=_params("arbitrary"),
        name="route",
    )(logits)


def _rowmap_kernel(eid_hbm, rank_hbm, start_hbm, asg_hbm, eid_v, rank_v, start_v, asg_v, *, n_rows, n_asg, chunk):
    on_first = (lax.axis_index("c") == 0) & (lax.axis_index("s") == 0)

    @pl.when(on_first)
    def _():
        pltpu.sync_copy(start_hbm, start_v)
        zeros = jnp.zeros((SC_LANES,), jnp.int32)

        @pl.loop(0, n_rows, step=SC_LANES)
        def _(r):
            asg_v[pl.ds(r, SC_LANES)] = zeros

        lanes = lax.iota(jnp.int32, SC_LANES)

        @pl.loop(0, n_asg, step=chunk)
        def _(a0):
            pltpu.sync_copy(eid_hbm.at[pl.ds(a0, chunk)], eid_v)
            pltpu.sync_copy(rank_hbm.at[pl.ds(a0, chunk)], rank_v)

            @pl.loop(0, chunk, step=SC_LANES)
            def _(k):
                dest = plsc.load_gather(start_v, [eid_v[pl.ds(k, SC_LANES)]]) + rank_v[pl.ds(k, SC_LANES)]
                plsc.store_scatter(asg_v, [dest], lanes + (a0 + k))

        pltpu.sync_copy(asg_v, asg_hbm)


def _rowmap(eid, rank, start, n_rows):
    n_asg = eid.shape[0]
    chunk = min(ROWMAP_CHUNK, n_asg)
    assert n_asg % chunk == 0 and chunk % SC_LANES == 0 and n_rows % SC_LANES == 0 and start.shape[0] % SC_LANES == 0
    mesh = plsc.VectorSubcoreMesh(core_axis_name="c", subcore_axis_name="s",
                                  num_cores=SC_CORES, num_subcores=SC_SUBCORES)
    i32 = jnp.int32
    return pl.kernel(
        functools.partial(_rowmap_kernel, n_rows=n_rows, n_asg=n_asg, chunk=chunk),
        out_type=jax.ShapeDtypeStruct((n_rows,), i32),
        mesh=mesh,
        scratch_types=[pltpu.VMEM((chunk,), i32), pltpu.VMEM((chunk,), i32),
                       pltpu.VMEM(start.shape, i32), pltpu.VMEM((n_rows,), i32)],
        compiler_params=pltpu.CompilerParams(needs_layout_passes=False),
        name="rowmap",
    )(eid, rank, start)


def _moe_kernel(be_ref, row0_ref, nsub_ref, nreal_ref, asg_ref, h1_hbm, wg_ref, wu_ref, wd_ref, y_hbm,
                xg_ref, xb_ref, yacc_ref, gsem, ssem, *, nb, nf, sub, tok_mask):
    b = pl.program_id(0)
    f = pl.program_id(1)
    nsub = nsub_ref[b]
    max_tiles = xb_ref.shape[0] // sub

    def gather_row(base, r):
        tok = asg_ref[base + r] & tok_mask
        pltpu.make_async_copy(h1_hbm.at[pl.ds(tok, 1)], xg_ref.at[pl.ds(r, 1)], gsem).start()

    def scatter_row(base, r):
        dst = asg_ref[base + r]
        pltpu.make_async_copy(yacc_ref.at[pl.ds(r, 1)], y_hbm.at[pl.ds(dst, 1)], ssem).start()

    def issue_tiles(row_fn, base, n_tiles):
        for s in range(max_tiles):
            @pl.when(s < n_tiles)
            def _(s=s):
                for r in range(s * sub, (s + 1) * sub):
                    row_fn(base, r)

    def issue_range(row_fn, base, lo, hi):
        groups = (hi - lo) // ISSUE_UNROLL

        def body(q, c):
            for k in range(ISSUE_UNROLL):
                row_fn(base, lo + q * ISSUE_UNROLL + k)
            return c

        lax.fori_loop(0, groups, body, 0)

        def tail(r, c):
            row_fn(base, r)
            return c

        lax.fori_loop(lo + groups * ISSUE_UNROLL, hi, tail, 0)

    def wait_rows(n, tile_copy, row_copy):
        def tile(s, c):
            tile_copy.wait()
            return c

        lax.fori_loop(0, n // sub, tile, 0)

        def row(r, c):
            row_copy.wait()
            return c

        lax.fori_loop((n // sub) * sub, n, row, 0)

    def gather_wait(n):
        wait_rows(n, pltpu.make_async_copy(h1_hbm.at[pl.ds(0, sub)], xg_ref.at[pl.ds(0, sub)], gsem),
                  pltpu.make_async_copy(h1_hbm.at[pl.ds(0, 1)], xg_ref.at[pl.ds(0, 1)], gsem))

    def scatter_wait(blk):
        wait_rows(nreal_ref[blk], pltpu.make_async_copy(yacc_ref.at[pl.ds(0, sub)], y_hbm.at[pl.ds(0, sub)], ssem),
                  pltpu.make_async_copy(yacc_ref.at[pl.ds(0, 1)], y_hbm.at[pl.ds(0, 1)], ssem))

    @pl.when((f == 0) & (b == 0))
    def _():
        yacc_ref[...] = jnp.zeros(yacc_ref.shape, F32)

    @pl.when((f == 0) & (b > 0) & (nsub == 0))
    def _():
        scatter_wait(jnp.maximum(b - 1, 0))

    @pl.when(nsub > 0)
    def _():
        @pl.when(f == 0)
        def _():
            @pl.when(b == 0)
            def _():
                issue_range(gather_row, row0_ref[0], 0, nsub * sub)

            gather_wait(nsub * sub)

            def cast(s, c):
                rows = pl.ds(pl.multiple_of(s * sub, sub), sub)
                xb_ref[rows, :] = xg_ref[rows, :].astype(BF16)
                return c

            lax.fori_loop(0, nsub, cast, 0)

            @pl.when(b + 1 < nb)
            def _():
                nxt = jnp.minimum(b + 1, nb - 1)
                issue_tiles(gather_row, row0_ref[nxt], nsub_ref[nxt])

        def compute(m):
            rows = slice(0, m * sub)
            xs = xb_ref[rows, :]
            hb = (jax.nn.silu(_dot(xs, wg_ref[...])) * _dot(xs, wu_ref[...])).astype(BF16)

            @pl.when((f == 0) & (b > 0))
            def _():
                scatter_wait(jnp.maximum(b - 1, 0))

            yp = _dot(hb, wd_ref[...])
            yacc_ref[rows, :] = jnp.where(f == 0, yp, yacc_ref[rows, :] + yp)

        for m in range(1, xb_ref.shape[0] // sub + 1):
            pl.when(nsub == m)(functools.partial(compute, m))

        @pl.when(f == nf - 1)
        def _():
            full = nreal_ref[b] // sub
            issue_tiles(scatter_row, row0_ref[b], full)
            issue_range(scatter_row, row0_ref[b], full * sub, nreal_ref[b])

            @pl.when(b == nb - 1)
            def _():
                scatter_wait(b)


def _moe(be, row0, nsub, nreal, asg, h1, w_gate, w_up, w_down, n_out_rows):
    t, d = h1.shape
    ne, _, de = w_gate.shape
    nb = be.shape[0]
    nf = de // MOE_TF
    assert de % MOE_TF == 0 and t & (t - 1) == 0
    last = nf - 1
    fsel = lambda f, n: jnp.where(n > 0, f, last)
    grid_spec = pltpu.PrefetchScalarGridSpec(
        num_scalar_prefetch=5,
        grid=(nb, nf),
        in_specs=[
            pl.BlockSpec(memory_space=pl.ANY),
            pl.BlockSpec((None, d, MOE_TF), lambda b, f, be, r0, ns, nr, asg: (be[b], 0, fsel(f, ns[b]))),
            pl.BlockSpec((None, d, MOE_TF), lambda b, f, be, r0, ns, nr, asg: (be[b], 0, fsel(f, ns[b]))),
            pl.BlockSpec((None, MOE_TF, d), lambda b, f, be, r0, ns, nr, asg: (be[b], fsel(f, ns[b]), 0)),
        ],
        out_specs=pl.BlockSpec(memory_space=pl.ANY),
        scratch_shapes=[pltpu.VMEM((MOE_ROWS, d), F32),
                        pltpu.VMEM((MOE_ROWS, d), BF16),
                        pltpu.VMEM((MOE_ROWS, d), F32),
                        pltpu.SemaphoreType.DMA,
                        pltpu.SemaphoreType.DMA],
    )
    return pl.pallas_call(
        functools.partial(_moe_kernel, nb=nb, nf=nf, sub=MOE_SUB, tok_mask=t - 1),
        grid_spec=grid_spec,
        out_shape=jax.ShapeDtypeStruct((n_out_rows, d), F32),
        compiler_params=_params("arbitrary", "arbitrary", vmem=MOE_VMEM_LIMIT),
        name="moe",
    )(be, row0, nsub, nreal, asg, h1, w_gate, w_up, w_down)


def _combine_gate_kernel(h1_ref, y0_ref, y1_ref, pk_ref, g_ref, b_ref, wpg_ref, bpg_ref, p_ref, wpe_ref,
                         o_ref, za_ref, zb_ref, ha_ref, hb_ref, *, nj, tn, alpha):
    n = pl.program_id(0)
    j = n % nj
    parity = (n // nj) % 2

    @pl.when(n == 0)
    def _():
        for ref in (za_ref, zb_ref, ha_ref, hb_ref):
            ref[...] = jnp.zeros(ref.shape, ref.dtype)

    def layer_norm_tile(z_ref, h_ref):
        d = nj * tn
        mu = sum(jnp.sum(z_ref[jj], axis=-1, keepdims=True) for jj in range(nj)) / d
        var = sum(jnp.sum(jnp.square(z_ref[jj] - mu), axis=-1, keepdims=True) for jj in range(nj)) / d
        rstd = lax.rsqrt(var + LN_EPS)
        for jj in range(nj):
            cols = slice(jj * tn, (jj + 1) * tn)
            h = (z_ref[jj] - mu) * rstd * g_ref[:, cols] + b_ref[:, cols]
            z_ref[jj] = h
            h_ref[:, cols] = h.astype(BF16)

    def step(zw_ref, hw_ref, zo_ref, ho_ref, first):
        pk = pk_ref[...]
        ffn = pk[:, 2:3] * y0_ref[...] + pk[:, 3:4] * y1_ref[...]
        z = alpha * h1_ref[...] + ffn
        if first:
            zw_ref[0] = z
            layer_norm_tile(zo_ref, ho_ref)
            zg, hg = zw_ref[nj - 1], hw_ref[...]
        else:
            zw_ref[j] = z
            zg, hg = zo_ref[j - 1], ho_ref[...]
        gate = jax.nn.sigmoid(_dot(hg, wpg_ref[...]) + bpg_ref[...])
        ple = _dot(p_ref[...].astype(BF16), wpe_ref[...])
        o_ref[...] = zg + gate * ple

    slots = ((za_ref, ha_ref, zb_ref, hb_ref), (zb_ref, hb_ref, za_ref, ha_ref))
    for par in range(2):
        for first in (True, False):
            cond = (parity == par) & ((j == 0) if first else (j > 0))
            pl.when(cond)(functools.partial(step, *slots[par], first))


def _combine_gate(h1, y2, pk, g, b, w_pg_b, b_pg, p2, w_pe_b, alpha):
    t, d = h1.shape
    pd = p2.shape[1]
    tm, tn = MIX_TM, OUT_TN
    assert t % tm == 0 and d % tn == 0
    ni, nj = t // tm, d // tn
    n_comb = ni * nj
    cmb = lambda n: jnp.minimum(n, n_comb - 1)
    gat = lambda n: jnp.clip(n - nj - 1, 0, n_comb - 1)
    return pl.pallas_call(
        functools.partial(_combine_gate_kernel, nj=nj, tn=tn, alpha=alpha),
        grid=(n_comb + nj + 1,),
        in_specs=[
            pl.BlockSpec((tm, tn), lambda n: (cmb(n) // nj, cmb(n) % nj)),
            pl.BlockSpec((tm, tn), lambda n: (cmb(n) // nj, cmb(n) % nj)),
            pl.BlockSpec((tm, tn), lambda n: (cmb(n) // nj + ni, cmb(n) % nj)),
            pl.BlockSpec((tm, LANE), lambda n: (cmb(n) // nj, 0)),
            pl.BlockSpec((1, d), lambda n: (0, 0)),
            pl.BlockSpec((1, d), lambda n: (0, 0)),
            pl.BlockSpec((d, tn), lambda n: (0, gat(n) % nj)),
            pl.BlockSpec((1, tn), lambda n: (0, gat(n) % nj)),
            pl.BlockSpec((tm, pd), lambda n: (gat(n) // nj, 0)),
            pl.BlockSpec((pd, tn), lambda n: (0, gat(n) % nj)),
        ],
        out_specs=pl.BlockSpec((tm, tn), lambda n: (gat(n) // nj, gat(n) % nj)),
        out_shape=jax.ShapeDtypeStruct((t, d), F32),
        scratch_shapes=[pltpu.VMEM((nj, tm, tn), F32), pltpu.VMEM((nj, tm, tn), F32),
                        pltpu.VMEM((tm, d), BF16), pltpu.VMEM((tm, d), BF16)],
        compiler_params=_params("arbitrary"),
        name="combine_gate",
    )(h1, y2, y2, pk, g, b, w_pg_b, b_pg, p2, w_pe_b)


def _block_table(counts, n_asg):
    ne = counts.shape[0]
    padded = ((counts + MOE_SUB - 1) // MOE_SUB) * MOE_SUB
    pad_end = jnp.cumsum(padded)
    pad_start = pad_end - padded
    nblk = (padded + MOE_ROWS - 1) // MOE_ROWS
    blk_end = jnp.cumsum(nblk)
    blk_start = blk_end - nblk
    nb = ne + -(-n_asg // MOE_ROWS)
    bidx = jnp.arange(nb, dtype=jnp.int32)
    used = bidx < blk_end[-1]
    be = jnp.minimum(jnp.searchsorted(blk_end, jnp.minimum(bidx, blk_end[-1] - 1), side="right"), ne - 1)
    be = be.astype(jnp.int32)
    within = bidx - blk_start[be]
    row0 = pad_start[be] + within * MOE_ROWS
    nsub = jnp.clip((padded[be] - within * MOE_ROWS) // MOE_SUB, 0, MOE_ROWS // MOE_SUB)
    nsub = jnp.where(used, nsub, 0)
    row0 = jnp.where(used, row0, 0)
    nreal = jnp.where(used, jnp.clip(counts[be] - within * MOE_ROWS, 0, MOE_ROWS), 0)
    i32 = lambda a: a.astype(jnp.int32)
    return i32(pad_start), be, i32(row0), i32(nsub), i32(nreal)


def kernel(x, p, w_in, conv_w, sg_ln_g, sg_ln_b, sg_w, sg_b, w_out, ln1_g, ln1_b, w_rg, b_rg, w_re, b_re,
           w_gate, w_up, w_down, ln2_g, ln2_b, w_pg, b_pg, w_pe):
    depth = w_in.shape[0]
    bsz, seq, d = x.shape
    t = bsz * seq
    alpha = (2 * depth) ** 0.25
    n_groups = w_rg.shape[-1]
    ne = w_re.shape[-1]
    per_group = ne // n_groups
    n_asg = t * TOP_K
    n_rows = ((n_asg + ne * (MOE_SUB - 1) + MOE_SUB - 1) // MOE_SUB) * MOE_SUB

    h = x.reshape(t, d)
    for i in range(depth):
        cdim = conv_w.shape[-1]
        nj = cdim // MIX_TN
        w_in_p = _regroup_cols(w_in[i], 5, nj, MIX_TN)
        y, w_out_p, w_pg_b = _mixer(h, w_in_p, conv_w[i], sg_ln_g[i].reshape(1, -1), sg_ln_b[i].reshape(1, -1),
                                    sg_w[i], sg_b[i][:, :, None], w_out[i], w_pg[i], seq)
        wr = jnp.concatenate([w_rg[i], w_re[i], jnp.zeros((d, LANE - n_groups - ne), F32)], axis=1).astype(BF16)
        br = jnp.concatenate([b_rg[i], b_re[i], jnp.zeros((LANE - n_groups - ne,), F32)]).reshape(1, LANE)
        h1, logits = _outproj(y, w_out_p, h, ln1_g[i].reshape(1, d), ln1_b[i].reshape(1, d), wr, br, alpha)
        pk, cnt = _route(logits, n_groups, per_group)
        pad_start, be, row0, nsub, nreal = _block_table(cnt[0, :ne].astype(jnp.int32), n_asg)
        eid = jnp.concatenate([pk[:, 0], pk[:, 1]]).astype(jnp.int32)
        rank = jnp.concatenate([pk[:, 4], pk[:, 5]]).astype(jnp.int32)
        asg = _rowmap(eid, rank, pad_start, n_rows)
        y2 = _moe(be, row0, nsub, nreal, asg, h1, w_gate[i], w_up[i], w_down[i], n_asg)
        h = _combine_gate(h1, y2, pk, ln2_g[i].reshape(1, d), ln2_b[i].reshape(1, d), w_pg_b,
                          b_pg[i].reshape(1, d), p[i].reshape(t, -1), w_pe[i].astype(BF16), alpha)
    return h.reshape(bsz, seq, d)
```

```python
import functools

import jax
import jax.numpy as jnp
from jax import lax
from jax.experimental import pallas as pl
from jax.experimental.pallas import tpu as pltpu
from jax.experimental.pallas import tpu_sc as plsc

F32 = jnp.float32
BF16 = jnp.bfloat16

LANE = 128
SUBLANE = 8
MXU_COL = 256
SC_CORES = 2
SC_SUBCORES = 16
SC_LANES = 16
VMEM_LIMIT = 56 * 1024 * 1024
MOE_VMEM_LIMIT = 60 * 1024 * 1024

CHUNK = 64
SG_BLOCK = 128
LN_EPS = 1e-5
TOP_K = 2

MIX_TM = 512
MIX_TN = MXU_COL
OUT_TN = 512
ROUTE_TM = 512
MOE_SUB = 128
MOE_ROWS = 768
MOE_TF = MXU_COL
ISSUE_UNROLL = 8
ROWMAP_CHUNK = 4096
CAST_TK = 4096
CAST_ROWS = 16

_dot = functools.partial(jnp.dot, preferred_element_type=F32)


def _params(*sem, vmem=VMEM_LIMIT):
    return pltpu.CompilerParams(dimension_semantics=sem, vmem_limit_bytes=vmem)


def _layer_norm(z, g, b):
    mu = jnp.mean(z, axis=-1, keepdims=True)
    zc = z - mu
    var = jnp.mean(zc * zc, axis=-1, keepdims=True)
    return zc * lax.rsqrt(var + LN_EPS) * g + b


def _cast_kernel(w_ref, o_ref):
    o_ref[...] = w_ref[...].astype(o_ref.dtype)


def _regroup_cols(w, groups, nj, tn):
    d = w.shape[0]
    tk = min(CAST_TK, d)
    assert d % tk == 0 and w.shape[1] == groups * nj * tn
    return pl.pallas_call(
        _cast_kernel,
        grid=(nj, groups, d // tk),
        in_specs=[pl.BlockSpec((tk, tn), lambda j, s, k: (k, s * nj + j))],
        out_specs=pl.BlockSpec((tk, tn), lambda j, s, k: (k, j * groups + s)),
        out_shape=jax.ShapeDtypeStruct(w.shape, BF16),
        compiler_params=_params("arbitrary", "arbitrary", "arbitrary"),
        name="regroup_cols",
    )(w)


def _mixer_kernel(x_ref, w_ref, cw_ref, lng_ref, lnb_ref, sgw_ref, sgb_ref, wo_ref, wg_ref,
                  y_ref, wo_out_ref, wg_out_ref,
                  xb_ref, pa_ref, pb_ref, gbuf_ref, carry_ref, *, tm, tn, nj, n_steps, tiles_per_seq):
    n = pl.program_id(0)

    @pl.when(n == 0)
    def _():
        pb_ref[...] = jnp.zeros(pb_ref.shape, F32)
        carry_ref[...] = jnp.zeros(carry_ref.shape, F32)

    @pl.when((n % nj == 0) & (n < n_steps))
    def _():
        xb_ref[...] = x_ref[...].astype(BF16)

    m = jnp.maximum(n - 1, 0)
    jp = m % nj
    first = ((m // nj) % tiles_per_seq) == 0

    def mix(proj_ref):
        b, c, h, u, v = (proj_ref[:, k * tn:(k + 1) * tn] for k in range(5))
        g = c * h
        gbuf_ref[0:SUBLANE, :] = jnp.where(first, 0.0, carry_ref[jp])
        gbuf_ref[SUBLANE:, :] = g
        carry_ref[jp] = g[tm - SUBLANE:, :]
        g1 = gbuf_ref[pl.ds(SUBLANE - 1, tm), :]
        g2 = gbuf_ref[pl.ds(SUBLANE - 2, tm), :]
        cw = cw_ref[...]
        conv = cw[0:1, :] * g2 + cw[1:2, :] * g1 + cw[2:3, :] * g
        y_ref[:, 0:tn] = (b * conv).astype(y_ref.dtype)

        pos_i = lax.broadcasted_iota(jnp.int32, (SG_BLOCK, SG_BLOCK), 0)
        pos_j = lax.broadcasted_iota(jnp.int32, (SG_BLOCK, SG_BLOCK), 1)
        mask = (pos_j // CHUNK) <= (pos_i // CHUNK)
        for hh in range(tn // LANE):
            sl = slice(hh * LANE, (hh + 1) * LANE)
            vn = _layer_norm(jax.nn.gelu(v[:, sl]), lng_ref[:, sl], lnb_ref[:, sl]).astype(BF16)
            gu = jax.nn.gelu(u[:, sl])
            ws = jnp.where(mask, sgw_ref[hh], 0.0).astype(BF16)
            bcol = sgb_ref[hh]
            nblk = tm // SG_BLOCK
            vcat = jnp.concatenate([vn[r * SG_BLOCK:(r + 1) * SG_BLOCK, :] for r in range(nblk)], axis=1)
            sg = _dot(ws, vcat) + bcol
            for r in range(nblk):
                rows = slice(r * SG_BLOCK, (r + 1) * SG_BLOCK)
                y_ref[rows, tn + hh * LANE:tn + (hh + 1) * LANE] = (
                    gu[rows, :] * sg[:, r * LANE:(r + 1) * LANE]).astype(y_ref.dtype)

    def step(store_ref, load_ref):
        store_ref[...] = _dot(xb_ref[...], w_ref[...])
        mix(load_ref)
        wo_out_ref[...] = wo_ref[...].astype(BF16)
        wg_out_ref[...] = wg_ref[...].astype(BF16)

    pl.when(n % 2 == 0)(functools.partial(step, pa_ref, pb_ref))
    pl.when(n % 2 == 1)(functools.partial(step, pb_ref, pa_ref))


def _mixer(x2, w_in_p, conv_w, lng, lnb, sg_w, sg_bcol, w_out, w_pg, seq):
    t, d = x2.shape
    cdim = conv_w.shape[1]
    tm, tn = MIX_TM, MIX_TN
    nj = cdim // tn
    assert seq % tm == 0 and t % tm == 0 and cdim % tn == 0 and w_in_p.shape[1] == 5 * cdim
    hp = tn // LANE
    n_steps = (t // tm) * nj
    cur = lambda n: jnp.minimum(n, n_steps - 1)
    prv = lambda n: jnp.maximum(n - 1, 0)
    wrows = w_out.shape[0]
    cb = max(CAST_ROWS, -(-wrows // n_steps // CAST_ROWS) * CAST_ROWS)
    nblk = wrows // cb
    spb = n_steps // nblk
    bpt = tn // cb
    assert w_pg.shape == w_out.shape == (2 * cdim, d) and wrows % cb == 0 and n_steps % nblk == 0 and tn % cb == 0
    blk = lambda n: jnp.minimum(n // spb, nblk - 1)

    def wo_src(n):
        k = blk(n)
        tile, sub = k // bpt, k % bpt
        return ((tile % 2) * nj + tile // 2) * bpt + sub, 0

    return pl.pallas_call(
        functools.partial(_mixer_kernel, tm=tm, tn=tn, nj=nj, n_steps=n_steps, tiles_per_seq=seq // tm),
        grid=(n_steps + 1,),
        in_specs=[
            pl.BlockSpec((tm, d), lambda n: (cur(n) // nj, 0)),
            pl.BlockSpec((d, 5 * tn), lambda n: (0, cur(n) % nj)),
            pl.BlockSpec((conv_w.shape[0], tn), lambda n: (0, prv(n) % nj)),
            pl.BlockSpec((1, tn), lambda n: (0, prv(n) % nj)),
            pl.BlockSpec((1, tn), lambda n: (0, prv(n) % nj)),
            pl.BlockSpec((hp, SG_BLOCK, SG_BLOCK), lambda n: (prv(n) % nj, 0, 0)),
            pl.BlockSpec((hp, SG_BLOCK, 1), lambda n: (prv(n) % nj, 0, 0)),
            pl.BlockSpec((cb, d), wo_src),
            pl.BlockSpec((cb, d), lambda n: (blk(n), 0)),
        ],
        out_specs=[pl.BlockSpec((tm, 2 * tn), lambda n: (prv(n) // nj, prv(n) % nj)),
                   pl.BlockSpec((cb, d), lambda n: (blk(n), 0)),
                   pl.BlockSpec((cb, d), lambda n: (blk(n), 0))],
        out_shape=[jax.ShapeDtypeStruct((t, 2 * cdim), BF16),
                   jax.ShapeDtypeStruct(w_out.shape, BF16),
                   jax.ShapeDtypeStruct(w_pg.shape, BF16)],
        scratch_shapes=[pltpu.VMEM((tm, d), BF16),
                        pltpu.VMEM((tm, 5 * tn), F32),
                        pltpu.VMEM((tm, 5 * tn), F32),
                        pltpu.VMEM((tm + SUBLANE, tn), F32),
                        pltpu.VMEM((nj, SUBLANE, tn), F32)],
        compiler_params=_params("arbitrary"),
        name="mixer",
    )(x2, w_in_p, conv_w, lng, lnb, sg_w, sg_bcol, w_out, w_pg)


def _outproj_kernel(y_ref, w_ref, x_ref, g_ref, b_ref, wr_ref, br_ref, h1_ref, lg_ref, z_ref, *, tn, nj, alpha):
    j = pl.program_id(1)
    z_ref[j] = alpha * x_ref[...] + _dot(y_ref[...], w_ref[...])

    @pl.when(j == nj - 1)
    def _():
        d = nj * tn
        mu = sum(jnp.sum(z_ref[jj], axis=-1, keepdims=True) for jj in range(nj)) / d
        var = sum(jnp.sum(jnp.square(z_ref[jj] - mu), axis=-1, keepdims=True) for jj in range(nj)) / d
        rstd = lax.rsqrt(var + LN_EPS)
        logits = br_ref[...]
        for jj in range(nj):
            cols = slice(jj * tn, (jj + 1) * tn)
            h = (z_ref[jj] - mu) * rstd * g_ref[:, cols] + b_ref[:, cols]
            h1_ref[:, cols] = h
            logits = logits + _dot(h.astype(BF16), wr_ref[cols, :])
        lg_ref[...] = logits


def _outproj(y, w_out_p, x2, g, b, wr, br, alpha):
    t, d = x2.shape
    kdim = y.shape[1]
    tm, tn = MIX_TM, OUT_TN
    nj = d // tn
    assert t % tm == 0 and d % tn == 0 and w_out_p.shape[0] == kdim
    return pl.pallas_call(
        functools.partial(_outproj_kernel, tn=tn, nj=nj, alpha=alpha),
        grid=(t // tm, nj),
        in_specs=[
            pl.BlockSpec((tm, kdim), lambda i, j: (i, 0)),
            pl.BlockSpec((kdim, tn), lambda i, j: (0, j)),
            pl.BlockSpec((tm, tn), lambda i, j: (i, j)),
            pl.BlockSpec((1, d), lambda i, j: (0, 0)),
            pl.BlockSpec((1, d), lambda i, j: (0, 0)),
            pl.BlockSpec((d, LANE), lambda i, j: (0, 0)),
            pl.BlockSpec((1, LANE), lambda i, j: (0, 0)),
        ],
        out_specs=[pl.BlockSpec((tm, d), lambda i, j: (i, 0)),
                   pl.BlockSpec((tm, LANE), lambda i, j: (i, 0))],
        out_shape=[jax.ShapeDtypeStruct((t, d), F32), jax.ShapeDtypeStruct((t, LANE), F32)],
        scratch_shapes=[pltpu.VMEM((nj, tm, tn), F32)],
        compiler_params=_params("arbitrary", "arbitrary"),
        name="outproj",
    )(y, w_out_p, x2, g, b, wr, br)


def _route_kernel(lg_ref, pk_ref, cnt_ref, ids_ref, run_ref, *, n_groups, per_group):
    i = pl.program_id(0)

    @pl.when(i == 0)
    def _():
        run_ref[...] = jnp.zeros_like(run_ref)

    l = lg_ref[...]
    tm = l.shape[0]
    lane = lax.broadcasted_iota(jnp.int32, l.shape, 1)
    neg = jnp.float32(-jnp.inf)
    gmask = lane < n_groups
    gl = jnp.where(gmask, l, neg)
    gmax = jnp.max(gl, axis=-1, keepdims=True)
    gi = jnp.min(jnp.where(gl == gmax, lane, LANE), axis=-1, keepdims=True)
    gp = 1.0 / jnp.sum(jnp.where(gmask, jnp.exp(l - gmax), 0.0), axis=-1, keepdims=True)
    elane = lane - n_groups
    emask = (elane >= 0) & (elane // per_group == gi)
    el = jnp.where(emask, l, neg)
    m1 = jnp.max(el, axis=-1, keepdims=True)
    i1 = jnp.min(jnp.where(el == m1, lane, LANE), axis=-1, keepdims=True)
    el2 = jnp.where(lane == i1, neg, el)
    m2 = jnp.max(el2, axis=-1, keepdims=True)
    i2 = jnp.min(jnp.where(el2 == m2, lane, LANE), axis=-1, keepdims=True)
    t2 = jnp.exp(m2 - m1)
    w0 = gp / (1.0 + t2)
    w1 = gp * t2 / (1.0 + t2)
    e0 = i1 - n_groups
    e1 = i2 - n_groups
    oh0 = lane == e0
    oh1 = lane == e1
    oh = (oh0 | oh1).astype(BF16)
    row = lax.broadcasted_iota(jnp.int32, (tm, tm), 0)
    col = lax.broadcasted_iota(jnp.int32, (tm, tm), 1)
    tri = (row > col).astype(BF16)
    before = run_ref[...] + _dot(tri, oh)
    r0 = jnp.sum(jnp.where(oh0, before, 0.0), axis=-1, keepdims=True)
    r1 = jnp.sum(jnp.where(oh1, before, 0.0), axis=-1, keepdims=True)
    total = run_ref[...] + jnp.sum(oh.astype(F32), axis=0, keepdims=True)
    run_ref[...] = total
    cnt_ref[...] = total
    vals = (e0.astype(F32), e1.astype(F32), w0, w1, r0, r1)
    pk = jnp.zeros(l.shape, F32)
    for k, val in enumerate(vals):
        pk = jnp.where(lane == k, val, pk)
    pk_ref[...] = pk
    ids_ref[...] = pk.T[0:SUBLANE, :].astype(jnp.int32)


def _route(logits, n_groups, per_group):
    t = logits.shape[0]
    tm = ROUTE_TM
    assert t % tm == 0 and n_groups * (1 + per_group) <= LANE
    return pl.pallas_call(
        functools.partial(_route_kernel, n_groups=n_groups, per_group=per_group),
        grid=(t // tm,),
        in_specs=[pl.BlockSpec((tm, LANE), lambda i: (i, 0))],
        out_specs=[pl.BlockSpec((tm, LANE), lambda i: (i, 0)),
                   pl.BlockSpec((1, LANE), lambda i: (0, 0)),
                   pl.BlockSpec((SUBLANE, tm), lambda i: (0, i))],
        out_shape=[jax.ShapeDtypeStruct((t, LANE), F32), jax.ShapeDtypeStruct((1, LANE), F32),
                   jax.ShapeDtypeStruct((SUBLANE, t), jnp.int32)],
        scratch_shapes=[pltpu.VMEM((1, LANE), F32)],
        compiler_params=_params("arbitrary"),
        name="route",
    )(logits)


def _rowmap_kernel(eid_hbm, rank_hbm, start_hbm, asg_hbm, eid_v, rank_v, start_v, asg_v, *, n_rows, n_asg, chunk):
    on_first = (lax.axis_index("c") == 0) & (lax.axis_index("s") == 0)

    @pl.when(on_first)
    def _():
        pltpu.sync_copy(start_hbm, start_v)
        zeros = jnp.zeros((SC_LANES,), jnp.int32)

        @pl.loop(0, n_rows, step=SC_LANES)
        def _(r):
            asg_v[pl.ds(r, SC_LANES)] = zeros

        lanes = lax.iota(jnp.int32, SC_LANES)

        @pl.loop(0, n_asg, step=chunk)
        def _(a0):
            pltpu.sync_copy(eid_hbm.at[pl.ds(a0, chunk)], eid_v)
            pltpu.sync_copy(rank_hbm.at[pl.ds(a0, chunk)], rank_v)

            @pl.loop(0, chunk, step=SC_LANES)
            def _(k):
                dest = plsc.load_gather(start_v, [eid_v[pl.ds(k, SC_LANES)]]) + rank_v[pl.ds(k, SC_LANES)]
                plsc.store_scatter(asg_v, [dest], lanes + (a0 + k))

        pltpu.sync_copy(asg_v, asg_hbm)


def _rowmap(eid, rank, start, n_rows):
    n_asg = eid.shape[0]
    chunk = min(ROWMAP_CHUNK, n_asg)
    assert n_asg % chunk == 0 and chunk % SC_LANES == 0 and n_rows % SC_LANES == 0 and start.shape[0] % SC_LANES == 0
    mesh = plsc.VectorSubcoreMesh(core_axis_name="c", subcore_axis_name="s",
                                  num_cores=SC_CORES, num_subcores=SC_SUBCORES)
    i32 = jnp.int32
    return pl.kernel(
        functools.partial(_rowmap_kernel, n_rows=n_rows, n_asg=n_asg, chunk=chunk),
        out_type=jax.ShapeDtypeStruct((n_rows,), i32),
        mesh=mesh,
        scratch_types=[pltpu.VMEM((chunk,), i32), pltpu.VMEM((chunk,), i32),
                       pltpu.VMEM(start.shape, i32), pltpu.VMEM((n_rows,), i32)],
        compiler_params=pltpu.CompilerParams(needs_layout_passes=False),
        name="rowmap",
    )(eid, rank, start)


def _moe_kernel(be_ref, row0_ref, nsub_ref, nreal_ref, asg_ref, h1_hbm, wg_ref, wu_ref, wd_ref, y_hbm,
                xg_ref, xb_ref, yacc_ref, gsem, ssem, *, nb, nf, sub, tok_mask):
    b = pl.program_id(0)
    f = pl.program_id(1)
    nsub = nsub_ref[b]
    max_tiles = xb_ref.shape[0] // sub

    def gather_row(base, r):
        tok = asg_ref[base + r] & tok_mask
        pltpu.make_async_copy(h1_hbm.at[pl.ds(tok, 1)], xg_ref.at[pl.ds(r, 1)], gsem).start()

    def scatter_row(base, r):
        dst = asg_ref[base + r]
        pltpu.make_async_copy(yacc_ref.at[pl.ds(r, 1)], y_hbm.at[pl.ds(dst, 1)], ssem).start()

    def issue_tiles(row_fn, base, n_tiles):
        for s in range(max_tiles):
            @pl.when(s < n_tiles)
            def _(s=s):
                for r in range(s * sub, (s + 1) * sub):
                    row_fn(base, r)

    def issue_range(row_fn, base, lo, hi):
        groups = (hi - lo) // ISSUE_UNROLL

        def body(q, c):
            for k in range(ISSUE_UNROLL):
                row_fn(base, lo + q * ISSUE_UNROLL + k)
            return c

        lax.fori_loop(0, groups, body, 0)

        def tail(r, c):
            row_fn(base, r)
            return c

        lax.fori_loop(lo + groups * ISSUE_UNROLL, hi, tail, 0)

    def wait_rows(n, tile_copy, row_copy):
        def tile(s, c):
            tile_copy.wait()
            return c

        lax.fori_loop(0, n // sub, tile, 0)

        def row(r, c):
            row_copy.wait()
            return c

        lax.fori_loop((n // sub) * sub, n, row, 0)

    def gather_wait(n):
        wait_rows(n, pltpu.make_async_copy(h1_hbm.at[pl.ds(0, sub)], xg_ref.at[pl.ds(0, sub)], gsem),
                  pltpu.make_async_copy(h1_hbm.at[pl.ds(0, 1)], xg_ref.at[pl.ds(0, 1)], gsem))

    def scatter_wait(blk):
        wait_rows(nreal_ref[blk], pltpu.make_async_copy(yacc_ref.at[pl.ds(0, sub)], y_hbm.at[pl.ds(0, sub)], ssem),
                  pltpu.make_async_copy(yacc_ref.at[pl.ds(0, 1)], y_hbm.at[pl.ds(0, 1)], ssem))

    @pl.when((f == 0) & (b == 0))
    def _():
        yacc_ref[...] = jnp.zeros(yacc_ref.shape, F32)

    @pl.when((f == 0) & (b > 0) & (nsub == 0))
    def _():
        scatter_wait(jnp.maximum(b - 1, 0))

    @pl.when(nsub > 0)
    def _():
        @pl.when(f == 0)
        def _():
            @pl.when(b == 0)
            def _():
                issue_range(gather_row, row0_ref[0], 0, nsub * sub)

            gather_wait(nsub * sub)

            def cast(s, c):
                rows = pl.ds(pl.multiple_of(s * sub, sub), sub)
                xb_ref[rows, :] = xg_ref[rows, :].astype(BF16)
                return c

            lax.fori_loop(0, nsub, cast, 0)

            @pl.when(b + 1 < nb)
            def _():
                nxt = jnp.minimum(b + 1, nb - 1)
                issue_tiles(gather_row, row0_ref[nxt], nsub_ref[nxt])

        def compute(m):
            rows = slice(0, m * sub)
            xs = xb_ref[rows, :]
            hb = (jax.nn.silu(_dot(xs, wg_ref[...])) * _dot(xs, wu_ref[...])).astype(BF16)

            @pl.when((f == 0) & (b > 0))
            def _():
                scatter_wait(jnp.maximum(b - 1, 0))

            yp = _dot(hb, wd_ref[...])
            yacc_ref[rows, :] = jnp.where(f == 0, yp, yacc_ref[rows, :] + yp)

        for m in range(1, xb_ref.shape[0] // sub + 1):
            pl.when(nsub == m)(functools.partial(compute, m))

        @pl.when(f == nf - 1)
        def _():
            full = nreal_ref[b] // sub
            issue_tiles(scatter_row, row0_ref[b], full)
            issue_range(scatter_row, row0_ref[b], full * sub, nreal_ref[b])

            @pl.when(b == nb - 1)
            def _():
                scatter_wait(b)


def _moe(be, row0, nsub, nreal, asg, h1, w_gate, w_up, w_down, n_out_rows):
    t, d = h1.shape
    ne, _, de = w_gate.shape
    nb = be.shape[0]
    nf = de // MOE_TF
    assert de % MOE_TF == 0 and t & (t - 1) == 0
    last = nf - 1
    fsel = lambda f, n: jnp.where(n > 0, f, last)
    grid_spec = pltpu.PrefetchScalarGridSpec(
        num_scalar_prefetch=5,
        grid=(nb, nf),
        in_specs=[
            pl.BlockSpec(memory_space=pl.ANY),
            pl.BlockSpec((None, d, MOE_TF), lambda b, f, be, r0, ns, nr, asg: (be[b], 0, fsel(f, ns[b]))),
            pl.BlockSpec((None, d, MOE_TF), lambda b, f, be, r0, ns, nr, asg: (be[b], 0, fsel(f, ns[b]))),
            pl.BlockSpec((None, MOE_TF, d), lambda b, f, be, r0, ns, nr, asg: (be[b], fsel(f, ns[b]), 0)),
        ],
        out_specs=pl.BlockSpec(memory_space=pl.ANY),
        scratch_shapes=[pltpu.VMEM((MOE_ROWS, d), F32),
                        pltpu.VMEM((MOE_ROWS, d), BF16),
                        pltpu.VMEM((MOE_ROWS, d), F32),
                        pltpu.SemaphoreType.DMA,
                        pltpu.SemaphoreType.DMA],
    )
    return pl.pallas_call(
        functools.partial(_moe_kernel, nb=nb, nf=nf, sub=MOE_SUB, tok_mask=t - 1),
        grid_spec=grid_spec,
        out_shape=jax.ShapeDtypeStruct((n_out_rows, d), F32),
        compiler_params=_params("arbitrary", "arbitrary", vmem=MOE_VMEM_LIMIT),
        name="moe",
    )(be, row0, nsub, nreal, asg, h1, w_gate, w_up, w_down)


def _combine_gate_kernel(h1_ref, y0_ref, y1_ref, pk_ref, g_ref, b_ref, wpg_ref, bpg_ref, p_ref, wpe_ref,
                         o_ref, za_ref, zb_ref, ha_ref, hb_ref, *, nj, tn, alpha):
    n = pl.program_id(0)
    j = n % nj
    parity = (n // nj) % 2

    @pl.when(n == 0)
    def _():
        for ref in (za_ref, zb_ref, ha_ref, hb_ref):
            ref[...] = jnp.zeros(ref.shape, ref.dtype)

    def layer_norm_tile(z_ref, h_ref):
        d = nj * tn
        mu = sum(jnp.sum(z_ref[jj], axis=-1, keepdims=True) for jj in range(nj)) / d
        var = sum(jnp.sum(jnp.square(z_ref[jj] - mu), axis=-1, keepdims=True) for jj in range(nj)) / d
        rstd = lax.rsqrt(var + LN_EPS)
        for jj in range(nj):
            cols = slice(jj * tn, (jj + 1) * tn)
            h = (z_ref[jj] - mu) * rstd * g_ref[:, cols] + b_ref[:, cols]
            z_ref[jj] = h
            h_ref[:, cols] = h.astype(BF16)

    def step(zw_ref, hw_ref, zo_ref, ho_ref, first):
        pk = pk_ref[...]
        ffn = pk[:, 2:3] * y0_ref[...] + pk[:, 3:4] * y1_ref[...]
        z = alpha * h1_ref[...] + ffn
        if first:
            zw_ref[0] = z
            layer_norm_tile(zo_ref, ho_ref)
            zg, hg = zw_ref[nj - 1], hw_ref[...]
        else:
            zw_ref[j] = z
            zg, hg = zo_ref[j - 1], ho_ref[...]
        gate = jax.nn.sigmoid(_dot(hg, wpg_ref[...]) + bpg_ref[...])
        ple = _dot(p_ref[...].astype(BF16), wpe_ref[...])
        o_ref[...] = zg + gate * ple

    slots = ((za_ref, ha_ref, zb_ref, hb_ref), (zb_ref, hb_ref, za_ref, ha_ref))
    for par in range(2):
        for first in (True, False):
            cond = (parity == par) & ((j == 0) if first else (j > 0))
            pl.when(cond)(functools.partial(step, *slots[par], first))


def _combine_gate(h1, y2, pk, g, b, w_pg_b, b_pg, p2, w_pe_b, alpha):
    t, d = h1.shape
    pd = p2.shape[1]
    tm, tn = MIX_TM, OUT_TN
    assert t % tm == 0 and d % tn == 0
    ni, nj = t // tm, d // tn
    n_comb = ni * nj
    cmb = lambda n: jnp.minimum(n, n_comb - 1)
    gat = lambda n: jnp.clip(n - nj - 1, 0, n_comb - 1)
    return pl.pallas_call(
        functools.partial(_combine_gate_kernel, nj=nj, tn=tn, alpha=alpha),
        grid=(n_comb + nj + 1,),
        in_specs=[
            pl.BlockSpec((tm, tn), lambda n: (cmb(n) // nj, cmb(n) % nj)),
            pl.BlockSpec((tm, tn), lambda n: (cmb(n) // nj, cmb(n) % nj)),
            pl.BlockSpec((tm, tn), lambda n: (cmb(n) // nj + ni, cmb(n) % nj)),
            pl.BlockSpec((tm, LANE), lambda n: (cmb(n) // nj, 0)),
            pl.BlockSpec((1, d), lambda n: (0, 0)),
            pl.BlockSpec((1, d), lambda n: (0, 0)),
            pl.BlockSpec((d, tn), lambda n: (0, gat(n) % nj)),
            pl.BlockSpec((1, tn), lambda n: (0, gat(n) % nj)),
            pl.BlockSpec((tm, pd), lambda n: (gat(n) // nj, 0)),
            pl.BlockSpec((pd, tn), lambda n: (0, gat(n) % nj)),
        ],
        out_specs=pl.BlockSpec((tm, tn), lambda n: (gat(n) // nj, gat(n) % nj)),
        out_shape=jax.ShapeDtypeStruct((t, d), F32),
        scratch_shapes=[pltpu.VMEM((nj, tm, tn), F32), pltpu.VMEM((nj, tm, tn), F32),
                        pltpu.VMEM((tm, d), BF16), pltpu.VMEM((tm, d), BF16)],
        compiler_params=_params("arbitrary"),
        name="combine_gate",
    )(h1, y2, y2, pk, g, b, w_pg_b, b_pg, p2, w_pe_b)


def _block_table(counts, n_asg):
    ne = counts.shape[0]
    padded = ((counts + MOE_SUB - 1) // MOE_SUB) * MOE_SUB
    pad_end = jnp.cumsum(padded)
    pad_start = pad_end - padded
    nblk = (padded + MOE_ROWS - 1) // MOE_ROWS
    blk_end = jnp.cumsum(nblk)
    blk_start = blk_end - nblk
    nb = ne + -(-n_asg // MOE_ROWS)
    bidx = jnp.arange(nb, dtype=jnp.int32)
    used = bidx < blk_end[-1]
    be = jnp.minimum(jnp.searchsorted(blk_end, jnp.minimum(bidx, blk_end[-1] - 1), side="right"), ne - 1)
    be = be.astype(jnp.int32)
    within = bidx - blk_start[be]
    row0 = pad_start[be] + within * MOE_ROWS
    nsub = jnp.clip((padded[be] - within * MOE_ROWS) // MOE_SUB, 0, MOE_ROWS // MOE_SUB)
    nsub = jnp.where(used, nsub, 0)
    row0 = jnp.where(used, row0, 0)
    nreal = jnp.where(used, jnp.clip(counts[be] - within * MOE_ROWS, 0, MOE_ROWS), 0)
    i32 = lambda a: a.astype(jnp.int32)
    return i32(pad_start), be, i32(row0), i32(nsub), i32(nreal)


def kernel(x, p, w_in, conv_w, sg_ln_g, sg_ln_b, sg_w, sg_b, w_out, ln1_g, ln1_b, w_rg, b_rg, w_re, b_re,
           w_gate, w_up, w_down, ln2_g, ln2_b, w_pg, b_pg, w_pe):
    depth = w_in.shape[0]
    bsz, seq, d = x.shape
    t = bsz * seq
    alpha = (2 * depth) ** 0.25
    n_groups = w_rg.shape[-1]
    ne = w_re.shape[-1]
    per_group = ne // n_groups
    n_asg = t * TOP_K
    n_rows = ((n_asg + ne * (MOE_SUB - 1) + MOE_SUB - 1) // MOE_SUB) * MOE_SUB

    h = x.reshape(t, d)
    for i in range(depth):
        cdim = conv_w.shape[-1]
        nj = cdim // MIX_TN
        w_in_p = _regroup_cols(w_in[i], 5, nj, MIX_TN)
        y, w_out_p, w_pg_b = _mixer(h, w_in_p, conv_w[i], sg_ln_g[i].reshape(1, -1), sg_ln_b[i].reshape(1, -1),
                                    sg_w[i], sg_b[i][:, :, None], w_out[i], w_pg[i], seq)
        wr = jnp.concatenate([w_rg[i], w_re[i], jnp.zeros((d, LANE - n_groups - ne), F32)], axis=1).astype(BF16)
        br = jnp.concatenate([b_rg[i], b_re[i], jnp.zeros((LANE - n_groups - ne,), F32)]).reshape(1, LANE)
        h1, logits = _outproj(y, w_out_p, h, ln1_g[i].reshape(1, d), ln1_b[i].reshape(1, d), wr, br, alpha)
        pk, cnt, ids = _route(logits, n_groups, per_group)
        pad_start, be, row0, nsub, nreal = _block_table(cnt[0, :ne].astype(jnp.int32), n_asg)
        eid = ids[0:2].reshape(-1)
        rank = ids[4:6].reshape(-1)
        asg = _rowmap(eid, rank, pad_start, n_rows)
        y2 = _moe(be, row0, nsub, nreal, asg, h1, w_gate[i], w_up[i], w_down[i], n_asg)
        h = _combine_gate(h1, y2, pk, ln2_g[i].reshape(1, d), ln2_b[i].reshape(1, d), w_pg_b,
                          b_pg[i].reshape(1, d), p[i].reshape(t, -1), w_pe[i].astype(BF16), alpha)
    return h.reshape(bsz, seq, d)
```

```python
import functools

import jax
import jax.numpy as jnp
from jax import lax
from jax.experimental import pallas as pl
from jax.experimental.pallas import tpu as pltpu
from jax.experimental.pallas import tpu_sc as plsc

F32 = jnp.float32
BF16 = jnp.bfloat16

LANE = 128
SUBLANE = 8
MXU_COL = 256
SC_CORES = 2
SC_SUBCORES = 16
SC_LANES = 16
VMEM_LIMIT = 56 * 1024 * 1024
MOE_VMEM_LIMIT = 60 * 1024 * 1024

CHUNK = 64
SG_BLOCK = 128
LN_EPS = 1e-5
TOP_K = 2

MIX_TM = 512
MIX_TN = MXU_COL
OUT_TN = 512
ROUTE_TM = 512
MOE_SUB = 128
MOE_ROWS = 768
MOE_TF = MXU_COL
ISSUE_UNROLL = 8
ROWMAP_CHUNK = 4096
CAST_TK = 4096
CAST_ROWS = 16

_dot = functools.partial(jnp.dot, preferred_element_type=F32)


def _params(*sem, vmem=VMEM_LIMIT):
    return pltpu.CompilerParams(dimension_semantics=sem, vmem_limit_bytes=vmem)


def _layer_norm(z, g, b):
    mu = jnp.mean(z, axis=-1, keepdims=True)
    zc = z - mu
    var = jnp.mean(zc * zc, axis=-1, keepdims=True)
    return zc * lax.rsqrt(var + LN_EPS) * g + b


def _cast_kernel(w_ref, o_ref):
    o_ref[...] = w_ref[...].astype(o_ref.dtype)


def _regroup_cols(w, groups, nj, tn):
    d = w.shape[0]
    tk = min(CAST_TK, d)
    assert d % tk == 0 and w.shape[1] == groups * nj * tn
    return pl.pallas_call(
        _cast_kernel,
        grid=(nj, groups, d // tk),
        in_specs=[pl.BlockSpec((tk, tn), lambda j, s, k: (k, s * nj + j))],
        out_specs=pl.BlockSpec((tk, tn), lambda j, s, k: (k, j * groups + s)),
        out_shape=jax.ShapeDtypeStruct(w.shape, BF16),
        compiler_params=_params("arbitrary", "arbitrary", "arbitrary"),
        name="regroup_cols",
    )(w)


def _mixer_kernel(x_ref, w_ref, cw_ref, lng_ref, lnb_ref, sgw_ref, sgb_ref, wo_ref, wg_ref,
                  y_ref, wo_out_ref, wg_out_ref,
                  xb_ref, pa_ref, pb_ref, gbuf_ref, carry_ref, *, tm, tn, nj, n_steps, tiles_per_seq):
    n = pl.program_id(0)

    @pl.when(n == 0)
    def _():
        pb_ref[...] = jnp.zeros(pb_ref.shape, F32)
        carry_ref[...] = jnp.zeros(carry_ref.shape, F32)

    @pl.when((n % nj == 0) & (n < n_steps))
    def _():
        xb_ref[...] = x_ref[...].astype(BF16)

    m = jnp.maximum(n - 1, 0)
    jp = m % nj
    first = ((m // nj) % tiles_per_seq) == 0

    def mix(proj_ref):
        b, c, h, u, v = (proj_ref[:, k * tn:(k + 1) * tn] for k in range(5))
        g = c * h
        gbuf_ref[0:SUBLANE, :] = jnp.where(first, 0.0, carry_ref[jp])
        gbuf_ref[SUBLANE:, :] = g
        carry_ref[jp] = g[tm - SUBLANE:, :]
        g1 = gbuf_ref[pl.ds(SUBLANE - 1, tm), :]
        g2 = gbuf_ref[pl.ds(SUBLANE - 2, tm), :]
        cw = cw_ref[...]
        conv = cw[0:1, :] * g2 + cw[1:2, :] * g1 + cw[2:3, :] * g
        y_ref[:, 0:tn] = (b * conv).astype(y_ref.dtype)

        pos_i = lax.broadcasted_iota(jnp.int32, (SG_BLOCK, SG_BLOCK), 0)
        pos_j = lax.broadcasted_iota(jnp.int32, (SG_BLOCK, SG_BLOCK), 1)
        mask = (pos_j // CHUNK) <= (pos_i // CHUNK)
        for hh in range(tn // LANE):
            sl = slice(hh * LANE, (hh + 1) * LANE)
            vn = _layer_norm(jax.nn.gelu(v[:, sl]), lng_ref[:, sl], lnb_ref[:, sl]).astype(BF16)
            gu = jax.nn.gelu(u[:, sl])
            ws = jnp.where(mask, sgw_ref[hh], 0.0).astype(BF16)
            bcol = sgb_ref[hh]
            nblk = tm // SG_BLOCK
            vcat = jnp.concatenate([vn[r * SG_BLOCK:(r + 1) * SG_BLOCK, :] for r in range(nblk)], axis=1)
            sg = _dot(ws, vcat) + bcol
            for r in range(nblk):
                rows = slice(r * SG_BLOCK, (r + 1) * SG_BLOCK)
                y_ref[rows, tn + hh * LANE:tn + (hh + 1) * LANE] = (
                    gu[rows, :] * sg[:, r * LANE:(r + 1) * LANE]).astype(y_ref.dtype)

    def step(store_ref, load_ref):
        store_ref[...] = _dot(xb_ref[...], w_ref[...])
        mix(load_ref)
        wo_out_ref[...] = wo_ref[...].astype(BF16)
        wg_out_ref[...] = wg_ref[...].astype(BF16)

    pl.when(n % 2 == 0)(functools.partial(step, pa_ref, pb_ref))
    pl.when(n % 2 == 1)(functools.partial(step, pb_ref, pa_ref))


def _mixer(x2, w_in_p, conv_w, lng, lnb, sg_w, sg_bcol, w_out, w_pg, seq):
    t, d = x2.shape
    cdim = conv_w.shape[1]
    tm, tn = MIX_TM, MIX_TN
    nj = cdim // tn
    assert seq % tm == 0 and t % tm == 0 and cdim % tn == 0 and w_in_p.shape[1] == 5 * cdim
    hp = tn // LANE
    n_steps = (t // tm) * nj
    cur = lambda n: jnp.minimum(n, n_steps - 1)
    prv = lambda n: jnp.maximum(n - 1, 0)
    wrows = w_out.shape[0]
    cb = max(CAST_ROWS, -(-wrows // n_steps // CAST_ROWS) * CAST_ROWS)
    nblk = wrows // cb
    spb = n_steps // nblk
    bpt = tn // cb
    assert w_pg.shape == w_out.shape == (2 * cdim, d) and wrows % cb == 0 and n_steps % nblk == 0 and tn % cb == 0
    blk = lambda n: jnp.minimum(n // spb, nblk - 1)

    def wo_src(n):
        k = blk(n)
        tile, sub = k // bpt, k % bpt
        return ((tile % 2) * nj + tile // 2) * bpt + sub, 0

    return pl.pallas_call(
        functools.partial(_mixer_kernel, tm=tm, tn=tn, nj=nj, n_steps=n_steps, tiles_per_seq=seq // tm),
        grid=(n_steps + 1,),
        in_specs=[
            pl.BlockSpec((tm, d), lambda n: (cur(n) // nj, 0)),
            pl.BlockSpec((d, 5 * tn), lambda n: (0, cur(n) % nj)),
            pl.BlockSpec((conv_w.shape[0], tn), lambda n: (0, prv(n) % nj)),
            pl.BlockSpec((1, tn), lambda n: (0, prv(n) % nj)),
            pl.BlockSpec((1, tn), lambda n: (0, prv(n) % nj)),
            pl.BlockSpec((hp, SG_BLOCK, SG_BLOCK), lambda n: (prv(n) % nj, 0, 0)),
            pl.BlockSpec((hp, SG_BLOCK, 1), lambda n: (prv(n) % nj, 0, 0)),
            pl.BlockSpec((cb, d), wo_src),
            pl.BlockSpec((cb, d), lambda n: (blk(n), 0)),
        ],
        out_specs=[pl.BlockSpec((tm, 2 * tn), lambda n: (prv(n) // nj, prv(n) % nj)),
                   pl.BlockSpec((cb, d), lambda n: (blk(n), 0)),
                   pl.BlockSpec((cb, d), lambda n: (blk(n), 0))],
        out_shape=[jax.ShapeDtypeStruct((t, 2 * cdim), BF16),
                   jax.ShapeDtypeStruct(w_out.shape, BF16),
                   jax.ShapeDtypeStruct(w_pg.shape, BF16)],
        scratch_shapes=[pltpu.VMEM((tm, d), BF16),
                        pltpu.VMEM((tm, 5 * tn), F32),
                        pltpu.VMEM((tm, 5 * tn), F32),
                        pltpu.VMEM((tm + SUBLANE, tn), F32),
                        pltpu.VMEM((nj, SUBLANE, tn), F32)],
        compiler_params=_params("arbitrary"),
        name="mixer",
    )(x2, w_in_p, conv_w, lng, lnb, sg_w, sg_bcol, w_out, w_pg)


def _outproj_kernel(y_ref, w_ref, x_ref, g_ref, b_ref, wr_ref, br_ref, h1_ref, lg_ref, z_ref, *, tn, nj, alpha):
    j = pl.program_id(1)
    z_ref[j] = alpha * x_ref[...] + _dot(y_ref[...], w_ref[...])

    @pl.when(j == nj - 1)
    def _():
        d = nj * tn
        mu = sum(jnp.sum(z_ref[jj], axis=-1, keepdims=True) for jj in range(nj)) / d
        var = sum(jnp.sum(jnp.square(z_ref[jj] - mu), axis=-1, keepdims=True) for jj in range(nj)) / d
        rstd = lax.rsqrt(var + LN_EPS)
        logits = br_ref[...]
        for jj in range(nj):
            cols = slice(jj * tn, (jj + 1) * tn)
            h = (z_ref[jj] - mu) * rstd * g_ref[:, cols] + b_ref[:, cols]
            h1_ref[:, cols] = h
            logits = logits + _dot(h.astype(BF16), wr_ref[cols, :])
        lg_ref[...] = logits


def _outproj(y, w_out_p, x2, g, b, wr, br, alpha):
    t, d = x2.shape
    kdim = y.shape[1]
    tm, tn = MIX_TM, OUT_TN
    nj = d // tn
    assert t % tm == 0 and d % tn == 0 and w_out_p.shape[0] == kdim
    return pl.pallas_call(
        functools.partial(_outproj_kernel, tn=tn, nj=nj, alpha=alpha),
        grid=(t // tm, nj),
        in_specs=[
            pl.BlockSpec((tm, kdim), lambda i, j: (i, 0)),
            pl.BlockSpec((kdim, tn), lambda i, j: (0, j)),
            pl.BlockSpec((tm, tn), lambda i, j: (i, j)),
            pl.BlockSpec((1, d), lambda i, j: (0, 0)),
            pl.BlockSpec((1, d), lambda i, j: (0, 0)),
            pl.BlockSpec((d, LANE), lambda i, j: (0, 0)),
            pl.BlockSpec((1, LANE), lambda i, j: (0, 0)),
        ],
        out_specs=[pl.BlockSpec((tm, d), lambda i, j: (i, 0)),
                   pl.BlockSpec((tm, LANE), lambda i, j: (i, 0))],
        out_shape=[jax.ShapeDtypeStruct((t, d), F32), jax.ShapeDtypeStruct((t, LANE), F32)],
        scratch_shapes=[pltpu.VMEM((nj, tm, tn), F32)],
        compiler_params=_params("arbitrary", "arbitrary"),
        name="outproj",
    )(y, w_out_p, x2, g, b, wr, br)


def _route_kernel(lg_ref, pk_ref, cnt_ref, run_ref, *, n_groups, per_group):
    i = pl.program_id(0)

    @pl.when(i == 0)
    def _():
        run_ref[...] = jnp.zeros_like(run_ref)

    l = lg_ref[...]
    tm = l.shape[0]
    lane = lax.broadcasted_iota(jnp.int32, l.shape, 1)
    neg = jnp.float32(-jnp.inf)
    gmask = lane < n_groups
    gl = jnp.where(gmask, l, neg)
    gmax = jnp.max(gl, axis=-1, keepdims=True)
    gi = jnp.min(jnp.where(gl == gmax, lane, LANE), axis=-1, keepdims=True)
    gp = 1.0 / jnp.sum(jnp.where(gmask, jnp.exp(l - gmax), 0.0), axis=-1, keepdims=True)
    elane = lane - n_groups
    emask = (elane >= 0) & (elane // per_group == gi)
    el = jnp.where(emask, l, neg)
    m1 = jnp.max(el, axis=-1, keepdims=True)
    i1 = jnp.min(jnp.where(el == m1, lane, LANE), axis=-1, keepdims=True)
    el2 = jnp.where(lane == i1, neg, el)
    m2 = jnp.max(el2, axis=-1, keepdims=True)
    i2 = jnp.min(jnp.where(el2 == m2, lane, LANE), axis=-1, keepdims=True)
    t2 = jnp.exp(m2 - m1)
    w0 = gp / (1.0 + t2)
    w1 = gp * t2 / (1.0 + t2)
    e0 = i1 - n_groups
    e1 = i2 - n_groups
    oh0 = lane == e0
    oh1 = lane == e1
    oh = (oh0 | oh1).astype(BF16)
    row = lax.broadcasted_iota(jnp.int32, (tm, tm), 0)
    col = lax.broadcasted_iota(jnp.int32, (tm, tm), 1)
    tri = (row > col).astype(BF16)
    before = run_ref[...] + _dot(tri, oh)
    r0 = jnp.sum(jnp.where(oh0, before, 0.0), axis=-1, keepdims=True)
    r1 = jnp.sum(jnp.where(oh1, before, 0.0), axis=-1, keepdims=True)
    total = run_ref[...] + jnp.sum(oh.astype(F32), axis=0, keepdims=True)
    run_ref[...] = total
    cnt_ref[...] = total
    vals = (e0.astype(F32), e1.astype(F32), w0, w1, r0, r1)
    pk = jnp.zeros(l.shape, F32)
    for k, val in enumerate(vals):
        pk = jnp.where(lane == k, val, pk)
    pk_ref[...] = pk


def _route(logits, n_groups, per_group):
    t = logits.shape[0]
    tm = ROUTE_TM
    assert t % tm == 0 and n_groups * (1 + per_group) <= LANE
    return pl.pallas_call(
        functools.partial(_route_kernel, n_groups=n_groups, per_group=per_group),
        grid=(t // tm,),
        in_specs=[pl.BlockSpec((tm, LANE), lambda i: (i, 0))],
        out_specs=[pl.BlockSpec((tm, LANE), lambda i: (i, 0)),
                   pl.BlockSpec((1, LANE), lambda i: (0, 0))],
        out_shape=[jax.ShapeDtypeStruct((t, LANE), F32), jax.ShapeDtypeStruct((1, LANE), F32)],
        scratch_shapes=[pltpu.VMEM((1, LANE), F32)],
        compiler_params=_params("arbitrary"),
        name="route",
    )(logits)


def _rowmap_kernel(eid_hbm, rank_hbm, start_hbm, asg_hbm, eid_v, rank_v, start_v, asg_v, *, n_rows, n_asg, chunk):
    on_first = (lax.axis_index("c") == 0) & (lax.axis_index("s") == 0)

    @pl.when(on_first)
    def _():
        pltpu.sync_copy(start_hbm, start_v)
        zeros = jnp.zeros((SC_LANES,), jnp.int32)

        @pl.loop(0, n_rows, step=SC_LANES)
        def _(r):
            asg_v[pl.ds(r, SC_LANES)] = zeros

        lanes = lax.iota(jnp.int32, SC_LANES)

        @pl.loop(0, n_asg, step=chunk)
        def _(a0):
            pltpu.sync_copy(eid_hbm.at[pl.ds(a0, chunk)], eid_v)
            pltpu.sync_copy(rank_hbm.at[pl.ds(a0, chunk)], rank_v)

            @pl.loop(0, chunk, step=SC_LANES)
            def _(k):
                dest = plsc.load_gather(start_v, [eid_v[pl.ds(k, SC_LANES)]]) + rank_v[pl.ds(k, SC_LANES)]
                plsc.store_scatter(asg_v, [dest], lanes + (a0 + k))

        pltpu.sync_copy(asg_v, asg_hbm)


def _rowmap(eid, rank, start, n_rows):
    n_asg = eid.shape[0]
    chunk = min(ROWMAP_CHUNK, n_asg)
    assert n_asg % chunk == 0 and chunk % SC_LANES == 0 and n_rows % SC_LANES == 0 and start.shape[0] % SC_LANES == 0
    mesh = plsc.VectorSubcoreMesh(core_axis_name="c", subcore_axis_name="s",
                                  num_cores=SC_CORES, num_subcores=SC_SUBCORES)
    i32 = jnp.int32
    return pl.kernel(
        functools.partial(_rowmap_kernel, n_rows=n_rows, n_asg=n_asg, chunk=chunk),
        out_type=jax.ShapeDtypeStruct((n_rows,), i32),
        mesh=mesh,
        scratch_types=[pltpu.VMEM((chunk,), i32), pltpu.VMEM((chunk,), i32),
                       pltpu.VMEM(start.shape, i32), pltpu.VMEM((n_rows,), i32)],
        compiler_params=pltpu.CompilerParams(needs_layout_passes=False),
        name="rowmap",
    )(eid, rank, start)


def _moe_kernel(be_ref, row0_ref, nsub_ref, nreal_ref, asg_ref, h1_hbm, wg_ref, wu_ref, wd_ref, y_hbm,
                xg_ref, xb_ref, yacc_ref, gsem, ssem, *, nb, nf, sub, tok_mask):
    b = pl.program_id(0)
    f = pl.program_id(1)
    nsub = nsub_ref[b]
    max_tiles = xb_ref.shape[0] // sub

    def gather_row(base, r):
        tok = asg_ref[base + r] & tok_mask
        pltpu.make_async_copy(h1_hbm.at[pl.ds(tok, 1)], xg_ref.at[pl.ds(r, 1)], gsem).start()

    def scatter_row(base, r):
        dst = asg_ref[base + r]
        pltpu.make_async_copy(yacc_ref.at[pl.ds(r, 1)], y_hbm.at[pl.ds(dst, 1)], ssem).start()

    def issue_tiles(row_fn, base, n_tiles):
        for s in range(max_tiles):
            @pl.when(s < n_tiles)
            def _(s=s):
                for r in range(s * sub, (s + 1) * sub):
                    row_fn(base, r)

    def issue_range(row_fn, base, lo, hi):
        groups = (hi - lo) // ISSUE_UNROLL

        def body(q, c):
            for k in range(ISSUE_UNROLL):
                row_fn(base, lo + q * ISSUE_UNROLL + k)
            return c

        lax.fori_loop(0, groups, body, 0)

        def tail(r, c):
            row_fn(base, r)
            return c

        lax.fori_loop(lo + groups * ISSUE_UNROLL, hi, tail, 0)

    def wait_rows(n, tile_copy, row_copy):
        def tile(s, c):
            tile_copy.wait()
            return c

        lax.fori_loop(0, n // sub, tile, 0)

        def row(r, c):
            row_copy.wait()
            return c

        lax.fori_loop((n // sub) * sub, n, row, 0)

    def gather_wait(n):
        wait_rows(n, pltpu.make_async_copy(h1_hbm.at[pl.ds(0, sub)], xg_ref.at[pl.ds(0, sub)], gsem),
                  pltpu.make_async_copy(h1_hbm.at[pl.ds(0, 1)], xg_ref.at[pl.ds(0, 1)], gsem))

    def scatter_wait(blk):
        wait_rows(nreal_ref[blk], pltpu.make_async_copy(yacc_ref.at[pl.ds(0, sub)], y_hbm.at[pl.ds(0, sub)], ssem),
                  pltpu.make_async_copy(yacc_ref.at[pl.ds(0, 1)], y_hbm.at[pl.ds(0, 1)], ssem))

    @pl.when((f == 0) & (b == 0))
    def _():
        yacc_ref[...] = jnp.zeros(yacc_ref.shape, F32)

    @pl.when((f == 0) & (b > 0) & (nsub == 0))
    def _():
        scatter_wait(jnp.maximum(b - 1, 0))

    @pl.when(nsub > 0)
    def _():
        @pl.when(f == 0)
        def _():
            @pl.when(b == 0)
            def _():
                issue_range(gather_row, row0_ref[0], 0, nsub * sub)

            gather_wait(nsub * sub)

            def cast(s, c):
                rows = pl.ds(pl.multiple_of(s * sub, sub), sub)
                xb_ref[rows, :] = xg_ref[rows, :].astype(BF16)
                return c

            lax.fori_loop(0, nsub, cast, 0)

            @pl.when(b + 1 < nb)
            def _():
                nxt = jnp.minimum(b + 1, nb - 1)
                issue_tiles(gather_row, row0_ref[nxt], nsub_ref[nxt])

        def compute(m):
            rows = slice(0, m * sub)
            xs = xb_ref[rows, :]
            hb = (jax.nn.silu(_dot(xs, wg_ref[...])) * _dot(xs, wu_ref[...])).astype(BF16)

            @pl.when((f == 0) & (b > 0))
            def _():
                scatter_wait(jnp.maximum(b - 1, 0))

            yp = _dot(hb, wd_ref[...])
            yacc_ref[rows, :] = jnp.where(f == 0, yp, yacc_ref[rows, :] + yp)

        for m in range(1, xb_ref.shape[0] // sub + 1):
            pl.when(nsub == m)(functools.partial(compute, m))

        @pl.when(f == nf - 1)
        def _():
            full = nreal_ref[b] // sub
            issue_tiles(scatter_row, row0_ref[b], full)
            issue_range(scatter_row, row0_ref[b], full * sub, nreal_ref[b])

            @pl.when(b == nb - 1)
            def _():
                scatter_wait(b)


def _moe(be, row0, nsub, nreal, asg, h1, w_gate, w_up, w_down, n_out_rows):
    t, d = h1.shape
    ne, _, de = w_gate.shape
    nb = be.shape[0]
    nf = de // MOE_TF
    assert de % MOE_TF == 0 and t & (t - 1) == 0
    last = nf - 1
    fsel = lambda f, n: jnp.where(n > 0, f, last)
    grid_spec = pltpu.PrefetchScalarGridSpec(
        num_scalar_prefetch=5,
        grid=(nb, nf),
        in_specs=[
            pl.BlockSpec(memory_space=pl.ANY),
            pl.BlockSpec((None, d, MOE_TF), lambda b, f, be, r0, ns, nr, asg: (be[b], 0, fsel(f, ns[b]))),
            pl.BlockSpec((None, d, MOE_TF), lambda b, f, be, r0, ns, nr, asg: (be[b], 0, fsel(f, ns[b]))),
            pl.BlockSpec((None, MOE_TF, d), lambda b, f, be, r0, ns, nr, asg: (be[b], fsel(f, ns[b]), 0)),
        ],
        out_specs=pl.BlockSpec(memory_space=pl.ANY),
        scratch_shapes=[pltpu.VMEM((MOE_ROWS, d), F32),
                        pltpu.VMEM((MOE_ROWS, d), BF16),
                        pltpu.VMEM((MOE_ROWS, d), F32),
                        pltpu.SemaphoreType.DMA,
                        pltpu.SemaphoreType.DMA],
    )
    return pl.pallas_call(
        functools.partial(_moe_kernel, nb=nb, nf=nf, sub=MOE_SUB, tok_mask=t - 1),
        grid_spec=grid_spec,
        out_shape=jax.ShapeDtypeStruct((n_out_rows, d), F32),
        compiler_params=_params("arbitrary", "arbitrary", vmem=MOE_VMEM_LIMIT),
        name="moe",
    )(be, row0, nsub, nreal, asg, h1, w_gate, w_up, w_down)


def _combine_gate_kernel(h1_ref, y0_ref, y1_ref, pk_ref, g_ref, b_ref, wpg_ref, bpg_ref, p_ref, wpe_ref,
                         o_ref, za_ref, zb_ref, ha_ref, hb_ref, *, nj, tn, alpha):
    n = pl.program_id(0)
    j = n % nj
    parity = (n // nj) % 2

    @pl.when(n == 0)
    def _():
        for ref in (za_ref, zb_ref, ha_ref, hb_ref):
            ref[...] = jnp.zeros(ref.shape, ref.dtype)

    def layer_norm_tile(z_ref, h_ref):
        d = nj * tn
        mu = sum(jnp.sum(z_ref[jj], axis=-1, keepdims=True) for jj in range(nj)) / d
        var = sum(jnp.sum(jnp.square(z_ref[jj] - mu), axis=-1, keepdims=True) for jj in range(nj)) / d
        rstd = lax.rsqrt(var + LN_EPS)
        for jj in range(nj):
            cols = slice(jj * tn, (jj + 1) * tn)
            h = (z_ref[jj] - mu) * rstd * g_ref[:, cols] + b_ref[:, cols]
            z_ref[jj] = h
            h_ref[:, cols] = h.astype(BF16)

    def step(zw_ref, hw_ref, zo_ref, ho_ref, first):
        pk = pk_ref[...]
        ffn = pk[:, 2:3] * y0_ref[...] + pk[:, 3:4] * y1_ref[...]
        z = alpha * h1_ref[...] + ffn
        if first:
            zw_ref[0] = z
            layer_norm_tile(zo_ref, ho_ref)
            zg, hg = zw_ref[nj - 1], hw_ref[...]
        else:
            zw_ref[j] = z
            zg, hg = zo_ref[j - 1], ho_ref[...]
        gate = jax.nn.sigmoid(_dot(hg, wpg_ref[...]) + bpg_ref[...])
        ple = _dot(p_ref[...].astype(BF16), wpe_ref[...])
        o_ref[...] = zg + gate * ple

    slots = ((za_ref, ha_ref, zb_ref, hb_ref), (zb_ref, hb_ref, za_ref, ha_ref))
    for par in range(2):
        for first in (True, False):
            cond = (parity == par) & ((j == 0) if first else (j > 0))
            pl.when(cond)(functools.partial(step, *slots[par], first))


def _combine_gate(h1, y2, pk, g, b, w_pg_b, b_pg, p2, w_pe_b, alpha):
    t, d = h1.shape
    pd = p2.shape[1]
    tm, tn = MIX_TM, OUT_TN
    assert t % tm == 0 and d % tn == 0
    ni, nj = t // tm, d // tn
    n_comb = ni * nj
    cmb = lambda n: jnp.minimum(n, n_comb - 1)
    gat = lambda n: jnp.clip(n - nj - 1, 0, n_comb - 1)
    return pl.pallas_call(
        functools.partial(_combine_gate_kernel, nj=nj, tn=tn, alpha=alpha),
        grid=(n_comb + nj + 1,),
        in_specs=[
            pl.BlockSpec((tm, tn), lambda n: (cmb(n) // nj, cmb(n) % nj)),
            pl.BlockSpec((tm, tn), lambda n: (cmb(n) // nj, cmb(n) % nj)),
            pl.BlockSpec((tm, tn), lambda n: (cmb(n) // nj + ni, cmb(n) % nj)),
            pl.BlockSpec((tm, LANE), lambda n: (cmb(n) // nj, 0)),
            pl.BlockSpec((1, d), lambda n: (0, 0)),
            pl.BlockSpec((1, d), lambda n: (0, 0)),
            pl.BlockSpec((d, tn), lambda n: (0, gat(n) % nj)),
            pl.BlockSpec((1, tn), lambda n: (0, gat(n) % nj)),
            pl.BlockSpec((tm, pd), lambda n: (gat(n) // nj, 0)),
            pl.BlockSpec((pd, tn), lambda n: (0, gat(n) % nj)),
        ],
        out_specs=pl.BlockSpec((tm, tn), lambda n: (gat(n) // nj, gat(n) % nj)),
        out_shape=jax.ShapeDtypeStruct((t, d), F32),
        scratch_shapes=[pltpu.VMEM((nj, tm, tn), F32), pltpu.VMEM((nj, tm, tn), F32),
                        pltpu.VMEM((tm, d), BF16), pltpu.VMEM((tm, d), BF16)],
        compiler_params=_params("arbitrary"),
        name="combine_gate",
    )(h1, y2, y2, pk, g, b, w_pg_b, b_pg, p2, w_pe_b)


def _block_table(counts, n_asg):
    ne = counts.shape[0]
    padded = ((counts + MOE_SUB - 1) // MOE_SUB) * MOE_SUB
    pad_end = jnp.cumsum(padded)
    pad_start = pad_end - padded
    nblk = (padded + MOE_ROWS - 1) // MOE_ROWS
    blk_end = jnp.cumsum(nblk)
    blk_start = blk_end - nblk
    nb = ne + -(-n_asg // MOE_ROWS)
    bidx = jnp.arange(nb, dtype=jnp.int32)
    used = bidx < blk_end[-1]
    be = jnp.minimum(jnp.searchsorted(blk_end, jnp.minimum(bidx, blk_end[-1] - 1), side="right"), ne - 1)
    be = be.astype(jnp.int32)
    within = bidx - blk_start[be]
    row0 = pad_start[be] + within * MOE_ROWS
    nsub = jnp.clip((padded[be] - within * MOE_ROWS) // MOE_SUB, 0, MOE_ROWS // MOE_SUB)
    nsub = jnp.where(used, nsub, 0)
    row0 = jnp.where(used, row0, 0)
    nreal = jnp.where(used, jnp.clip(counts[be] - within * MOE_ROWS, 0, MOE_ROWS), 0)
    i32 = lambda a: a.astype(jnp.int32)
    return i32(pad_start), be, i32(row0), i32(nsub), i32(nreal)


def kernel(x, p, w_in, conv_w, sg_ln_g, sg_ln_b, sg_w, sg_b, w_out, ln1_g, ln1_b, w_rg, b_rg, w_re, b_re,
           w_gate, w_up, w_down, ln2_g, ln2_b, w_pg, b_pg, w_pe):
    depth = w_in.shape[0]
    bsz, seq, d = x.shape
    t = bsz * seq
    alpha = (2 * depth) ** 0.25
    n_groups = w_rg.shape[-1]
    ne = w_re.shape[-1]
    per_group = ne // n_groups
    n_asg = t * TOP_K
    n_rows = ((n_asg + ne * (MOE_SUB - 1) + MOE_SUB - 1) // MOE_SUB) * MOE_SUB

    h = x.reshape(t, d)
    for i in range(depth):
        cdim = conv_w.shape[-1]
        nj = cdim // MIX_TN
        w_in_p = _regroup_cols(w_in[i], 5, nj, MIX_TN)
        y, w_out_p, w_pg_b = _mixer(h, w_in_p, conv_w[i], sg_ln_g[i].reshape(1, -1), sg_ln_b[i].reshape(1, -1),
                                    sg_w[i], sg_b[i][:, :, None], w_out[i], w_pg[i], seq)
        wr = jnp.concatenate([w_rg[i], w_re[i], jnp.zeros((d, LANE - n_groups - ne), F32)], axis=1).astype(BF16)
        br = jnp.concatenate([b_rg[i], b_re[i], jnp.zeros((LANE - n_groups - ne,), F32)]).reshape(1, LANE)
        h1, logits = _outproj(y, w_out_p, h, ln1_g[i].reshape(1, d), ln1_b[i].reshape(1, d), wr, br, alpha)
        pk, cnt = _route(logits, n_groups, per_group)
        pad_start, be, row0, nsub, nreal = _block_table(cnt[0, :ne].astype(jnp.int32), n_asg)
        eid = jnp.concatenate([pk[:, 0], pk[:, 1]]).astype(jnp.int32)
        rank = jnp.concatenate([pk[:, 4], pk[:, 5]]).astype(jnp.int32)
        asg = _rowmap(eid, rank, pad_start, n_rows)
        y2 = _moe(be, row0, nsub, nreal, asg, h1, w_gate[i], w_up[i], w_down[i], n_asg)
        h = _combine_gate(h1, y2, pk, ln2_g[i].reshape(1, d), ln2_b[i].reshape(1, d), w_pg_b,
                          b_pg[i].reshape(1, d), p[i].reshape(t, -1), w_pe[i].astype(BF16), alpha)
    return h.reshape(bsz, seq, d)
```

```python
import functools

import jax
import jax.numpy as jnp
from jax import lax
from jax.experimental import pallas as pl
from jax.experimental.pallas import tpu as pltpu
from jax.experimental.pallas import tpu_sc as plsc

F32 = jnp.float32
BF16 = jnp.bfloat16

LANE = 128
SUBLANE = 8
MXU_COL = 256
SC_CORES = 2
SC_SUBCORES = 16
SC_LANES = 16
VMEM_LIMIT = 56 * 1024 * 1024
MOE_VMEM_LIMIT = 60 * 1024 * 1024

CHUNK = 64
SG_BLOCK = 128
LN_EPS = 1e-5
TOP_K = 2

MIX_TM = 512
MIX_TN = MXU_COL
OUT_TN = 512
ROUTE_TM = 512
MOE_SUB = 128
MOE_ROWS = 768
MOE_TF = MXU_COL
ISSUE_UNROLL = 8
GATHER_DMA_QUEUE = 1
ROWMAP_CHUNK = 4096
CAST_TK = 4096
CAST_ROWS = 16

_dot = functools.partial(jnp.dot, preferred_element_type=F32)


def _params(*sem, vmem=VMEM_LIMIT):
    return pltpu.CompilerParams(dimension_semantics=sem, vmem_limit_bytes=vmem)


def _layer_norm(z, g, b):
    mu = jnp.mean(z, axis=-1, keepdims=True)
    zc = z - mu
    var = jnp.mean(zc * zc, axis=-1, keepdims=True)
    return zc * lax.rsqrt(var + LN_EPS) * g + b


def _cast_kernel(w_ref, o_ref):
    o_ref[...] = w_ref[...].astype(o_ref.dtype)


def _regroup_cols(w, groups, nj, tn):
    d = w.shape[0]
    tk = min(CAST_TK, d)
    assert d % tk == 0 and w.shape[1] == groups * nj * tn
    return pl.pallas_call(
        _cast_kernel,
        grid=(nj, groups, d // tk),
        in_specs=[pl.BlockSpec((tk, tn), lambda j, s, k: (k, s * nj + j))],
        out_specs=pl.BlockSpec((tk, tn), lambda j, s, k: (k, j * groups + s)),
        out_shape=jax.ShapeDtypeStruct(w.shape, BF16),
        compiler_params=_params("arbitrary", "arbitrary", "arbitrary"),
        name="regroup_cols",
    )(w)


def _mixer_kernel(x_ref, w_ref, cw_ref, lng_ref, lnb_ref, sgw_ref, sgb_ref, wo_ref, wg_ref,
                  y_ref, wo_out_ref, wg_out_ref,
                  xb_ref, pa_ref, pb_ref, gbuf_ref, carry_ref, *, tm, tn, nj, n_steps, tiles_per_seq):
    n = pl.program_id(0)

    @pl.when(n == 0)
    def _():
        pb_ref[...] = jnp.zeros(pb_ref.shape, F32)
        carry_ref[...] = jnp.zeros(carry_ref.shape, F32)

    @pl.when((n % nj == 0) & (n < n_steps))
    def _():
        xb_ref[...] = x_ref[...].astype(BF16)

    m = jnp.maximum(n - 1, 0)
    jp = m % nj
    first = ((m // nj) % tiles_per_seq) == 0

    def mix(proj_ref):
        b, c, h, u, v = (proj_ref[:, k * tn:(k + 1) * tn] for k in range(5))
        g = c * h
        gbuf_ref[0:SUBLANE, :] = jnp.where(first, 0.0, carry_ref[jp])
        gbuf_ref[SUBLANE:, :] = g
        carry_ref[jp] = g[tm - SUBLANE:, :]
        g1 = gbuf_ref[pl.ds(SUBLANE - 1, tm), :]
        g2 = gbuf_ref[pl.ds(SUBLANE - 2, tm), :]
        cw = cw_ref[...]
        conv = cw[0:1, :] * g2 + cw[1:2, :] * g1 + cw[2:3, :] * g
        y_ref[:, 0:tn] = (b * conv).astype(y_ref.dtype)

        pos_i = lax.broadcasted_iota(jnp.int32, (SG_BLOCK, SG_BLOCK), 0)
        pos_j = lax.broadcasted_iota(jnp.int32, (SG_BLOCK, SG_BLOCK), 1)
        mask = (pos_j // CHUNK) <= (pos_i // CHUNK)
        for hh in range(tn // LANE):
            sl = slice(hh * LANE, (hh + 1) * LANE)
            vn = _layer_norm(jax.nn.gelu(v[:, sl]), lng_ref[:, sl], lnb_ref[:, sl]).astype(BF16)
            gu = jax.nn.gelu(u[:, sl])
            ws = jnp.where(mask, sgw_ref[hh], 0.0).astype(BF16)
            bcol = sgb_ref[hh]
            nblk = tm // SG_BLOCK
            vcat = jnp.concatenate([vn[r * SG_BLOCK:(r + 1) * SG_BLOCK, :] for r in range(nblk)], axis=1)
            sg = _dot(ws, vcat) + bcol
            for r in range(nblk):
                rows = slice(r * SG_BLOCK, (r + 1) * SG_BLOCK)
                y_ref[rows, tn + hh * LANE:tn + (hh + 1) * LANE] = (
                    gu[rows, :] * sg[:, r * LANE:(r + 1) * LANE]).astype(y_ref.dtype)

    def step(store_ref, load_ref):
        store_ref[...] = _dot(xb_ref[...], w_ref[...])
        mix(load_ref)
        wo_out_ref[...] = wo_ref[...].astype(BF16)
        wg_out_ref[...] = wg_ref[...].astype(BF16)

    pl.when(n % 2 == 0)(functools.partial(step, pa_ref, pb_ref))
    pl.when(n % 2 == 1)(functools.partial(step, pb_ref, pa_ref))


def _mixer(x2, w_in_p, conv_w, lng, lnb, sg_w, sg_bcol, w_out, w_pg, seq):
    t, d = x2.shape
    cdim = conv_w.shape[1]
    tm, tn = MIX_TM, MIX_TN
    nj = cdim // tn
    assert seq % tm == 0 and t % tm == 0 and cdim % tn == 0 and w_in_p.shape[1] == 5 * cdim
    hp = tn // LANE
    n_steps = (t // tm) * nj
    cur = lambda n: jnp.minimum(n, n_steps - 1)
    prv = lambda n: jnp.maximum(n - 1, 0)
    wrows = w_out.shape[0]
    cb = max(CAST_ROWS, -(-wrows // n_steps // CAST_ROWS) * CAST_ROWS)
    nblk = wrows // cb
    spb = n_steps // nblk
    bpt = tn // cb
    assert w_pg.shape == w_out.shape == (2 * cdim, d) and wrows % cb == 0 and n_steps % nblk == 0 and tn % cb == 0
    blk = lambda n: jnp.minimum(n // spb, nblk - 1)

    def wo_src(n):
        k = blk(n)
        tile, sub = k // bpt, k % bpt
        return ((tile % 2) * nj + tile // 2) * bpt + sub, 0

    return pl.pallas_call(
        functools.partial(_mixer_kernel, tm=tm, tn=tn, nj=nj, n_steps=n_steps, tiles_per_seq=seq // tm),
        grid=(n_steps + 1,),
        in_specs=[
            pl.BlockSpec((tm, d), lambda n: (cur(n) // nj, 0)),
            pl.BlockSpec((d, 5 * tn), lambda n: (0, cur(n) % nj)),
            pl.BlockSpec((conv_w.shape[0], tn), lambda n: (0, prv(n) % nj)),
            pl.BlockSpec((1, tn), lambda n: (0, prv(n) % nj)),
            pl.BlockSpec((1, tn), lambda n: (0, prv(n) % nj)),
            pl.BlockSpec((hp, SG_BLOCK, SG_BLOCK), lambda n: (prv(n) % nj, 0, 0)),
            pl.BlockSpec((hp, SG_BLOCK, 1), lambda n: (prv(n) % nj, 0, 0)),
            pl.BlockSpec((cb, d), wo_src),
            pl.BlockSpec((cb, d), lambda n: (blk(n), 0)),
        ],
        out_specs=[pl.BlockSpec((tm, 2 * tn), lambda n: (prv(n) // nj, prv(n) % nj)),
                   pl.BlockSpec((cb, d), lambda n: (blk(n), 0)),
                   pl.BlockSpec((cb, d), lambda n: (blk(n), 0))],
        out_shape=[jax.ShapeDtypeStruct((t, 2 * cdim), BF16),
                   jax.ShapeDtypeStruct(w_out.shape, BF16),
                   jax.ShapeDtypeStruct(w_pg.shape, BF16)],
        scratch_shapes=[pltpu.VMEM((tm, d), BF16),
                        pltpu.VMEM((tm, 5 * tn), F32),
                        pltpu.VMEM((tm, 5 * tn), F32),
                        pltpu.VMEM((tm + SUBLANE, tn), F32),
                        pltpu.VMEM((nj, SUBLANE, tn), F32)],
        compiler_params=_params("arbitrary"),
        name="mixer",
    )(x2, w_in_p, conv_w, lng, lnb, sg_w, sg_bcol, w_out, w_pg)


def _outproj_kernel(y_ref, w_ref, x_ref, g_ref, b_ref, wr_ref, br_ref, h1_ref, lg_ref, z_ref, *, tn, nj, alpha):
    j = pl.program_id(1)
    z_ref[j] = alpha * x_ref[...] + _dot(y_ref[...], w_ref[...])

    @pl.when(j == nj - 1)
    def _():
        d = nj * tn
        mu = sum(jnp.sum(z_ref[jj], axis=-1, keepdims=True) for jj in range(nj)) / d
        var = sum(jnp.sum(jnp.square(z_ref[jj] - mu), axis=-1, keepdims=True) for jj in range(nj)) / d
        rstd = lax.rsqrt(var + LN_EPS)
        logits = br_ref[...]
        for jj in range(nj):
            cols = slice(jj * tn, (jj + 1) * tn)
            h = (z_ref[jj] - mu) * rstd * g_ref[:, cols] + b_ref[:, cols]
            h1_ref[:, cols] = h
            logits = logits + _dot(h.astype(BF16), wr_ref[cols, :])
        lg_ref[...] = logits


def _outproj(y, w_out_p, x2, g, b, wr, br, alpha):
    t, d = x2.shape
    kdim = y.shape[1]
    tm, tn = MIX_TM, OUT_TN
    nj = d // tn
    assert t % tm == 0 and d % tn == 0 and w_out_p.shape[0] == kdim
    return pl.pallas_call(
        functools.partial(_outproj_kernel, tn=tn, nj=nj, alpha=alpha),
        grid=(t // tm, nj),
        in_specs=[
            pl.BlockSpec((tm, kdim), lambda i, j: (i, 0)),
            pl.BlockSpec((kdim, tn), lambda i, j: (0, j)),
            pl.BlockSpec((tm, tn), lambda i, j: (i, j)),
            pl.BlockSpec((1, d), lambda i, j: (0, 0)),
            pl.BlockSpec((1, d), lambda i, j: (0, 0)),
            pl.BlockSpec((d, LANE), lambda i, j: (0, 0)),
            pl.BlockSpec((1, LANE), lambda i, j: (0, 0)),
        ],
        out_specs=[pl.BlockSpec((tm, d), lambda i, j: (i, 0)),
                   pl.BlockSpec((tm, LANE), lambda i, j: (i, 0))],
        out_shape=[jax.ShapeDtypeStruct((t, d), F32), jax.ShapeDtypeStruct((t, LANE), F32)],
        scratch_shapes=[pltpu.VMEM((nj, tm, tn), F32)],
        compiler_params=_params("arbitrary", "arbitrary"),
        name="outproj",
    )(y, w_out_p, x2, g, b, wr, br)


def _route_kernel(lg_ref, pk_ref, cnt_ref, run_ref, *, n_groups, per_group):
    i = pl.program_id(0)

    @pl.when(i == 0)
    def _():
        run_ref[...] = jnp.zeros_like(run_ref)

    l = lg_ref[...]
    tm = l.shape[0]
    lane = lax.broadcasted_iota(jnp.int32, l.shape, 1)
    neg = jnp.float32(-jnp.inf)
    gmask = lane < n_groups
    gl = jnp.where(gmask, l, neg)
    gmax = jnp.max(gl, axis=-1, keepdims=True)
    gi = jnp.min(jnp.where(gl == gmax, lane, LANE), axis=-1, keepdims=True)
    gp = 1.0 / jnp.sum(jnp.where(gmask, jnp.exp(l - gmax), 0.0), axis=-1, keepdims=True)
    elane = lane - n_groups
    emask = (elane >= 0) & (elane // per_group == gi)
    el = jnp.where(emask, l, neg)
    m1 = jnp.max(el, axis=-1, keepdims=True)
    i1 = jnp.min(jnp.where(el == m1, lane, LANE), axis=-1, keepdims=True)
    el2 = jnp.where(lane == i1, neg, el)
    m2 = jnp.max(el2, axis=-1, keepdims=True)
    i2 = jnp.min(jnp.where(el2 == m2, lane, LANE), axis=-1, keepdims=True)
    t2 = jnp.exp(m2 - m1)
    w0 = gp / (1.0 + t2)
    w1 = gp * t2 / (1.0 + t2)
    e0 = i1 - n_groups
    e1 = i2 - n_groups
    oh0 = lane == e0
    oh1 = lane == e1
    oh = (oh0 | oh1).astype(BF16)
    row = lax.broadcasted_iota(jnp.int32, (tm, tm), 0)
    col = lax.broadcasted_iota(jnp.int32, (tm, tm), 1)
    tri = (row > col).astype(BF16)
    before = run_ref[...] + _dot(tri, oh)
    r0 = jnp.sum(jnp.where(oh0, before, 0.0), axis=-1, keepdims=True)
    r1 = jnp.sum(jnp.where(oh1, before, 0.0), axis=-1, keepdims=True)
    total = run_ref[...] + jnp.sum(oh.astype(F32), axis=0, keepdims=True)
    run_ref[...] = total
    cnt_ref[...] = total
    vals = (e0.astype(F32), e1.astype(F32), w0, w1, r0, r1)
    pk = jnp.zeros(l.shape, F32)
    for k, val in enumerate(vals):
        pk = jnp.where(lane == k, val, pk)
    pk_ref[...] = pk


def _route(logits, n_groups, per_group):
    t = logits.shape[0]
    tm = ROUTE_TM
    assert t % tm == 0 and n_groups * (1 + per_group) <= LANE
    return pl.pallas_call(
        functools.partial(_route_kernel, n_groups=n_groups, per_group=per_group),
        grid=(t // tm,),
        in_specs=[pl.BlockSpec((tm, LANE), lambda i: (i, 0))],
        out_specs=[pl.BlockSpec((tm, LANE), lambda i: (i, 0)),
                   pl.BlockSpec((1, LANE), lambda i: (0, 0))],
        out_shape=[jax.ShapeDtypeStruct((t, LANE), F32), jax.ShapeDtypeStruct((1, LANE), F32)],
        scratch_shapes=[pltpu.VMEM((1, LANE), F32)],
        compiler_params=_params("arbitrary"),
        name="route",
    )(logits)


def _rowmap_kernel(eid_hbm, rank_hbm, start_hbm, asg_hbm, eid_v, rank_v, start_v, asg_v, *, n_rows, n_asg, chunk):
    on_first = (lax.axis_index("c") == 0) & (lax.axis_index("s") == 0)

    @pl.when(on_first)
    def _():
        pltpu.sync_copy(start_hbm, start_v)
        zeros = jnp.zeros((SC_LANES,), jnp.int32)

        @pl.loop(0, n_rows, step=SC_LANES)
        def _(r):
            asg_v[pl.ds(r, SC_LANES)] = zeros

        lanes = lax.iota(jnp.int32, SC_LANES)

        @pl.loop(0, n_asg, step=chunk)
        def _(a0):
            pltpu.sync_copy(eid_hbm.at[pl.ds(a0, chunk)], eid_v)
            pltpu.sync_copy(rank_hbm.at[pl.ds(a0, chunk)], rank_v)

            @pl.loop(0, chunk, step=SC_LANES)
            def _(k):
                dest = plsc.load_gather(start_v, [eid_v[pl.ds(k, SC_LANES)]]) + rank_v[pl.ds(k, SC_LANES)]
                plsc.store_scatter(asg_v, [dest], lanes + (a0 + k))

        pltpu.sync_copy(asg_v, asg_hbm)


def _rowmap(eid, rank, start, n_rows):
    n_asg = eid.shape[0]
    chunk = min(ROWMAP_CHUNK, n_asg)
    assert n_asg % chunk == 0 and chunk % SC_LANES == 0 and n_rows % SC_LANES == 0 and start.shape[0] % SC_LANES == 0
    mesh = plsc.VectorSubcoreMesh(core_axis_name="c", subcore_axis_name="s",
                                  num_cores=SC_CORES, num_subcores=SC_SUBCORES)
    i32 = jnp.int32
    return pl.kernel(
        functools.partial(_rowmap_kernel, n_rows=n_rows, n_asg=n_asg, chunk=chunk),
        out_type=jax.ShapeDtypeStruct((n_rows,), i32),
        mesh=mesh,
        scratch_types=[pltpu.VMEM((chunk,), i32), pltpu.VMEM((chunk,), i32),
                       pltpu.VMEM(start.shape, i32), pltpu.VMEM((n_rows,), i32)],
        compiler_params=pltpu.CompilerParams(needs_layout_passes=False),
        name="rowmap",
    )(eid, rank, start)


def _moe_kernel(be_ref, row0_ref, nsub_ref, nreal_ref, asg_ref, h1_hbm, wg_ref, wu_ref, wd_ref, y_hbm,
                xg_ref, xb_ref, yacc_ref, gsem, ssem, *, nb, nf, sub, tok_mask):
    b = pl.program_id(0)
    f = pl.program_id(1)
    nsub = nsub_ref[b]
    max_tiles = xb_ref.shape[0] // sub

    def gather_row(base, r):
        tok = asg_ref[base + r] & tok_mask
        pltpu.make_async_copy(h1_hbm.at[pl.ds(tok, 1)], xg_ref.at[pl.ds(r, 1)], gsem).start(priority=GATHER_DMA_QUEUE)

    def scatter_row(base, r):
        dst = asg_ref[base + r]
        pltpu.make_async_copy(yacc_ref.at[pl.ds(r, 1)], y_hbm.at[pl.ds(dst, 1)], ssem).start()

    def issue_tiles(row_fn, base, n_tiles):
        for s in range(max_tiles):
            @pl.when(s < n_tiles)
            def _(s=s):
                for r in range(s * sub, (s + 1) * sub):
                    row_fn(base, r)

    def issue_range(row_fn, base, lo, hi):
        groups = (hi - lo) // ISSUE_UNROLL

        def body(q, c):
            for k in range(ISSUE_UNROLL):
                row_fn(base, lo + q * ISSUE_UNROLL + k)
            return c

        lax.fori_loop(0, groups, body, 0)

        def tail(r, c):
            row_fn(base, r)
            return c

        lax.fori_loop(lo + groups * ISSUE_UNROLL, hi, tail, 0)

    def wait_rows(n, tile_copy, row_copy):
        def tile(s, c):
            tile_copy.wait()
            return c

        lax.fori_loop(0, n // sub, tile, 0)

        def row(r, c):
            row_copy.wait()
            return c

        lax.fori_loop((n // sub) * sub, n, row, 0)

    def gather_wait(n):
        wait_rows(n, pltpu.make_async_copy(h1_hbm.at[pl.ds(0, sub)], xg_ref.at[pl.ds(0, sub)], gsem),
                  pltpu.make_async_copy(h1_hbm.at[pl.ds(0, 1)], xg_ref.at[pl.ds(0, 1)], gsem))

    def scatter_wait(blk):
        wait_rows(nreal_ref[blk], pltpu.make_async_copy(yacc_ref.at[pl.ds(0, sub)], y_hbm.at[pl.ds(0, sub)], ssem),
                  pltpu.make_async_copy(yacc_ref.at[pl.ds(0, 1)], y_hbm.at[pl.ds(0, 1)], ssem))

    @pl.when((f == 0) & (b == 0))
    def _():
        yacc_ref[...] = jnp.zeros(yacc_ref.shape, F32)

    @pl.when((f == 0) & (b > 0) & (nsub == 0))
    def _():
        scatter_wait(jnp.maximum(b - 1, 0))

    @pl.when(nsub > 0)
    def _():
        @pl.when(f == 0)
        def _():
            @pl.when(b == 0)
            def _():
                issue_range(gather_row, row0_ref[0], 0, nsub * sub)

            gather_wait(nsub * sub)

            def cast(s, c):
                rows = pl.ds(pl.multiple_of(s * sub, sub), sub)
                xb_ref[rows, :] = xg_ref[rows, :].astype(BF16)
                return c

            lax.fori_loop(0, nsub, cast, 0)

            @pl.when(b + 1 < nb)
            def _():
                nxt = jnp.minimum(b + 1, nb - 1)
                issue_tiles(gather_row, row0_ref[nxt], nsub_ref[nxt])

        def compute(m):
            rows = slice(0, m * sub)
            xs = xb_ref[rows, :]
            hb = (jax.nn.silu(_dot(xs, wg_ref[...])) * _dot(xs, wu_ref[...])).astype(BF16)

            @pl.when((f == 0) & (b > 0))
            def _():
                scatter_wait(jnp.maximum(b - 1, 0))

            yp = _dot(hb, wd_ref[...])
            yacc_ref[rows, :] = jnp.where(f == 0, yp, yacc_ref[rows, :] + yp)

        for m in range(1, xb_ref.shape[0] // sub + 1):
            pl.when(nsub == m)(functools.partial(compute, m))

        @pl.when(f == nf - 1)
        def _():
            full = nreal_ref[b] // sub
            issue_tiles(scatter_row, row0_ref[b], full)
            issue_range(scatter_row, row0_ref[b], full * sub, nreal_ref[b])

            @pl.when(b == nb - 1)
            def _():
                scatter_wait(b)


def _moe(be, row0, nsub, nreal, asg, h1, w_gate, w_up, w_down, n_out_rows):
    t, d = h1.shape
    ne, _, de = w_gate.shape
    nb = be.shape[0]
    nf = de // MOE_TF
    assert de % MOE_TF == 0 and t & (t - 1) == 0
    last = nf - 1
    fsel = lambda f, n: jnp.where(n > 0, f, last)
    grid_spec = pltpu.PrefetchScalarGridSpec(
        num_scalar_prefetch=5,
        grid=(nb, nf),
        in_specs=[
            pl.BlockSpec(memory_space=pl.ANY),
            pl.BlockSpec((None, d, MOE_TF), lambda b, f, be, r0, ns, nr, asg: (be[b], 0, fsel(f, ns[b]))),
            pl.BlockSpec((None, d, MOE_TF), lambda b, f, be, r0, ns, nr, asg: (be[b], 0, fsel(f, ns[b]))),
            pl.BlockSpec((None, MOE_TF, d), lambda b, f, be, r0, ns, nr, asg: (be[b], fsel(f, ns[b]), 0)),
        ],
        out_specs=pl.BlockSpec(memory_space=pl.ANY),
        scratch_shapes=[pltpu.VMEM((MOE_ROWS, d), F32),
                        pltpu.VMEM((MOE_ROWS, d), BF16),
                        pltpu.VMEM((MOE_ROWS, d), F32),
                        pltpu.SemaphoreType.DMA,
                        pltpu.SemaphoreType.DMA],
    )
    return pl.pallas_call(
        functools.partial(_moe_kernel, nb=nb, nf=nf, sub=MOE_SUB, tok_mask=t - 1),
        grid_spec=grid_spec,
        out_shape=jax.ShapeDtypeStruct((n_out_rows, d), F32),
        compiler_params=_params("arbitrary", "arbitrary", vmem=MOE_VMEM_LIMIT),
        name="moe",
    )(be, row0, nsub, nreal, asg, h1, w_gate, w_up, w_down)


def _combine_gate_kernel(h1_ref, y0_ref, y1_ref, pk_ref, g_ref, b_ref, wpg_ref, bpg_ref, p_ref, wpe_ref,
                         o_ref, za_ref, zb_ref, ha_ref, hb_ref, *, nj, tn, alpha):
    n = pl.program_id(0)
    j = n % nj
    parity = (n // nj) % 2

    @pl.when(n == 0)
    def _():
        for ref in (za_ref, zb_ref, ha_ref, hb_ref):
            ref[...] = jnp.zeros(ref.shape, ref.dtype)

    def layer_norm_tile(z_ref, h_ref):
        d = nj * tn
        mu = sum(jnp.sum(z_ref[jj], axis=-1, keepdims=True) for jj in range(nj)) / d
        var = sum(jnp.sum(jnp.square(z_ref[jj] - mu), axis=-1, keepdims=True) for jj in range(nj)) / d
        rstd = lax.rsqrt(var + LN_EPS)
        for jj in range(nj):
            cols = slice(jj * tn, (jj + 1) * tn)
            h = (z_ref[jj] - mu) * rstd * g_ref[:, cols] + b_ref[:, cols]
            z_ref[jj] = h
            h_ref[:, cols] = h.astype(BF16)

    def step(zw_ref, hw_ref, zo_ref, ho_ref, first):
        pk = pk_ref[...]
        ffn = pk[:, 2:3] * y0_ref[...] + pk[:, 3:4] * y1_ref[...]
        z = alpha * h1_ref[...] + ffn
        if first:
            zw_ref[0] = z
            layer_norm_tile(zo_ref, ho_ref)
            zg, hg = zw_ref[nj - 1], hw_ref[...]
        else:
            zw_ref[j] = z
            zg, hg = zo_ref[j - 1], ho_ref[...]
        gate = jax.nn.sigmoid(_dot(hg, wpg_ref[...]) + bpg_ref[...])
        ple = _dot(p_ref[...].astype(BF16), wpe_ref[...])
        o_ref[...] = zg + gate * ple

    slots = ((za_ref, ha_ref, zb_ref, hb_ref), (zb_ref, hb_ref, za_ref, ha_ref))
    for par in range(2):
        for first in (True, False):
            cond = (parity == par) & ((j == 0) if first else (j > 0))
            pl.when(cond)(functools.partial(step, *slots[par], first))


def _combine_gate(h1, y2, pk, g, b, w_pg_b, b_pg, p2, w_pe_b, alpha):
    t, d = h1.shape
    pd = p2.shape[1]
    tm, tn = MIX_TM, OUT_TN
    assert t % tm == 0 and d % tn == 0
    ni, nj = t // tm, d // tn
    n_comb = ni * nj
    cmb = lambda n: jnp.minimum(n, n_comb - 1)
    gat = lambda n: jnp.clip(n - nj - 1, 0, n_comb - 1)
    return pl.pallas_call(
        functools.partial(_combine_gate_kernel, nj=nj, tn=tn, alpha=alpha),
        grid=(n_comb + nj + 1,),
        in_specs=[
            pl.BlockSpec((tm, tn), lambda n: (cmb(n) // nj, cmb(n) % nj)),
            pl.BlockSpec((tm, tn), lambda n: (cmb(n) // nj, cmb(n) % nj)),
            pl.BlockSpec((tm, tn), lambda n: (cmb(n) // nj + ni, cmb(n) % nj)),
            pl.BlockSpec((tm, LANE), lambda n: (cmb(n) // nj, 0)),
            pl.BlockSpec((1, d), lambda n: (0, 0)),
            pl.BlockSpec((1, d), lambda n: (0, 0)),
            pl.BlockSpec((d, tn), lambda n: (0, gat(n) % nj)),
            pl.BlockSpec((1, tn), lambda n: (0, gat(n) % nj)),
            pl.BlockSpec((tm, pd), lambda n: (gat(n) // nj, 0)),
            pl.BlockSpec((pd, tn), lambda n: (0, gat(n) % nj)),
        ],
        out_specs=pl.BlockSpec((tm, tn), lambda n: (gat(n) // nj, gat(n) % nj)),
        out_shape=jax.ShapeDtypeStruct((t, d), F32),
        scratch_shapes=[pltpu.VMEM((nj, tm, tn), F32), pltpu.VMEM((nj, tm, tn), F32),
                        pltpu.VMEM((tm, d), BF16), pltpu.VMEM((tm, d), BF16)],
        compiler_params=_params("arbitrary"),
        name="combine_gate",
    )(h1, y2, y2, pk, g, b, w_pg_b, b_pg, p2, w_pe_b)


def _block_table(counts, n_asg):
    ne = counts.shape[0]
    padded = ((counts + MOE_SUB - 1) // MOE_SUB) * MOE_SUB
    pad_end = jnp.cumsum(padded)
    pad_start = pad_end - padded
    nblk = (padded + MOE_ROWS - 1) // MOE_ROWS
    blk_end = jnp.cumsum(nblk)
    blk_start = blk_end - nblk
    nb = ne + -(-n_asg // MOE_ROWS)
    bidx = jnp.arange(nb, dtype=jnp.int32)
    used = bidx < blk_end[-1]
    be = jnp.minimum(jnp.searchsorted(blk_end, jnp.minimum(bidx, blk_end[-1] - 1), side="right"), ne - 1)
    be = be.astype(jnp.int32)
    within = bidx - blk_start[be]
    row0 = pad_start[be] + within * MOE_ROWS
    nsub = jnp.clip((padded[be] - within * MOE_ROWS) // MOE_SUB, 0, MOE_ROWS // MOE_SUB)
    nsub = jnp.where(used, nsub, 0)
    row0 = jnp.where(used, row0, 0)
    nreal = jnp.where(used, jnp.clip(counts[be] - within * MOE_ROWS, 0, MOE_ROWS), 0)
    i32 = lambda a: a.astype(jnp.int32)
    return i32(pad_start), be, i32(row0), i32(nsub), i32(nreal)


def kernel(x, p, w_in, conv_w, sg_ln_g, sg_ln_b, sg_w, sg_b, w_out, ln1_g, ln1_b, w_rg, b_rg, w_re, b_re,
           w_gate, w_up, w_down, ln2_g, ln2_b, w_pg, b_pg, w_pe):
    depth = w_in.shape[0]
    bsz, seq, d = x.shape
    t = bsz * seq
    alpha = (2 * depth) ** 0.25
    n_groups = w_rg.shape[-1]
    ne = w_re.shape[-1]
    per_group = ne // n_groups
    n_asg = t * TOP_K
    n_rows = ((n_asg + ne * (MOE_SUB - 1) + MOE_SUB - 1) // MOE_SUB) * MOE_SUB

    h = x.reshape(t, d)
    for i in range(depth):
        cdim = conv_w.shape[-1]
        nj = cdim // MIX_TN
        w_in_p = _regroup_cols(w_in[i], 5, nj, MIX_TN)
        y, w_out_p, w_pg_b = _mixer(h, w_in_p, conv_w[i], sg_ln_g[i].reshape(1, -1), sg_ln_b[i].reshape(1, -1),
                                    sg_w[i], sg_b[i][:, :, None], w_out[i], w_pg[i], seq)
        wr = jnp.concatenate([w_rg[i], w_re[i], jnp.zeros((d, LANE - n_groups - ne), F32)], axis=1).astype(BF16)
        br = jnp.concatenate([b_rg[i], b_re[i], jnp.zeros((LANE - n_groups - ne,), F32)]).reshape(1, LANE)
        h1, logits = _outproj(y, w_out_p, h, ln1_g[i].reshape(1, d), ln1_b[i].reshape(1, d), wr, br, alpha)
        pk, cnt = _route(logits, n_groups, per_group)
        pad_start, be, row0, nsub, nreal = _block_table(cnt[0, :ne].astype(jnp.int32), n_asg)
        eid = jnp.concatenate([pk[:, 0], pk[:, 1]]).astype(jnp.int32)
        rank = jnp.concatenate([pk[:, 4], pk[:, 5]]).astype(jnp.int32)
        asg = _rowmap(eid, rank, pad_start, n_rows)
        y2 = _moe(be, row0, nsub, nreal, asg, h1, w_gate[i], w_up[i], w_down[i], n_asg)
        h = _combine_gate(h1, y2, pk, ln2_g[i].reshape(1, d), ln2_b[i].reshape(1, d), w_pg_b,
                          b_pg[i].reshape(1, d), p[i].reshape(t, -1), w_pe[i].astype(BF16), alpha)
    return h.reshape(bsz, seq, d)
```

```python
import functools

import jax
import jax.numpy as jnp
from jax import lax
from jax.experimental import pallas as pl
from jax.experimental.pallas import tpu as pltpu
from jax.experimental.pallas import tpu_sc as plsc

F32 = jnp.float32
BF16 = jnp.bfloat16

LANE = 128
SUBLANE = 8
MXU_COL = 256
SC_CORES = 2
SC_SUBCORES = 16
SC_LANES = 16
VMEM_LIMIT = 56 * 1024 * 1024
MOE_VMEM_LIMIT = 60 * 1024 * 1024

CHUNK = 64
SG_BLOCK = 128
LN_EPS = 1e-5
TOP_K = 2

MIX_TM = 512
MIX_TN = MXU_COL
OUT_TN = 512
ROUTE_TM = 512
MOE_SUB = 128
MOE_ROWS = 768
MOE_TF = MXU_COL
ISSUE_UNROLL = 8
ROWMAP_CHUNK = 4096
SC_UNROLL = 4
CAST_TK = 4096
CAST_ROWS = 16

_dot = functools.partial(jnp.dot, preferred_element_type=F32)


def _params(*sem, vmem=VMEM_LIMIT):
    return pltpu.CompilerParams(dimension_semantics=sem, vmem_limit_bytes=vmem)


def _layer_norm(z, g, b):
    mu = jnp.mean(z, axis=-1, keepdims=True)
    zc = z - mu
    var = jnp.mean(zc * zc, axis=-1, keepdims=True)
    return zc * lax.rsqrt(var + LN_EPS) * g + b


def _cast_kernel(w_ref, o_ref):
    o_ref[...] = w_ref[...].astype(o_ref.dtype)


def _regroup_cols(w, groups, nj, tn):
    d = w.shape[0]
    tk = min(CAST_TK, d)
    assert d % tk == 0 and w.shape[1] == groups * nj * tn
    return pl.pallas_call(
        _cast_kernel,
        grid=(nj, groups, d // tk),
        in_specs=[pl.BlockSpec((tk, tn), lambda j, s, k: (k, s * nj + j))],
        out_specs=pl.BlockSpec((tk, tn), lambda j, s, k: (k, j * groups + s)),
        out_shape=jax.ShapeDtypeStruct(w.shape, BF16),
        compiler_params=_params("arbitrary", "arbitrary", "arbitrary"),
        name="regroup_cols",
    )(w)


def _mixer_kernel(x_ref, w_ref, cw_ref, lng_ref, lnb_ref, sgw_ref, sgb_ref, wo_ref, wg_ref,
                  y_ref, wo_out_ref, wg_out_ref,
                  xb_ref, pa_ref, pb_ref, gbuf_ref, carry_ref, *, tm, tn, nj, n_steps, tiles_per_seq):
    n = pl.program_id(0)

    @pl.when(n == 0)
    def _():
        pb_ref[...] = jnp.zeros(pb_ref.shape, F32)
        carry_ref[...] = jnp.zeros(carry_ref.shape, F32)

    @pl.when((n % nj == 0) & (n < n_steps))
    def _():
        xb_ref[...] = x_ref[...].astype(BF16)

    m = jnp.maximum(n - 1, 0)
    jp = m % nj
    first = ((m // nj) % tiles_per_seq) == 0

    def mix(proj_ref):
        b, c, h, u, v = (proj_ref[:, k * tn:(k + 1) * tn] for k in range(5))
        g = c * h
        gbuf_ref[0:SUBLANE, :] = jnp.where(first, 0.0, carry_ref[jp])
        gbuf_ref[SUBLANE:, :] = g
        carry_ref[jp] = g[tm - SUBLANE:, :]
        g1 = gbuf_ref[pl.ds(SUBLANE - 1, tm), :]
        g2 = gbuf_ref[pl.ds(SUBLANE - 2, tm), :]
        cw = cw_ref[...]
        conv = cw[0:1, :] * g2 + cw[1:2, :] * g1 + cw[2:3, :] * g
        y_ref[:, 0:tn] = (b * conv).astype(y_ref.dtype)

        pos_i = lax.broadcasted_iota(jnp.int32, (SG_BLOCK, SG_BLOCK), 0)
        pos_j = lax.broadcasted_iota(jnp.int32, (SG_BLOCK, SG_BLOCK), 1)
        mask = (pos_j // CHUNK) <= (pos_i // CHUNK)
        for hh in range(tn // LANE):
            sl = slice(hh * LANE, (hh + 1) * LANE)
            vn = _layer_norm(jax.nn.gelu(v[:, sl]), lng_ref[:, sl], lnb_ref[:, sl]).astype(BF16)
            gu = jax.nn.gelu(u[:, sl])
            ws = jnp.where(mask, sgw_ref[hh], 0.0).astype(BF16)
            bcol = sgb_ref[hh]
            nblk = tm // SG_BLOCK
            vcat = jnp.concatenate([vn[r * SG_BLOCK:(r + 1) * SG_BLOCK, :] for r in range(nblk)], axis=1)
            sg = _dot(ws, vcat) + bcol
            for r in range(nblk):
                rows = slice(r * SG_BLOCK, (r + 1) * SG_BLOCK)
                y_ref[rows, tn + hh * LANE:tn + (hh + 1) * LANE] = (
                    gu[rows, :] * sg[:, r * LANE:(r + 1) * LANE]).astype(y_ref.dtype)

    def step(store_ref, load_ref):
        store_ref[...] = _dot(xb_ref[...], w_ref[...])
        mix(load_ref)
        wo_out_ref[...] = wo_ref[...].astype(BF16)
        wg_out_ref[...] = wg_ref[...].astype(BF16)

    pl.when(n % 2 == 0)(functools.partial(step, pa_ref, pb_ref))
    pl.when(n % 2 == 1)(functools.partial(step, pb_ref, pa_ref))


def _mixer(x2, w_in_p, conv_w, lng, lnb, sg_w, sg_bcol, w_out, w_pg, seq):
    t, d = x2.shape
    cdim = conv_w.shape[1]
    tm, tn = MIX_TM, MIX_TN
    nj = cdim // tn
    assert seq % tm == 0 and t % tm == 0 and cdim % tn == 0 and w_in_p.shape[1] == 5 * cdim
    hp = tn // LANE
    n_steps = (t // tm) * nj
    cur = lambda n: jnp.minimum(n, n_steps - 1)
    prv = lambda n: jnp.maximum(n - 1, 0)
    wrows = w_out.shape[0]
    cb = max(CAST_ROWS, -(-wrows // n_steps // CAST_ROWS) * CAST_ROWS)
    nblk = wrows // cb
    spb = n_steps // nblk
    bpt = tn // cb
    assert w_pg.shape == w_out.shape == (2 * cdim, d) and wrows % cb == 0 and n_steps % nblk == 0 and tn % cb == 0
    blk = lambda n: jnp.minimum(n // spb, nblk - 1)

    def wo_src(n):
        k = blk(n)
        tile, sub = k // bpt, k % bpt
        return ((tile % 2) * nj + tile // 2) * bpt + sub, 0

    return pl.pallas_call(
        functools.partial(_mixer_kernel, tm=tm, tn=tn, nj=nj, n_steps=n_steps, tiles_per_seq=seq // tm),
        grid=(n_steps + 1,),
        in_specs=[
            pl.BlockSpec((tm, d), lambda n: (cur(n) // nj, 0)),
            pl.BlockSpec((d, 5 * tn), lambda n: (0, cur(n) % nj)),
            pl.BlockSpec((conv_w.shape[0], tn), lambda n: (0, prv(n) % nj)),
            pl.BlockSpec((1, tn), lambda n: (0, prv(n) % nj)),
            pl.BlockSpec((1, tn), lambda n: (0, prv(n) % nj)),
            pl.BlockSpec((hp, SG_BLOCK, SG_BLOCK), lambda n: (prv(n) % nj, 0, 0)),
            pl.BlockSpec((hp, SG_BLOCK, 1), lambda n: (prv(n) % nj, 0, 0)),
            pl.BlockSpec((cb, d), wo_src),
            pl.BlockSpec((cb, d), lambda n: (blk(n), 0)),
        ],
        out_specs=[pl.BlockSpec((tm, 2 * tn), lambda n: (prv(n) // nj, prv(n) % nj)),
                   pl.BlockSpec((cb, d), lambda n: (blk(n), 0)),
                   pl.BlockSpec((cb, d), lambda n: (blk(n), 0))],
        out_shape=[jax.ShapeDtypeStruct((t, 2 * cdim), BF16),
                   jax.ShapeDtypeStruct(w_out.shape, BF16),
                   jax.ShapeDtypeStruct(w_pg.shape, BF16)],
        scratch_shapes=[pltpu.VMEM((tm, d), BF16),
                        pltpu.VMEM((tm, 5 * tn), F32),
                        pltpu.VMEM((tm, 5 * tn), F32),
                        pltpu.VMEM((tm + SUBLANE, tn), F32),
                        pltpu.VMEM((nj, SUBLANE, tn), F32)],
        compiler_params=_params("arbitrary"),
        name="mixer",
    )(x2, w_in_p, conv_w, lng, lnb, sg_w, sg_bcol, w_out, w_pg)


def _outproj_kernel(y_ref, w_ref, x_ref, g_ref, b_ref, wr_ref, br_ref, h1_ref, lg_ref, z_ref, *, tn, nj, alpha):
    j = pl.program_id(1)
    z_ref[j] = alpha * x_ref[...] + _dot(y_ref[...], w_ref[...])

    @pl.when(j == nj - 1)
    def _():
        d = nj * tn
        mu = sum(jnp.sum(z_ref[jj], axis=-1, keepdims=True) for jj in range(nj)) / d
        var = sum(jnp.sum(jnp.square(z_ref[jj] - mu), axis=-1, keepdims=True) for jj in range(nj)) / d
        rstd = lax.rsqrt(var + LN_EPS)
        logits = br_ref[...]
        for jj in range(nj):
            cols = slice(jj * tn, (jj + 1) * tn)
            h = (z_ref[jj] - mu) * rstd * g_ref[:, cols] + b_ref[:, cols]
            h1_ref[:, cols] = h
            logits = logits + _dot(h.astype(BF16), wr_ref[cols, :])
        lg_ref[...] = logits


def _outproj(y, w_out_p, x2, g, b, wr, br, alpha):
    t, d = x2.shape
    kdim = y.shape[1]
    tm, tn = MIX_TM, OUT_TN
    nj = d // tn
    assert t % tm == 0 and d % tn == 0 and w_out_p.shape[0] == kdim
    return pl.pallas_call(
        functools.partial(_outproj_kernel, tn=tn, nj=nj, alpha=alpha),
        grid=(t // tm, nj),
        in_specs=[
            pl.BlockSpec((tm, kdim), lambda i, j: (i, 0)),
            pl.BlockSpec((kdim, tn), lambda i, j: (0, j)),
            pl.BlockSpec((tm, tn), lambda i, j: (i, j)),
            pl.BlockSpec((1, d), lambda i, j: (0, 0)),
            pl.BlockSpec((1, d), lambda i, j: (0, 0)),
            pl.BlockSpec((d, LANE), lambda i, j: (0, 0)),
            pl.BlockSpec((1, LANE), lambda i, j: (0, 0)),
        ],
        out_specs=[pl.BlockSpec((tm, d), lambda i, j: (i, 0)),
                   pl.BlockSpec((tm, LANE), lambda i, j: (i, 0))],
        out_shape=[jax.ShapeDtypeStruct((t, d), F32), jax.ShapeDtypeStruct((t, LANE), F32)],
        scratch_shapes=[pltpu.VMEM((nj, tm, tn), F32)],
        compiler_params=_params("arbitrary", "arbitrary"),
        name="outproj",
    )(y, w_out_p, x2, g, b, wr, br)


def _route_kernel(lg_ref, pk_ref, cnt_ref, run_ref, *, n_groups, per_group):
    i = pl.program_id(0)

    @pl.when(i == 0)
    def _():
        run_ref[...] = jnp.zeros_like(run_ref)

    l = lg_ref[...]
    tm = l.shape[0]
    lane = lax.broadcasted_iota(jnp.int32, l.shape, 1)
    neg = jnp.float32(-jnp.inf)
    gmask = lane < n_groups
    gl = jnp.where(gmask, l, neg)
    gmax = jnp.max(gl, axis=-1, keepdims=True)
    gi = jnp.min(jnp.where(gl == gmax, lane, LANE), axis=-1, keepdims=True)
    gp = 1.0 / jnp.sum(jnp.where(gmask, jnp.exp(l - gmax), 0.0), axis=-1, keepdims=True)
    elane = lane - n_groups
    emask = (elane >= 0) & (elane // per_group == gi)
    el = jnp.where(emask, l, neg)
    m1 = jnp.max(el, axis=-1, keepdims=True)
    i1 = jnp.min(jnp.where(el == m1, lane, LANE), axis=-1, keepdims=True)
    el2 = jnp.where(lane == i1, neg, el)
    m2 = jnp.max(el2, axis=-1, keepdims=True)
    i2 = jnp.min(jnp.where(el2 == m2, lane, LANE), axis=-1, keepdims=True)
    t2 = jnp.exp(m2 - m1)
    w0 = gp / (1.0 + t2)
    w1 = gp * t2 / (1.0 + t2)
    e0 = i1 - n_groups
    e1 = i2 - n_groups
    oh0 = lane == e0
    oh1 = lane == e1
    oh = (oh0 | oh1).astype(BF16)
    row = lax.broadcasted_iota(jnp.int32, (tm, tm), 0)
    col = lax.broadcasted_iota(jnp.int32, (tm, tm), 1)
    tri = (row > col).astype(BF16)
    before = run_ref[...] + _dot(tri, oh)
    r0 = jnp.sum(jnp.where(oh0, before, 0.0), axis=-1, keepdims=True)
    r1 = jnp.sum(jnp.where(oh1, before, 0.0), axis=-1, keepdims=True)
    total = run_ref[...] + jnp.sum(oh.astype(F32), axis=0, keepdims=True)
    run_ref[...] = total
    cnt_ref[...] = total
    vals = (e0.astype(F32), e1.astype(F32), w0, w1, r0, r1)
    pk = jnp.zeros(l.shape, F32)
    for k, val in enumerate(vals):
        pk = jnp.where(lane == k, val, pk)
    pk_ref[...] = pk


def _route(logits, n_groups, per_group):
    t = logits.shape[0]
    tm = ROUTE_TM
    assert t % tm == 0 and n_groups * (1 + per_group) <= LANE
    return pl.pallas_call(
        functools.partial(_route_kernel, n_groups=n_groups, per_group=per_group),
        grid=(t // tm,),
        in_specs=[pl.BlockSpec((tm, LANE), lambda i: (i, 0))],
        out_specs=[pl.BlockSpec((tm, LANE), lambda i: (i, 0)),
                   pl.BlockSpec((1, LANE), lambda i: (0, 0))],
        out_shape=[jax.ShapeDtypeStruct((t, LANE), F32), jax.ShapeDtypeStruct((1, LANE), F32)],
        scratch_shapes=[pltpu.VMEM((1, LANE), F32)],
        compiler_params=_params("arbitrary"),
        name="route",
    )(logits)


def _rowmap_kernel(eid_hbm, rank_hbm, start_hbm, asg_hbm, eid_v, rank_v, start_v, asg_v, *, n_rows, n_asg, chunk):
    on_first = (lax.axis_index("c") == 0) & (lax.axis_index("s") == 0)

    @pl.when(on_first)
    def _():
        pltpu.sync_copy(start_hbm, start_v)
        zeros = jnp.zeros((SC_LANES,), jnp.int32)

        group = SC_UNROLL * SC_LANES

        @pl.loop(0, n_rows, step=group)
        def _(r):
            for u in range(SC_UNROLL):
                asg_v[pl.ds(r + u * SC_LANES, SC_LANES)] = zeros

        lanes = lax.iota(jnp.int32, SC_LANES)

        @pl.loop(0, n_asg, step=chunk)
        def _(a0):
            pltpu.sync_copy(eid_hbm.at[pl.ds(a0, chunk)], eid_v)
            pltpu.sync_copy(rank_hbm.at[pl.ds(a0, chunk)], rank_v)

            @pl.loop(0, chunk, step=group)
            def _(k0):
                for u in range(SC_UNROLL):
                    k = k0 + u * SC_LANES
                    dest = plsc.load_gather(start_v, [eid_v[pl.ds(k, SC_LANES)]]) + rank_v[pl.ds(k, SC_LANES)]
                    plsc.store_scatter(asg_v, [dest], lanes + (a0 + k))

        pltpu.sync_copy(asg_v, asg_hbm)


def _rowmap(eid, rank, start, n_rows):
    n_asg = eid.shape[0]
    chunk = min(ROWMAP_CHUNK, n_asg)
    group = SC_UNROLL * SC_LANES
    assert n_asg % chunk == 0 and chunk % group == 0 and n_rows % group == 0 and start.shape[0] % SC_LANES == 0
    mesh = plsc.VectorSubcoreMesh(core_axis_name="c", subcore_axis_name="s",
                                  num_cores=SC_CORES, num_subcores=SC_SUBCORES)
    i32 = jnp.int32
    return pl.kernel(
        functools.partial(_rowmap_kernel, n_rows=n_rows, n_asg=n_asg, chunk=chunk),
        out_type=jax.ShapeDtypeStruct((n_rows,), i32),
        mesh=mesh,
        scratch_types=[pltpu.VMEM((chunk,), i32), pltpu.VMEM((chunk,), i32),
                       pltpu.VMEM(start.shape, i32), pltpu.VMEM((n_rows,), i32)],
        compiler_params=pltpu.CompilerParams(needs_layout_passes=False),
        name="rowmap",
    )(eid, rank, start)


def _moe_kernel(be_ref, row0_ref, nsub_ref, nreal_ref, asg_ref, h1_hbm, wg_ref, wu_ref, wd_ref, y_hbm,
                xg_ref, xb_ref, yacc_ref, gsem, ssem, *, nb, nf, sub, tok_mask):
    b = pl.program_id(0)
    f = pl.program_id(1)
    nsub = nsub_ref[b]
    max_tiles = xb_ref.shape[0] // sub

    def gather_row(base, r):
        tok = asg_ref[base + r] & tok_mask
        pltpu.make_async_copy(h1_hbm.at[pl.ds(tok, 1)], xg_ref.at[pl.ds(r, 1)], gsem).start()

    def scatter_row(base, r):
        dst = asg_ref[base + r]
        pltpu.make_async_copy(yacc_ref.at[pl.ds(r, 1)], y_hbm.at[pl.ds(dst, 1)], ssem).start()

    def issue_tiles(row_fn, base, n_tiles):
        for s in range(max_tiles):
            @pl.when(s < n_tiles)
            def _(s=s):
                for r in range(s * sub, (s + 1) * sub):
                    row_fn(base, r)

    def issue_range(row_fn, base, lo, hi):
        groups = (hi - lo) // ISSUE_UNROLL

        def body(q, c):
            for k in range(ISSUE_UNROLL):
                row_fn(base, lo + q * ISSUE_UNROLL + k)
            return c

        lax.fori_loop(0, groups, body, 0)

        def tail(r, c):
            row_fn(base, r)
            return c

        lax.fori_loop(lo + groups * ISSUE_UNROLL, hi, tail, 0)

    def wait_rows(n, tile_copy, row_copy):
        def tile(s, c):
            tile_copy.wait()
            return c

        lax.fori_loop(0, n // sub, tile, 0)

        def row(r, c):
            row_copy.wait()
            return c

        lax.fori_loop((n // sub) * sub, n, row, 0)

    def gather_wait(n):
        wait_rows(n, pltpu.make_async_copy(h1_hbm.at[pl.ds(0, sub)], xg_ref.at[pl.ds(0, sub)], gsem),
                  pltpu.make_async_copy(h1_hbm.at[pl.ds(0, 1)], xg_ref.at[pl.ds(0, 1)], gsem))

    def scatter_wait(blk):
        wait_rows(nreal_ref[blk], pltpu.make_async_copy(yacc_ref.at[pl.ds(0, sub)], y_hbm.at[pl.ds(0, sub)], ssem),
                  pltpu.make_async_copy(yacc_ref.at[pl.ds(0, 1)], y_hbm.at[pl.ds(0, 1)], ssem))

    @pl.when((f == 0) & (b == 0))
    def _():
        yacc_ref[...] = jnp.zeros(yacc_ref.shape, F32)

    @pl.when((f == 0) & (b > 0) & (nsub == 0))
    def _():
        scatter_wait(jnp.maximum(b - 1, 0))

    @pl.when(nsub > 0)
    def _():
        @pl.when(f == 0)
        def _():
            @pl.when(b == 0)
            def _():
                issue_range(gather_row, row0_ref[0], 0, nsub * sub)

            gather_wait(nsub * sub)

            def cast(s, c):
                rows = pl.ds(pl.multiple_of(s * sub, sub), sub)
                xb_ref[rows, :] = xg_ref[rows, :].astype(BF16)
                return c

            lax.fori_loop(0, nsub, cast, 0)

            @pl.when(b + 1 < nb)
            def _():
                nxt = jnp.minimum(b + 1, nb - 1)
                issue_tiles(gather_row, row0_ref[nxt], nsub_ref[nxt])

        def compute(m):
            rows = slice(0, m * sub)
            xs = xb_ref[rows, :]
            hb = (jax.nn.silu(_dot(xs, wg_ref[...])) * _dot(xs, wu_ref[...])).astype(BF16)

            @pl.when((f == 0) & (b > 0))
            def _():
                scatter_wait(jnp.maximum(b - 1, 0))

            yp = _dot(hb, wd_ref[...])
            yacc_ref[rows, :] = jnp.where(f == 0, yp, yacc_ref[rows, :] + yp)

        for m in range(1, xb_ref.shape[0] // sub + 1):
            pl.when(nsub == m)(functools.partial(compute, m))

        @pl.when(f == nf - 1)
        def _():
            full = nreal_ref[b] // sub
            issue_tiles(scatter_row, row0_ref[b], full)
            issue_range(scatter_row, row0_ref[b], full * sub, nreal_ref[b])

            @pl.when(b == nb - 1)
            def _():
                scatter_wait(b)


def _moe(be, row0, nsub, nreal, asg, h1, w_gate, w_up, w_down, n_out_rows):
    t, d = h1.shape
    ne, _, de = w_gate.shape
    nb = be.shape[0]
    nf = de // MOE_TF
    assert de % MOE_TF == 0 and t & (t - 1) == 0
    last = nf - 1
    fsel = lambda f, n: jnp.where(n > 0, f, last)
    grid_spec = pltpu.PrefetchScalarGridSpec(
        num_scalar_prefetch=5,
        grid=(nb, nf),
        in_specs=[
            pl.BlockSpec(memory_space=pl.ANY),
            pl.BlockSpec((None, d, MOE_TF), lambda b, f, be, r0, ns, nr, asg: (be[b], 0, fsel(f, ns[b]))),
            pl.BlockSpec((None, d, MOE_TF), lambda b, f, be, r0, ns, nr, asg: (be[b], 0, fsel(f, ns[b]))),
            pl.BlockSpec((None, MOE_TF, d), lambda b, f, be, r0, ns, nr, asg: (be[b], fsel(f, ns[b]), 0)),
        ],
        out_specs=pl.BlockSpec(memory_space=pl.ANY),
        scratch_shapes=[pltpu.VMEM((MOE_ROWS, d), F32),
                        pltpu.VMEM((MOE_ROWS, d), BF16),
                        pltpu.VMEM((MOE_ROWS, d), F32),
                        pltpu.SemaphoreType.DMA,
                        pltpu.SemaphoreType.DMA],
    )
    return pl.pallas_call(
        functools.partial(_moe_kernel, nb=nb, nf=nf, sub=MOE_SUB, tok_mask=t - 1),
        grid_spec=grid_spec,
        out_shape=jax.ShapeDtypeStruct((n_out_rows, d), F32),
        compiler_params=_params("arbitrary", "arbitrary", vmem=MOE_VMEM_LIMIT),
        name="moe",
    )(be, row0, nsub, nreal, asg, h1, w_gate, w_up, w_down)


def _combine_gate_kernel(h1_ref, y0_ref, y1_ref, pk_ref, g_ref, b_ref, wpg_ref, bpg_ref, p_ref, wpe_ref,
                         o_ref, za_ref, zb_ref, ha_ref, hb_ref, *, nj, tn, alpha):
    n = pl.program_id(0)
    j = n % nj
    parity = (n // nj) % 2

    @pl.when(n == 0)
    def _():
        for ref in (za_ref, zb_ref, ha_ref, hb_ref):
            ref[...] = jnp.zeros(ref.shape, ref.dtype)

    def layer_norm_tile(z_ref, h_ref):
        d = nj * tn
        mu = sum(jnp.sum(z_ref[jj], axis=-1, keepdims=True) for jj in range(nj)) / d
        var = sum(jnp.sum(jnp.square(z_ref[jj] - mu), axis=-1, keepdims=True) for jj in range(nj)) / d
        rstd = lax.rsqrt(var + LN_EPS)
        for jj in range(nj):
            cols = slice(jj * tn, (jj + 1) * tn)
            h = (z_ref[jj] - mu) * rstd * g_ref[:, cols] + b_ref[:, cols]
            z_ref[jj] = h
            h_ref[:, cols] = h.astype(BF16)

    def step(zw_ref, hw_ref, zo_ref, ho_ref, first):
        pk = pk_ref[...]
        ffn = pk[:, 2:3] * y0_ref[...] + pk[:, 3:4] * y1_ref[...]
        z = alpha * h1_ref[...] + ffn
        if first:
            zw_ref[0] = z
            layer_norm_tile(zo_ref, ho_ref)
            zg, hg = zw_ref[nj - 1], hw_ref[...]
        else:
            zw_ref[j] = z
            zg, hg = zo_ref[j - 1], ho_ref[...]
        gate = jax.nn.sigmoid(_dot(hg, wpg_ref[...]) + bpg_ref[...])
        ple = _dot(p_ref[...].astype(BF16), wpe_ref[...])
        o_ref[...] = zg + gate * ple

    slots = ((za_ref, ha_ref, zb_ref, hb_ref), (zb_ref, hb_ref, za_ref, ha_ref))
    for par in range(2):
        for first in (True, False):
            cond = (parity == par) & ((j == 0) if first else (j > 0))
            pl.when(cond)(functools.partial(step, *slots[par], first))


def _combine_gate(h1, y2, pk, g, b, w_pg_b, b_pg, p2, w_pe_b, alpha):
    t, d = h1.shape
    pd = p2.shape[1]
    tm, tn = MIX_TM, OUT_TN
    assert t % tm == 0 and d % tn == 0
    ni, nj = t // tm, d // tn
    n_comb = ni * nj
    cmb = lambda n: jnp.minimum(n, n_comb - 1)
    gat = lambda n: jnp.clip(n - nj - 1, 0, n_comb - 1)
    return pl.pallas_call(
        functools.partial(_combine_gate_kernel, nj=nj, tn=tn, alpha=alpha),
        grid=(n_comb + nj + 1,),
        in_specs=[
            pl.BlockSpec((tm, tn), lambda n: (cmb(n) // nj, cmb(n) % nj)),
            pl.BlockSpec((tm, tn), lambda n: (cmb(n) // nj, cmb(n) % nj)),
            pl.BlockSpec((tm, tn), lambda n: (cmb(n) // nj + ni, cmb(n) % nj)),
            pl.BlockSpec((tm, LANE), lambda n: (cmb(n) // nj, 0)),
            pl.BlockSpec((1, d), lambda n: (0, 0)),
            pl.BlockSpec((1, d), lambda n: (0, 0)),
            pl.BlockSpec((d, tn), lambda n: (0, gat(n) % nj)),
            pl.BlockSpec((1, tn), lambda n: (0, gat(n) % nj)),
            pl.BlockSpec((tm, pd), lambda n: (gat(n) // nj, 0)),
            pl.BlockSpec((pd, tn), lambda n: (0, gat(n) % nj)),
        ],
        out_specs=pl.BlockSpec((tm, tn), lambda n: (gat(n) // nj, gat(n) % nj)),
        out_shape=jax.ShapeDtypeStruct((t, d), F32),
        scratch_shapes=[pltpu.VMEM((nj, tm, tn), F32), pltpu.VMEM((nj, tm, tn), F32),
                        pltpu.VMEM((tm, d), BF16), pltpu.VMEM((tm, d), BF16)],
        compiler_params=_params("arbitrary"),
        name="combine_gate",
    )(h1, y2, y2, pk, g, b, w_pg_b, b_pg, p2, w_pe_b)


def _block_table(counts, n_asg):
    ne = counts.shape[0]
    padded = ((counts + MOE_SUB - 1) // MOE_SUB) * MOE_SUB
    pad_end = jnp.cumsum(padded)
    pad_start = pad_end - padded
    nblk = (padded + MOE_ROWS - 1) // MOE_ROWS
    blk_end = jnp.cumsum(nblk)
    blk_start = blk_end - nblk
    nb = ne + -(-n_asg // MOE_ROWS)
    bidx = jnp.arange(nb, dtype=jnp.int32)
    used = bidx < blk_end[-1]
    be = jnp.minimum(jnp.searchsorted(blk_end, jnp.minimum(bidx, blk_end[-1] - 1), side="right"), ne - 1)
    be = be.astype(jnp.int32)
    within = bidx - blk_start[be]
    row0 = pad_start[be] + within * MOE_ROWS
    nsub = jnp.clip((padded[be] - within * MOE_ROWS) // MOE_SUB, 0, MOE_ROWS // MOE_SUB)
    nsub = jnp.where(used, nsub, 0)
    row0 = jnp.where(used, row0, 0)
    nreal = jnp.where(used, jnp.clip(counts[be] - within * MOE_ROWS, 0, MOE_ROWS), 0)
    i32 = lambda a: a.astype(jnp.int32)
    return i32(pad_start), be, i32(row0), i32(nsub), i32(nreal)


def kernel(x, p, w_in, conv_w, sg_ln_g, sg_ln_b, sg_w, sg_b, w_out, ln1_g, ln1_b, w_rg, b_rg, w_re, b_re,
           w_gate, w_up, w_down, ln2_g, ln2_b, w_pg, b_pg, w_pe):
    depth = w_in.shape[0]
    bsz, seq, d = x.shape
    t = bsz * seq
    alpha = (2 * depth) ** 0.25
    n_groups = w_rg.shape[-1]
    ne = w_re.shape[-1]
    per_group = ne // n_groups
    n_asg = t * TOP_K
    n_rows = ((n_asg + ne * (MOE_SUB - 1) + MOE_SUB - 1) // MOE_SUB) * MOE_SUB

    h = x.reshape(t, d)
    for i in range(depth):
        cdim = conv_w.shape[-1]
        nj = cdim // MIX_TN
        w_in_p = _regroup_cols(w_in[i], 5, nj, MIX_TN)
        y, w_out_p, w_pg_b = _mixer(h, w_in_p, conv_w[i], sg_ln_g[i].reshape(1, -1), sg_ln_b[i].reshape(1, -1),
                                    sg_w[i], sg_b[i][:, :, None], w_out[i], w_pg[i], seq)
        wr = jnp.concatenate([w_rg[i], w_re[i], jnp.zeros((d, LANE - n_groups - ne), F32)], axis=1).astype(BF16)
        br = jnp.concatenate([b_rg[i], b_re[i], jnp.zeros((LANE - n_groups - ne,), F32)]).reshape(1, LANE)
        h1, logits = _outproj(y, w_out_p, h, ln1_g[i].reshape(1, d), ln1_b[i].reshape(1, d), wr, br, alpha)
        pk, cnt = _route(logits, n_groups, per_group)
        pad_start, be, row0, nsub, nreal = _block_table(cnt[0, :ne].astype(jnp.int32), n_asg)
        eid = jnp.concatenate([pk[:, 0], pk[:, 1]]).astype(jnp.int32)
        rank = jnp.concatenate([pk[:, 4], pk[:, 5]]).astype(jnp.int32)
        asg = _rowmap(eid, rank, pad_start, n_rows)
        y2 = _moe(be, row0, nsub, nreal, asg, h1, w_gate[i], w_up[i], w_down[i], n_asg)
        h = _combine_gate(h1, y2, pk, ln2_g[i].reshape(1, d), ln2_b[i].reshape(1, d), w_pg_b,
                          b_pg[i].reshape(1, d), p[i].reshape(t, -1), w_pe[i].astype(BF16), alpha)
    return h.reshape(bsz, seq, d)
```

```python
import functools

import jax
import jax.numpy as jnp
from jax import lax
from jax.experimental import pallas as pl
from jax.experimental.pallas import tpu as pltpu
from jax.experimental.pallas import tpu_sc as plsc

F32 = jnp.float32
BF16 = jnp.bfloat16

LANE = 128
SUBLANE = 8
MXU_COL = 256
SC_CORES = 2
SC_SUBCORES = 16
SC_LANES = 16
VMEM_LIMIT = 56 * 1024 * 1024
MOE_VMEM_LIMIT = 60 * 1024 * 1024

CHUNK = 64
SG_BLOCK = 128
LN_EPS = 1e-5
TOP_K = 2

MIX_TM = 512
MIX_TN = MXU_COL
OUT_TN = 512
ROUTE_TM = 512
MOE_SUB = 128
MOE_ROWS = 768
MOE_TF = MXU_COL
ISSUE_UNROLL = 8
ROWMAP_CHUNK = 16384
SC_UNROLL = 8
CAST_TK = 4096
CAST_ROWS = 16

_dot = functools.partial(jnp.dot, preferred_element_type=F32)


def _params(*sem, vmem=VMEM_LIMIT):
    return pltpu.CompilerParams(dimension_semantics=sem, vmem_limit_bytes=vmem)


def _layer_norm(z, g, b):
    mu = jnp.mean(z, axis=-1, keepdims=True)
    zc = z - mu
    var = jnp.mean(zc * zc, axis=-1, keepdims=True)
    return zc * lax.rsqrt(var + LN_EPS) * g + b


def _cast_kernel(w_ref, o_ref):
    o_ref[...] = w_ref[...].astype(o_ref.dtype)


def _regroup_cols(w, groups, nj, tn):
    d = w.shape[0]
    tk = min(CAST_TK, d)
    assert d % tk == 0 and w.shape[1] == groups * nj * tn
    return pl.pallas_call(
        _cast_kernel,
        grid=(nj, groups, d // tk),
        in_specs=[pl.BlockSpec((tk, tn), lambda j, s, k: (k, s * nj + j))],
        out_specs=pl.BlockSpec((tk, tn), lambda j, s, k: (k, j * groups + s)),
        out_shape=jax.ShapeDtypeStruct(w.shape, BF16),
        compiler_params=_params("arbitrary", "arbitrary", "arbitrary"),
        name="regroup_cols",
    )(w)


def _mixer_kernel(x_ref, w_ref, cw_ref, lng_ref, lnb_ref, sgw_ref, sgb_ref, wo_ref, wg_ref,
                  y_ref, wo_out_ref, wg_out_ref,
                  xb_ref, pa_ref, pb_ref, gbuf_ref, carry_ref, *, tm, tn, nj, n_steps, tiles_per_seq):
    n = pl.program_id(0)

    @pl.when(n == 0)
    def _():
        pb_ref[...] = jnp.zeros(pb_ref.shape, F32)
        carry_ref[...] = jnp.zeros(carry_ref.shape, F32)

    @pl.when((n % nj == 0) & (n < n_steps))
    def _():
        xb_ref[...] = x_ref[...].astype(BF16)

    m = jnp.maximum(n - 1, 0)
    jp = m % nj
    first = ((m // nj) % tiles_per_seq) == 0

    def mix(proj_ref):
        b, c, h, u, v = (proj_ref[:, k * tn:(k + 1) * tn] for k in range(5))
        g = c * h
        gbuf_ref[0:SUBLANE, :] = jnp.where(first, 0.0, carry_ref[jp])
        gbuf_ref[SUBLANE:, :] = g
        carry_ref[jp] = g[tm - SUBLANE:, :]
        g1 = gbuf_ref[pl.ds(SUBLANE - 1, tm), :]
        g2 = gbuf_ref[pl.ds(SUBLANE - 2, tm), :]
        cw = cw_ref[...]
        conv = cw[0:1, :] * g2 + cw[1:2, :] * g1 + cw[2:3, :] * g
        y_ref[:, 0:tn] = (b * conv).astype(y_ref.dtype)

        pos_i = lax.broadcasted_iota(jnp.int32, (SG_BLOCK, SG_BLOCK), 0)
        pos_j = lax.broadcasted_iota(jnp.int32, (SG_BLOCK, SG_BLOCK), 1)
        mask = (pos_j // CHUNK) <= (pos_i // CHUNK)
        for hh in range(tn // LANE):
            sl = slice(hh * LANE, (hh + 1) * LANE)
            vn = _layer_norm(jax.nn.gelu(v[:, sl]), lng_ref[:, sl], lnb_ref[:, sl]).astype(BF16)
            gu = jax.nn.gelu(u[:, sl])
            ws = jnp.where(mask, sgw_ref[hh], 0.0).astype(BF16)
            bcol = sgb_ref[hh]
            nblk = tm // SG_BLOCK
            vcat = jnp.concatenate([vn[r * SG_BLOCK:(r + 1) * SG_BLOCK, :] for r in range(nblk)], axis=1)
            sg = _dot(ws, vcat) + bcol
            for r in range(nblk):
                rows = slice(r * SG_BLOCK, (r + 1) * SG_BLOCK)
                y_ref[rows, tn + hh * LANE:tn + (hh + 1) * LANE] = (
                    gu[rows, :] * sg[:, r * LANE:(r + 1) * LANE]).astype(y_ref.dtype)

    def step(store_ref, load_ref):
        store_ref[...] = _dot(xb_ref[...], w_ref[...])
        mix(load_ref)
        wo_out_ref[...] = wo_ref[...].astype(BF16)
        wg_out_ref[...] = wg_ref[...].astype(BF16)

    pl.when(n % 2 == 0)(functools.partial(step, pa_ref, pb_ref))
    pl.when(n % 2 == 1)(functools.partial(step, pb_ref, pa_ref))


def _mixer(x2, w_in_p, conv_w, lng, lnb, sg_w, sg_bcol, w_out, w_pg, seq):
    t, d = x2.shape
    cdim = conv_w.shape[1]
    tm, tn = MIX_TM, MIX_TN
    nj = cdim // tn
    assert seq % tm == 0 and t % tm == 0 and cdim % tn == 0 and w_in_p.shape[1] == 5 * cdim
    hp = tn // LANE
    n_steps = (t // tm) * nj
    cur = lambda n: jnp.minimum(n, n_steps - 1)
    prv = lambda n: jnp.maximum(n - 1, 0)
    wrows = w_out.shape[0]
    cb = max(CAST_ROWS, -(-wrows // n_steps // CAST_ROWS) * CAST_ROWS)
    nblk = wrows // cb
    spb = n_steps // nblk
    bpt = tn // cb
    assert w_pg.shape == w_out.shape == (2 * cdim, d) and wrows % cb == 0 and n_steps % nblk == 0 and tn % cb == 0
    blk = lambda n: jnp.minimum(n // spb, nblk - 1)

    def wo_src(n):
        k = blk(n)
        tile, sub = k // bpt, k % bpt
        return ((tile % 2) * nj + tile // 2) * bpt + sub, 0

    return pl.pallas_call(
        functools.partial(_mixer_kernel, tm=tm, tn=tn, nj=nj, n_steps=n_steps, tiles_per_seq=seq // tm),
        grid=(n_steps + 1,),
        in_specs=[
            pl.BlockSpec((tm, d), lambda n: (cur(n) // nj, 0)),
            pl.BlockSpec((d, 5 * tn), lambda n: (0, cur(n) % nj)),
            pl.BlockSpec((conv_w.shape[0], tn), lambda n: (0, prv(n) % nj)),
            pl.BlockSpec((1, tn), lambda n: (0, prv(n) % nj)),
            pl.BlockSpec((1, tn), lambda n: (0, prv(n) % nj)),
            pl.BlockSpec((hp, SG_BLOCK, SG_BLOCK), lambda n: (prv(n) % nj, 0, 0)),
            pl.BlockSpec((hp, SG_BLOCK, 1), lambda n: (prv(n) % nj, 0, 0)),
            pl.BlockSpec((cb, d), wo_src),
            pl.BlockSpec((cb, d), lambda n: (blk(n), 0)),
        ],
        out_specs=[pl.BlockSpec((tm, 2 * tn), lambda n: (prv(n) // nj, prv(n) % nj)),
                   pl.BlockSpec((cb, d), lambda n: (blk(n), 0)),
                   pl.BlockSpec((cb, d), lambda n: (blk(n), 0))],
        out_shape=[jax.ShapeDtypeStruct((t, 2 * cdim), BF16),
                   jax.ShapeDtypeStruct(w_out.shape, BF16),
                   jax.ShapeDtypeStruct(w_pg.shape, BF16)],
        scratch_shapes=[pltpu.VMEM((tm, d), BF16),
                        pltpu.VMEM((tm, 5 * tn), F32),
                        pltpu.VMEM((tm, 5 * tn), F32),
                        pltpu.VMEM((tm + SUBLANE, tn), F32),
                        pltpu.VMEM((nj, SUBLANE, tn), F32)],
        compiler_params=_params("arbitrary"),
        name="mixer",
    )(x2, w_in_p, conv_w, lng, lnb, sg_w, sg_bcol, w_out, w_pg)


def _outproj_kernel(y_ref, w_ref, x_ref, g_ref, b_ref, wr_ref, br_ref, h1_ref, lg_ref, z_ref, *, tn, nj, alpha):
    j = pl.program_id(1)
    z_ref[j] = alpha * x_ref[...] + _dot(y_ref[...], w_ref[...])

    @pl.when(j == nj - 1)
    def _():
        d = nj * tn
        mu = sum(jnp.sum(z_ref[jj], axis=-1, keepdims=True) for jj in range(nj)) / d
        var = sum(jnp.sum(jnp.square(z_ref[jj] - mu), axis=-1, keepdims=True) for jj in range(nj)) / d
        rstd = lax.rsqrt(var + LN_EPS)
        logits = br_ref[...]
        for jj in range(nj):
            cols = slice(jj * tn, (jj + 1) * tn)
            h = (z_ref[jj] - mu) * rstd * g_ref[:, cols] + b_ref[:, cols]
            h1_ref[:, cols] = h
            logits = logits + _dot(h.astype(BF16), wr_ref[cols, :])
        lg_ref[...] = logits


def _outproj(y, w_out_p, x2, g, b, wr, br, alpha):
    t, d = x2.shape
    kdim = y.shape[1]
    tm, tn = MIX_TM, OUT_TN
    nj = d // tn
    assert t % tm == 0 and d % tn == 0 and w_out_p.shape[0] == kdim
    return pl.pallas_call(
        functools.partial(_outproj_kernel, tn=tn, nj=nj, alpha=alpha),
        grid=(t // tm, nj),
        in_specs=[
            pl.BlockSpec((tm, kdim), lambda i, j: (i, 0)),
            pl.BlockSpec((kdim, tn), lambda i, j: (0, j)),
            pl.BlockSpec((tm, tn), lambda i, j: (i, j)),
            pl.BlockSpec((1, d), lambda i, j: (0, 0)),
            pl.BlockSpec((1, d), lambda i, j: (0, 0)),
            pl.BlockSpec((d, LANE), lambda i, j: (0, 0)),
            pl.BlockSpec((1, LANE), lambda i, j: (0, 0)),
        ],
        out_specs=[pl.BlockSpec((tm, d), lambda i, j: (i, 0)),
                   pl.BlockSpec((tm, LANE), lambda i, j: (i, 0))],
        out_shape=[jax.ShapeDtypeStruct((t, d), F32), jax.ShapeDtypeStruct((t, LANE), F32)],
        scratch_shapes=[pltpu.VMEM((nj, tm, tn), F32)],
        compiler_params=_params("arbitrary", "arbitrary"),
        name="outproj",
    )(y, w_out_p, x2, g, b, wr, br)


def _route_kernel(lg_ref, pk_ref, cnt_ref, run_ref, *, n_groups, per_group):
    i = pl.program_id(0)

    @pl.when(i == 0)
    def _():
        run_ref[...] = jnp.zeros_like(run_ref)

    l = lg_ref[...]
    tm = l.shape[0]
    lane = lax.broadcasted_iota(jnp.int32, l.shape, 1)
    neg = jnp.float32(-jnp.inf)
    gmask = lane < n_groups
    gl = jnp.where(gmask, l, neg)
    gmax = jnp.max(gl, axis=-1, keepdims=True)
    gi = jnp.min(jnp.where(gl == gmax, lane, LANE), axis=-1, keepdims=True)
    gp = 1.0 / jnp.sum(jnp.where(gmask, jnp.exp(l - gmax), 0.0), axis=-1, keepdims=True)
    elane = lane - n_groups
    emask = (elane >= 0) & (elane // per_group == gi)
    el = jnp.where(emask, l, neg)
    m1 = jnp.max(el, axis=-1, keepdims=True)
    i1 = jnp.min(jnp.where(el == m1, lane, LANE), axis=-1, keepdims=True)
    el2 = jnp.where(lane == i1, neg, el)
    m2 = jnp.max(el2, axis=-1, keepdims=True)
    i2 = jnp.min(jnp.where(el2 == m2, lane, LANE), axis=-1, keepdims=True)
    t2 = jnp.exp(m2 - m1)
    w0 = gp / (1.0 + t2)
    w1 = gp * t2 / (1.0 + t2)
    e0 = i1 - n_groups
    e1 = i2 - n_groups
    oh0 = lane == e0
    oh1 = lane == e1
    oh = (oh0 | oh1).astype(BF16)
    row = lax.broadcasted_iota(jnp.int32, (tm, tm), 0)
    col = lax.broadcasted_iota(jnp.int32, (tm, tm), 1)
    tri = (row > col).astype(BF16)
    before = run_ref[...] + _dot(tri, oh)
    r0 = jnp.sum(jnp.where(oh0, before, 0.0), axis=-1, keepdims=True)
    r1 = jnp.sum(jnp.where(oh1, before, 0.0), axis=-1, keepdims=True)
    total = run_ref[...] + jnp.sum(oh.astype(F32), axis=0, keepdims=True)
    run_ref[...] = total
    cnt_ref[...] = total
    vals = (e0.astype(F32), e1.astype(F32), w0, w1, r0, r1)
    pk = jnp.zeros(l.shape, F32)
    for k, val in enumerate(vals):
        pk = jnp.where(lane == k, val, pk)
    pk_ref[...] = pk


def _route(logits, n_groups, per_group):
    t = logits.shape[0]
    tm = ROUTE_TM
    assert t % tm == 0 and n_groups * (1 + per_group) <= LANE
    return pl.pallas_call(
        functools.partial(_route_kernel, n_groups=n_groups, per_group=per_group),
        grid=(t // tm,),
        in_specs=[pl.BlockSpec((tm, LANE), lambda i: (i, 0))],
        out_specs=[pl.BlockSpec((tm, LANE), lambda i: (i, 0)),
                   pl.BlockSpec((1, LANE), lambda i: (0, 0))],
        out_shape=[jax.ShapeDtypeStruct((t, LANE), F32), jax.ShapeDtypeStruct((1, LANE), F32)],
        scratch_shapes=[pltpu.VMEM((1, LANE), F32)],
        compiler_params=_params("arbitrary"),
        name="route",
    )(logits)


def _rowmap_kernel(eid_hbm, rank_hbm, start_hbm, asg_hbm, eid_v, rank_v, start_v, asg_v, *, n_rows, n_asg, chunk):
    on_first = (lax.axis_index("c") == 0) & (lax.axis_index("s") == 0)

    @pl.when(on_first)
    def _():
        pltpu.sync_copy(start_hbm, start_v)
        zeros = jnp.zeros((SC_LANES,), jnp.int32)

        group = SC_UNROLL * SC_LANES

        @pl.loop(0, n_rows, step=group)
        def _(r):
            for u in range(SC_UNROLL):
                asg_v[pl.ds(r + u * SC_LANES, SC_LANES)] = zeros

        lanes = lax.iota(jnp.int32, SC_LANES)

        @pl.loop(0, n_asg, step=chunk)
        def _(a0):
            pltpu.sync_copy(eid_hbm.at[pl.ds(a0, chunk)], eid_v)
            pltpu.sync_copy(rank_hbm.at[pl.ds(a0, chunk)], rank_v)

            @pl.loop(0, chunk, step=group)
            def _(k0):
                for u in range(SC_UNROLL):
                    k = k0 + u * SC_LANES
                    dest = plsc.load_gather(start_v, [eid_v[pl.ds(k, SC_LANES)]]) + rank_v[pl.ds(k, SC_LANES)]
                    plsc.store_scatter(asg_v, [dest], lanes + (a0 + k))

        pltpu.sync_copy(asg_v, asg_hbm)


def _rowmap(eid, rank, start, n_rows):
    n_asg = eid.shape[0]
    chunk = min(ROWMAP_CHUNK, n_asg)
    group = SC_UNROLL * SC_LANES
    assert n_asg % chunk == 0 and chunk % group == 0 and n_rows % group == 0 and start.shape[0] % SC_LANES == 0
    mesh = plsc.VectorSubcoreMesh(core_axis_name="c", subcore_axis_name="s",
                                  num_cores=SC_CORES, num_subcores=SC_SUBCORES)
    i32 = jnp.int32
    return pl.kernel(
        functools.partial(_rowmap_kernel, n_rows=n_rows, n_asg=n_asg, chunk=chunk),
        out_type=jax.ShapeDtypeStruct((n_rows,), i32),
        mesh=mesh,
        scratch_types=[pltpu.VMEM((chunk,), i32), pltpu.VMEM((chunk,), i32),
                       pltpu.VMEM(start.shape, i32), pltpu.VMEM((n_rows,), i32)],
        compiler_params=pltpu.CompilerParams(needs_layout_passes=False),
        name="rowmap",
    )(eid, rank, start)


def _moe_kernel(be_ref, row0_ref, nsub_ref, nreal_ref, asg_ref, h1_hbm, wg_ref, wu_ref, wd_ref, y_hbm,
                xg_ref, xb_ref, yacc_ref, gsem, ssem, *, nb, nf, sub, tok_mask):
    b = pl.program_id(0)
    f = pl.program_id(1)
    nsub = nsub_ref[b]
    max_tiles = xb_ref.shape[0] // sub

    def gather_row(base, r):
        tok = asg_ref[base + r] & tok_mask
        pltpu.make_async_copy(h1_hbm.at[pl.ds(tok, 1)], xg_ref.at[pl.ds(r, 1)], gsem).start()

    def scatter_row(base, r):
        dst = asg_ref[base + r]
        pltpu.make_async_copy(yacc_ref.at[pl.ds(r, 1)], y_hbm.at[pl.ds(dst, 1)], ssem).start()

    def issue_tiles(row_fn, base, n_tiles):
        for s in range(max_tiles):
            @pl.when(s < n_tiles)
            def _(s=s):
                for r in range(s * sub, (s + 1) * sub):
                    row_fn(base, r)

    def issue_range(row_fn, base, lo, hi):
        groups = (hi - lo) // ISSUE_UNROLL

        def body(q, c):
            for k in range(ISSUE_UNROLL):
                row_fn(base, lo + q * ISSUE_UNROLL + k)
            return c

        lax.fori_loop(0, groups, body, 0)

        def tail(r, c):
            row_fn(base, r)
            return c

        lax.fori_loop(lo + groups * ISSUE_UNROLL, hi, tail, 0)

    def wait_rows(n, tile_copy, row_copy):
        def tile(s, c):
            tile_copy.wait()
            return c

        lax.fori_loop(0, n // sub, tile, 0)

        def row(r, c):
            row_copy.wait()
            return c

        lax.fori_loop((n // sub) * sub, n, row, 0)

    def gather_wait(n):
        wait_rows(n, pltpu.make_async_copy(h1_hbm.at[pl.ds(0, sub)], xg_ref.at[pl.ds(0, sub)], gsem),
                  pltpu.make_async_copy(h1_hbm.at[pl.ds(0, 1)], xg_ref.at[pl.ds(0, 1)], gsem))

    def scatter_wait(blk):
        wait_rows(nreal_ref[blk], pltpu.make_async_copy(yacc_ref.at[pl.ds(0, sub)], y_hbm.at[pl.ds(0, sub)], ssem),
                  pltpu.make_async_copy(yacc_ref.at[pl.ds(0, 1)], y_hbm.at[pl.ds(0, 1)], ssem))

    @pl.when((f == 0) & (b == 0))
    def _():
        yacc_ref[...] = jnp.zeros(yacc_ref.shape, F32)

    @pl.when((f == 0) & (b > 0) & (nsub == 0))
    def _():
        scatter_wait(jnp.maximum(b - 1, 0))

    @pl.when(nsub > 0)
    def _():
        @pl.when(f == 0)
        def _():
            @pl.when(b == 0)
            def _():
                issue_range(gather_row, row0_ref[0], 0, nsub * sub)

            gather_wait(nsub * sub)

            def cast(s, c):
                rows = pl.ds(pl.multiple_of(s * sub, sub), sub)
                xb_ref[rows, :] = xg_ref[rows, :].astype(BF16)
                return c

            lax.fori_loop(0, nsub, cast, 0)

            @pl.when(b + 1 < nb)
            def _():
                nxt = jnp.minimum(b + 1, nb - 1)
                issue_tiles(gather_row, row0_ref[nxt], nsub_ref[nxt])

        def compute(m):
            rows = slice(0, m * sub)
            xs = xb_ref[rows, :]
            hb = (jax.nn.silu(_dot(xs, wg_ref[...])) * _dot(xs, wu_ref[...])).astype(BF16)

            @pl.when((f == 0) & (b > 0))
            def _():
                scatter_wait(jnp.maximum(b - 1, 0))

            yp = _dot(hb, wd_ref[...])
            yacc_ref[rows, :] = jnp.where(f == 0, yp, yacc_ref[rows, :] + yp)

        for m in range(1, xb_ref.shape[0] // sub + 1):
            pl.when(nsub == m)(functools.partial(compute, m))

        @pl.when(f == nf - 1)
        def _():
            full = nreal_ref[b] // sub
            issue_tiles(scatter_row, row0_ref[b], full)
            issue_range(scatter_row, row0_ref[b], full * sub, nreal_ref[b])

            @pl.when(b == nb - 1)
            def _():
                scatter_wait(b)


def _moe(be, row0, nsub, nreal, asg, h1, w_gate, w_up, w_down, n_out_rows):
    t, d = h1.shape
    ne, _, de = w_gate.shape
    nb = be.shape[0]
    nf = de // MOE_TF
    assert de % MOE_TF == 0 and t & (t - 1) == 0
    last = nf - 1
    fsel = lambda f, n: jnp.where(n > 0, f, last)
    grid_spec = pltpu.PrefetchScalarGridSpec(
        num_scalar_prefetch=5,
        grid=(nb, nf),
        in_specs=[
            pl.BlockSpec(memory_space=pl.ANY),
            pl.BlockSpec((None, d, MOE_TF), lambda b, f, be, r0, ns, nr, asg: (be[b], 0, fsel(f, ns[b]))),
            pl.BlockSpec((None, d, MOE_TF), lambda b, f, be, r0, ns, nr, asg: (be[b], 0, fsel(f, ns[b]))),
            pl.BlockSpec((None, MOE_TF, d), lambda b, f, be, r0, ns, nr, asg: (be[b], fsel(f, ns[b]), 0)),
        ],
        out_specs=pl.BlockSpec(memory_space=pl.ANY),
        scratch_shapes=[pltpu.VMEM((MOE_ROWS, d), F32),
                        pltpu.VMEM((MOE_ROWS, d), BF16),
                        pltpu.VMEM((MOE_ROWS, d), F32),
                        pltpu.SemaphoreType.DMA,
                        pltpu.SemaphoreType.DMA],
    )
    return pl.pallas_call(
        functools.partial(_moe_kernel, nb=nb, nf=nf, sub=MOE_SUB, tok_mask=t - 1),
        grid_spec=grid_spec,
        out_shape=jax.ShapeDtypeStruct((n_out_rows, d), F32),
        compiler_params=_params("arbitrary", "arbitrary", vmem=MOE_VMEM_LIMIT),
        name="moe",
    )(be, row0, nsub, nreal, asg, h1, w_gate, w_up, w_down)


def _combine_gate_kernel(h1_ref, y0_ref, y1_ref, pk_ref, g_ref, b_ref, wpg_ref, bpg_ref, p_ref, wpe_ref,
                         o_ref, za_ref, zb_ref, ha_ref, hb_ref, *, nj, tn, alpha):
    n = pl.program_id(0)
    j = n % nj
    parity = (n // nj) % 2

    @pl.when(n == 0)
    def _():
        for ref in (za_ref, zb_ref, ha_ref, hb_ref):
            ref[...] = jnp.zeros(ref.shape, ref.dtype)

    def layer_norm_tile(z_ref, h_ref):
        d = nj * tn
        mu = sum(jnp.sum(z_ref[jj], axis=-1, keepdims=True) for jj in range(nj)) / d
        var = sum(jnp.sum(jnp.square(z_ref[jj] - mu), axis=-1, keepdims=True) for jj in range(nj)) / d
        rstd = lax.rsqrt(var + LN_EPS)
        for jj in range(nj):
            cols = slice(jj * tn, (jj + 1) * tn)
            h = (z_ref[jj] - mu) * rstd * g_ref[:, cols] + b_ref[:, cols]
            z_ref[jj] = h
            h_ref[:, cols] = h.astype(BF16)

    def step(zw_ref, hw_ref, zo_ref, ho_ref, first):
        pk = pk_ref[...]
        ffn = pk[:, 2:3] * y0_ref[...] + pk[:, 3:4] * y1_ref[...]
        z = alpha * h1_ref[...] + ffn
        if first:
            zw_ref[0] = z
            layer_norm_tile(zo_ref, ho_ref)
            zg, hg = zw_ref[nj - 1], hw_ref[...]
        else:
            zw_ref[j] = z
            zg, hg = zo_ref[j - 1], ho_ref[...]
        gate = jax.nn.sigmoid(_dot(hg, wpg_ref[...]) + bpg_ref[...])
        ple = _dot(p_ref[...].astype(BF16), wpe_ref[...])
        o_ref[...] = zg + gate * ple

    slots = ((za_ref, ha_ref, zb_ref, hb_ref), (zb_ref, hb_ref, za_ref, ha_ref))
    for par in range(2):
        for first in (True, False):
            cond = (parity == par) & ((j == 0) if first else (j > 0))
            pl.when(cond)(functools.partial(step, *slots[par], first))


def _combine_gate(h1, y2, pk, g, b, w_pg_b, b_pg, p2, w_pe_b, alpha):
    t, d = h1.shape
    pd = p2.shape[1]
    tm, tn = MIX_TM, OUT_TN
    assert t % tm == 0 and d % tn == 0
    ni, nj = t // tm, d // tn
    n_comb = ni * nj
    cmb = lambda n: jnp.minimum(n, n_comb - 1)
    gat = lambda n: jnp.clip(n - nj - 1, 0, n_comb - 1)
    return pl.pallas_call(
        functools.partial(_combine_gate_kernel, nj=nj, tn=tn, alpha=alpha),
        grid=(n_comb + nj + 1,),
        in_specs=[
            pl.BlockSpec((tm, tn), lambda n: (cmb(n) // nj, cmb(n) % nj)),
            pl.BlockSpec((tm, tn), lambda n: (cmb(n) // nj, cmb(n) % nj)),
            pl.BlockSpec((tm, tn), lambda n: (cmb(n) // nj + ni, cmb(n) % nj)),
            pl.BlockSpec((tm, LANE), lambda n: (cmb(n) // nj, 0)),
            pl.BlockSpec((1, d), lambda n: (0, 0)),
            pl.BlockSpec((1, d), lambda n: (0, 0)),
            pl.BlockSpec((d, tn), lambda n: (0, gat(n) % nj)),
            pl.BlockSpec((1, tn), lambda n: (0, gat(n) % nj)),
            pl.BlockSpec((tm, pd), lambda n: (gat(n) // nj, 0)),
            pl.BlockSpec((pd, tn), lambda n: (0, gat(n) % nj)),
        ],
        out_specs=pl.BlockSpec((tm, tn), lambda n: (gat(n) // nj, gat(n) % nj)),
        out_shape=jax.ShapeDtypeStruct((t, d), F32),
        scratch_shapes=[pltpu.VMEM((nj, tm, tn), F32), pltpu.VMEM((nj, tm, tn), F32),
                        pltpu.VMEM((tm, d), BF16), pltpu.VMEM((tm, d), BF16)],
        compiler_params=_params("arbitrary"),
        name="combine_gate",
    )(h1, y2, y2, pk, g, b, w_pg_b, b_pg, p2, w_pe_b)


def _block_table(counts, n_asg):
    ne = counts.shape[0]
    padded = ((counts + MOE_SUB - 1) // MOE_SUB) * MOE_SUB
    pad_end = jnp.cumsum(padded)
    pad_start = pad_end - padded
    nblk = (padded + MOE_ROWS - 1) // MOE_ROWS
    blk_end = jnp.cumsum(nblk)
    blk_start = blk_end - nblk
    nb = ne + -(-n_asg // MOE_ROWS)
    bidx = jnp.arange(nb, dtype=jnp.int32)
    used = bidx < blk_end[-1]
    be = jnp.minimum(jnp.searchsorted(blk_end, jnp.minimum(bidx, blk_end[-1] - 1), side="right"), ne - 1)
    be = be.astype(jnp.int32)
    within = bidx - blk_start[be]
    row0 = pad_start[be] + within * MOE_ROWS
    nsub = jnp.clip((padded[be] - within * MOE_ROWS) // MOE_SUB, 0, MOE_ROWS // MOE_SUB)
    nsub = jnp.where(used, nsub, 0)
    row0 = jnp.where(used, row0, 0)
    nreal = jnp.where(used, jnp.clip(counts[be] - within * MOE_ROWS, 0, MOE_ROWS), 0)
    i32 = lambda a: a.astype(jnp.int32)
    return i32(pad_start), be, i32(row0), i32(nsub), i32(nreal)


def kernel(x, p, w_in, conv_w, sg_ln_g, sg_ln_b, sg_w, sg_b, w_out, ln1_g, ln1_b, w_rg, b_rg, w_re, b_re,
           w_gate, w_up, w_down, ln2_g, ln2_b, w_pg, b_pg, w_pe):
    depth = w_in.shape[0]
    bsz, seq, d = x.shape
    t = bsz * seq
    alpha = (2 * depth) ** 0.25
    n_groups = w_rg.shape[-1]
    ne = w_re.shape[-1]
    per_group = ne // n_groups
    n_asg = t * TOP_K
    n_rows = ((n_asg + ne * (MOE_SUB - 1) + MOE_SUB - 1) // MOE_SUB) * MOE_SUB

    h = x.reshape(t, d)
    for i in range(depth):
        cdim = conv_w.shape[-1]
        nj = cdim // MIX_TN
        w_in_p = _regroup_cols(w_in[i], 5, nj, MIX_TN)
        y, w_out_p, w_pg_b = _mixer(h, w_in_p, conv_w[i], sg_ln_g[i].reshape(1, -1), sg_ln_b[i].reshape(1, -1),
                                    sg_w[i], sg_b[i][:, :, None], w_out[i], w_pg[i], seq)
        wr = jnp.concatenate([w_rg[i], w_re[i], jnp.zeros((d, LANE - n_groups - ne), F32)], axis=1).astype(BF16)
        br = jnp.concatenate([b_rg[i], b_re[i], jnp.zeros((LANE - n_groups - ne,), F32)]).reshape(1, LANE)
        h1, logits = _outproj(y, w_out_p, h, ln1_g[i].reshape(1, d), ln1_b[i].reshape(1, d), wr, br, alpha)
        pk, cnt = _route(logits, n_groups, per_group)
        pad_start, be, row0, nsub, nreal = _block_table(cnt[0, :ne].astype(jnp.int32), n_asg)
        eid = jnp.concatenate([pk[:, 0], pk[:, 1]]).astype(jnp.int32)
        rank = jnp.concatenate([pk[:, 4], pk[:, 5]]).astype(jnp.int32)
        asg = _rowmap(eid, rank, pad_start, n_rows)
        y2 = _moe(be, row0, nsub, nreal, asg, h1, w_gate[i], w_up[i], w_down[i], n_asg)
        h = _combine_gate(h1, y2, pk, ln2_g[i].reshape(1, d), ln2_b[i].reshape(1, d), w_pg_b,
                          b_pg[i].reshape(1, d), p[i].reshape(t, -1), w_pe[i].astype(BF16), alpha)
    return h.reshape(bsz, seq, d)
```

```python
import functools

import jax
import jax.numpy as jnp
from jax import lax
from jax.experimental import pallas as pl
from jax.experimental.pallas import tpu as pltpu
from jax.experimental.pallas import tpu_sc as plsc

F32 = jnp.float32
BF16 = jnp.bfloat16

LANE = 128
SUBLANE = 8
MXU_COL = 256
SC_CORES = 2
SC_SUBCORES = 16
SC_LANES = 16
VMEM_LIMIT = 56 * 1024 * 1024
MOE_VMEM_LIMIT = 60 * 1024 * 1024

CHUNK = 64
SG_BLOCK = 128
LN_EPS = 1e-5
TOP_K = 2

MIX_TM = 512
MIX_TN = MXU_COL
OUT_TN = 512
ROUTE_TM = 512
MOE_SUB = 128
MOE_ROWS = 768
MOE_TF = MXU_COL
ISSUE_UNROLL = 8
DMA_QUEUES = 2
ROWMAP_CHUNK = 16384
SC_UNROLL = 8
CAST_TK = 4096
CAST_ROWS = 16

_dot = functools.partial(jnp.dot, preferred_element_type=F32)


def _params(*sem, vmem=VMEM_LIMIT):
    return pltpu.CompilerParams(dimension_semantics=sem, vmem_limit_bytes=vmem)


def _layer_norm(z, g, b):
    mu = jnp.mean(z, axis=-1, keepdims=True)
    zc = z - mu
    var = jnp.mean(zc * zc, axis=-1, keepdims=True)
    return zc * lax.rsqrt(var + LN_EPS) * g + b


def _cast_kernel(w_ref, o_ref):
    o_ref[...] = w_ref[...].astype(o_ref.dtype)


def _regroup_cols(w, groups, nj, tn):
    d = w.shape[0]
    tk = min(CAST_TK, d)
    assert d % tk == 0 and w.shape[1] == groups * nj * tn
    return pl.pallas_call(
        _cast_kernel,
        grid=(nj, groups, d // tk),
        in_specs=[pl.BlockSpec((tk, tn), lambda j, s, k: (k, s * nj + j))],
        out_specs=pl.BlockSpec((tk, tn), lambda j, s, k: (k, j * groups + s)),
        out_shape=jax.ShapeDtypeStruct(w.shape, BF16),
        compiler_params=_params("arbitrary", "arbitrary", "arbitrary"),
        name="regroup_cols",
    )(w)


def _mixer_kernel(x_ref, w_ref, cw_ref, lng_ref, lnb_ref, sgw_ref, sgb_ref, wo_ref, wg_ref,
                  y_ref, wo_out_ref, wg_out_ref,
                  xb_ref, pa_ref, pb_ref, gbuf_ref, carry_ref, *, tm, tn, nj, n_steps, tiles_per_seq):
    n = pl.program_id(0)

    @pl.when(n == 0)
    def _():
        pb_ref[...] = jnp.zeros(pb_ref.shape, F32)
        carry_ref[...] = jnp.zeros(carry_ref.shape, F32)

    @pl.when((n % nj == 0) & (n < n_steps))
    def _():
        xb_ref[...] = x_ref[...].astype(BF16)

    m = jnp.maximum(n - 1, 0)
    jp = m % nj
    first = ((m // nj) % tiles_per_seq) == 0

    def mix(proj_ref):
        b, c, h, u, v = (proj_ref[:, k * tn:(k + 1) * tn] for k in range(5))
        g = c * h
        gbuf_ref[0:SUBLANE, :] = jnp.where(first, 0.0, carry_ref[jp])
        gbuf_ref[SUBLANE:, :] = g
        carry_ref[jp] = g[tm - SUBLANE:, :]
        g1 = gbuf_ref[pl.ds(SUBLANE - 1, tm), :]
        g2 = gbuf_ref[pl.ds(SUBLANE - 2, tm), :]
        cw = cw_ref[...]
        conv = cw[0:1, :] * g2 + cw[1:2, :] * g1 + cw[2:3, :] * g
        y_ref[:, 0:tn] = (b * conv).astype(y_ref.dtype)

        pos_i = lax.broadcasted_iota(jnp.int32, (SG_BLOCK, SG_BLOCK), 0)
        pos_j = lax.broadcasted_iota(jnp.int32, (SG_BLOCK, SG_BLOCK), 1)
        mask = (pos_j // CHUNK) <= (pos_i // CHUNK)
        for hh in range(tn // LANE):
            sl = slice(hh * LANE, (hh + 1) * LANE)
            vn = _layer_norm(jax.nn.gelu(v[:, sl]), lng_ref[:, sl], lnb_ref[:, sl]).astype(BF16)
            gu = jax.nn.gelu(u[:, sl])
            ws = jnp.where(mask, sgw_ref[hh], 0.0).astype(BF16)
            bcol = sgb_ref[hh]
            nblk = tm // SG_BLOCK
            vcat = jnp.concatenate([vn[r * SG_BLOCK:(r + 1) * SG_BLOCK, :] for r in range(nblk)], axis=1)
            sg = _dot(ws, vcat) + bcol
            for r in range(nblk):
                rows = slice(r * SG_BLOCK, (r + 1) * SG_BLOCK)
                y_ref[rows, tn + hh * LANE:tn + (hh + 1) * LANE] = (
                    gu[rows, :] * sg[:, r * LANE:(r + 1) * LANE]).astype(y_ref.dtype)

    def step(store_ref, load_ref):
        store_ref[...] = _dot(xb_ref[...], w_ref[...])
        mix(load_ref)
        wo_out_ref[...] = wo_ref[...].astype(BF16)
        wg_out_ref[...] = wg_ref[...].astype(BF16)

    pl.when(n % 2 == 0)(functools.partial(step, pa_ref, pb_ref))
    pl.when(n % 2 == 1)(functools.partial(step, pb_ref, pa_ref))


def _mixer(x2, w_in_p, conv_w, lng, lnb, sg_w, sg_bcol, w_out, w_pg, seq):
    t, d = x2.shape
    cdim = conv_w.shape[1]
    tm, tn = MIX_TM, MIX_TN
    nj = cdim // tn
    assert seq % tm == 0 and t % tm == 0 and cdim % tn == 0 and w_in_p.shape[1] == 5 * cdim
    hp = tn // LANE
    n_steps = (t // tm) * nj
    cur = lambda n: jnp.minimum(n, n_steps - 1)
    prv = lambda n: jnp.maximum(n - 1, 0)
    wrows = w_out.shape[0]
    cb = max(CAST_ROWS, -(-wrows // n_steps // CAST_ROWS) * CAST_ROWS)
    nblk = wrows // cb
    spb = n_steps // nblk
    bpt = tn // cb
    assert w_pg.shape == w_out.shape == (2 * cdim, d) and wrows % cb == 0 and n_steps % nblk == 0 and tn % cb == 0
    blk = lambda n: jnp.minimum(n // spb, nblk - 1)

    def wo_src(n):
        k = blk(n)
        tile, sub = k // bpt, k % bpt
        return ((tile % 2) * nj + tile // 2) * bpt + sub, 0

    return pl.pallas_call(
        functools.partial(_mixer_kernel, tm=tm, tn=tn, nj=nj, n_steps=n_steps, tiles_per_seq=seq // tm),
        grid=(n_steps + 1,),
        in_specs=[
            pl.BlockSpec((tm, d), lambda n: (cur(n) // nj, 0)),
            pl.BlockSpec((d, 5 * tn), lambda n: (0, cur(n) % nj)),
            pl.BlockSpec((conv_w.shape[0], tn), lambda n: (0, prv(n) % nj)),
            pl.BlockSpec((1, tn), lambda n: (0, prv(n) % nj)),
            pl.BlockSpec((1, tn), lambda n: (0, prv(n) % nj)),
            pl.BlockSpec((hp, SG_BLOCK, SG_BLOCK), lambda n: (prv(n) % nj, 0, 0)),
            pl.BlockSpec((hp, SG_BLOCK, 1), lambda n: (prv(n) % nj, 0, 0)),
            pl.BlockSpec((cb, d), wo_src),
            pl.BlockSpec((cb, d), lambda n: (blk(n), 0)),
        ],
        out_specs=[pl.BlockSpec((tm, 2 * tn), lambda n: (prv(n) // nj, prv(n) % nj)),
                   pl.BlockSpec((cb, d), lambda n: (blk(n), 0)),
                   pl.BlockSpec((cb, d), lambda n: (blk(n), 0))],
        out_shape=[jax.ShapeDtypeStruct((t, 2 * cdim), BF16),
                   jax.ShapeDtypeStruct(w_out.shape, BF16),
                   jax.ShapeDtypeStruct(w_pg.shape, BF16)],
        scratch_shapes=[pltpu.VMEM((tm, d), BF16),
                        pltpu.VMEM((tm, 5 * tn), F32),
                        pltpu.VMEM((tm, 5 * tn), F32),
                        pltpu.VMEM((tm + SUBLANE, tn), F32),
                        pltpu.VMEM((nj, SUBLANE, tn), F32)],
        compiler_params=_params("arbitrary"),
        name="mixer",
    )(x2, w_in_p, conv_w, lng, lnb, sg_w, sg_bcol, w_out, w_pg)


def _outproj_kernel(y_ref, w_ref, x_ref, g_ref, b_ref, wr_ref, br_ref, h1_ref, lg_ref, z_ref, *, tn, nj, alpha):
    j = pl.program_id(1)
    z_ref[j] = alpha * x_ref[...] + _dot(y_ref[...], w_ref[...])

    @pl.when(j == nj - 1)
    def _():
        d = nj * tn
        mu = sum(jnp.sum(z_ref[jj], axis=-1, keepdims=True) for jj in range(nj)) / d
        var = sum(jnp.sum(jnp.square(z_ref[jj] - mu), axis=-1, keepdims=True) for jj in range(nj)) / d
        rstd = lax.rsqrt(var + LN_EPS)
        logits = br_ref[...]
        for jj in range(nj):
            cols = slice(jj * tn, (jj + 1) * tn)
            h = (z_ref[jj] - mu) * rstd * g_ref[:, cols] + b_ref[:, cols]
            h1_ref[:, cols] = h
            logits = logits + _dot(h.astype(BF16), wr_ref[cols, :])
        lg_ref[...] = logits


def _outproj(y, w_out_p, x2, g, b, wr, br, alpha):
    t, d = x2.shape
    kdim = y.shape[1]
    tm, tn = MIX_TM, OUT_TN
    nj = d // tn
    assert t % tm == 0 and d % tn == 0 and w_out_p.shape[0] == kdim
    return pl.pallas_call(
        functools.partial(_outproj_kernel, tn=tn, nj=nj, alpha=alpha),
        grid=(t // tm, nj),
        in_specs=[
            pl.BlockSpec((tm, kdim), lambda i, j: (i, 0)),
            pl.BlockSpec((kdim, tn), lambda i, j: (0, j)),
            pl.BlockSpec((tm, tn), lambda i, j: (i, j)),
            pl.BlockSpec((1, d), lambda i, j: (0, 0)),
            pl.BlockSpec((1, d), lambda i, j: (0, 0)),
            pl.BlockSpec((d, LANE), lambda i, j: (0, 0)),
            pl.BlockSpec((1, LANE), lambda i, j: (0, 0)),
        ],
        out_specs=[pl.BlockSpec((tm, d), lambda i, j: (i, 0)),
                   pl.BlockSpec((tm, LANE), lambda i, j: (i, 0))],
        out_shape=[jax.ShapeDtypeStruct((t, d), F32), jax.ShapeDtypeStruct((t, LANE), F32)],
        scratch_shapes=[pltpu.VMEM((nj, tm, tn), F32)],
        compiler_params=_params("arbitrary", "arbitrary"),
        name="outproj",
    )(y, w_out_p, x2, g, b, wr, br)


def _route_kernel(lg_ref, pk_ref, cnt_ref, run_ref, *, n_groups, per_group):
    i = pl.program_id(0)

    @pl.when(i == 0)
    def _():
        run_ref[...] = jnp.zeros_like(run_ref)

    l = lg_ref[...]
    tm = l.shape[0]
    lane = lax.broadcasted_iota(jnp.int32, l.shape, 1)
    neg = jnp.float32(-jnp.inf)
    gmask = lane < n_groups
    gl = jnp.where(gmask, l, neg)
    gmax = jnp.max(gl, axis=-1, keepdims=True)
    gi = jnp.min(jnp.where(gl == gmax, lane, LANE), axis=-1, keepdims=True)
    gp = 1.0 / jnp.sum(jnp.where(gmask, jnp.exp(l - gmax), 0.0), axis=-1, keepdims=True)
    elane = lane - n_groups
    emask = (elane >= 0) & (elane // per_group == gi)
    el = jnp.where(emask, l, neg)
    m1 = jnp.max(el, axis=-1, keepdims=True)
    i1 = jnp.min(jnp.where(el == m1, lane, LANE), axis=-1, keepdims=True)
    el2 = jnp.where(lane == i1, neg, el)
    m2 = jnp.max(el2, axis=-1, keepdims=True)
    i2 = jnp.min(jnp.where(el2 == m2, lane, LANE), axis=-1, keepdims=True)
    t2 = jnp.exp(m2 - m1)
    w0 = gp / (1.0 + t2)
    w1 = gp * t2 / (1.0 + t2)
    e0 = i1 - n_groups
    e1 = i2 - n_groups
    oh0 = lane == e0
    oh1 = lane == e1
    oh = (oh0 | oh1).astype(BF16)
    row = lax.broadcasted_iota(jnp.int32, (tm, tm), 0)
    col = lax.broadcasted_iota(jnp.int32, (tm, tm), 1)
    tri = (row > col).astype(BF16)
    before = run_ref[...] + _dot(tri, oh)
    r0 = jnp.sum(jnp.where(oh0, before, 0.0), axis=-1, keepdims=True)
    r1 = jnp.sum(jnp.where(oh1, before, 0.0), axis=-1, keepdims=True)
    total = run_ref[...] + jnp.sum(oh.astype(F32), axis=0, keepdims=True)
    run_ref[...] = total
    cnt_ref[...] = total
    vals = (e0.astype(F32), e1.astype(F32), w0, w1, r0, r1)
    pk = jnp.zeros(l.shape, F32)
    for k, val in enumerate(vals):
        pk = jnp.where(lane == k, val, pk)
    pk_ref[...] = pk


def _route(logits, n_groups, per_group):
    t = logits.shape[0]
    tm = ROUTE_TM
    assert t % tm == 0 and n_groups * (1 + per_group) <= LANE
    return pl.pallas_call(
        functools.partial(_route_kernel, n_groups=n_groups, per_group=per_group),
        grid=(t // tm,),
        in_specs=[pl.BlockSpec((tm, LANE), lambda i: (i, 0))],
        out_specs=[pl.BlockSpec((tm, LANE), lambda i: (i, 0)),
                   pl.BlockSpec((1, LANE), lambda i: (0, 0))],
        out_shape=[jax.ShapeDtypeStruct((t, LANE), F32), jax.ShapeDtypeStruct((1, LANE), F32)],
        scratch_shapes=[pltpu.VMEM((1, LANE), F32)],
        compiler_params=_params("arbitrary"),
        name="route",
    )(logits)


def _rowmap_kernel(eid_hbm, rank_hbm, start_hbm, asg_hbm, eid_v, rank_v, start_v, asg_v, *, n_rows, n_asg, chunk):
    on_first = (lax.axis_index("c") == 0) & (lax.axis_index("s") == 0)

    @pl.when(on_first)
    def _():
        pltpu.sync_copy(start_hbm, start_v)
        zeros = jnp.zeros((SC_LANES,), jnp.int32)

        group = SC_UNROLL * SC_LANES

        @pl.loop(0, n_rows, step=group)
        def _(r):
            for u in range(SC_UNROLL):
                asg_v[pl.ds(r + u * SC_LANES, SC_LANES)] = zeros

        lanes = lax.iota(jnp.int32, SC_LANES)

        @pl.loop(0, n_asg, step=chunk)
        def _(a0):
            pltpu.sync_copy(eid_hbm.at[pl.ds(a0, chunk)], eid_v)
            pltpu.sync_copy(rank_hbm.at[pl.ds(a0, chunk)], rank_v)

            @pl.loop(0, chunk, step=group)
            def _(k0):
                for u in range(SC_UNROLL):
                    k = k0 + u * SC_LANES
                    dest = plsc.load_gather(start_v, [eid_v[pl.ds(k, SC_LANES)]]) + rank_v[pl.ds(k, SC_LANES)]
                    plsc.store_scatter(asg_v, [dest], lanes + (a0 + k))

        pltpu.sync_copy(asg_v, asg_hbm)


def _rowmap(eid, rank, start, n_rows):
    n_asg = eid.shape[0]
    chunk = min(ROWMAP_CHUNK, n_asg)
    group = SC_UNROLL * SC_LANES
    assert n_asg % chunk == 0 and chunk % group == 0 and n_rows % group == 0 and start.shape[0] % SC_LANES == 0
    mesh = plsc.VectorSubcoreMesh(core_axis_name="c", subcore_axis_name="s",
                                  num_cores=SC_CORES, num_subcores=SC_SUBCORES)
    i32 = jnp.int32
    return pl.kernel(
        functools.partial(_rowmap_kernel, n_rows=n_rows, n_asg=n_asg, chunk=chunk),
        out_type=jax.ShapeDtypeStruct((n_rows,), i32),
        mesh=mesh,
        scratch_types=[pltpu.VMEM((chunk,), i32), pltpu.VMEM((chunk,), i32),
                       pltpu.VMEM(start.shape, i32), pltpu.VMEM((n_rows,), i32)],
        compiler_params=pltpu.CompilerParams(needs_layout_passes=False),
        name="rowmap",
    )(eid, rank, start)


def _moe_kernel(be_ref, row0_ref, nsub_ref, nreal_ref, asg_ref, h1_hbm, wg_ref, wu_ref, wd_ref, y_hbm,
                xg_ref, xb_ref, yacc_ref, gsem, ssem, *, nb, nf, sub, tok_mask):
    b = pl.program_id(0)
    f = pl.program_id(1)
    nsub = nsub_ref[b]
    max_tiles = xb_ref.shape[0] // sub

    def gather_row(base, r, slot=0):
        del slot
        tok = asg_ref[base + r] & tok_mask
        pltpu.make_async_copy(h1_hbm.at[pl.ds(tok, 1)], xg_ref.at[pl.ds(r, 1)], gsem).start()

    def scatter_row(base, r, slot=0):
        dst = asg_ref[base + r]
        pltpu.make_async_copy(yacc_ref.at[pl.ds(r, 1)], y_hbm.at[pl.ds(dst, 1)], ssem).start(
            priority=slot % DMA_QUEUES)

    def issue_tiles(row_fn, base, n_tiles):
        for s in range(max_tiles):
            @pl.when(s < n_tiles)
            def _(s=s):
                for r in range(s * sub, (s + 1) * sub):
                    row_fn(base, r, r)

    def issue_range(row_fn, base, lo, hi):
        groups = (hi - lo) // ISSUE_UNROLL

        def body(q, c):
            for k in range(ISSUE_UNROLL):
                row_fn(base, lo + q * ISSUE_UNROLL + k, k)
            return c

        lax.fori_loop(0, groups, body, 0)

        def tail(r, c):
            row_fn(base, r)
            return c

        lax.fori_loop(lo + groups * ISSUE_UNROLL, hi, tail, 0)

    def wait_rows(n, tile_copy, row_copy):
        def tile(s, c):
            tile_copy.wait()
            return c

        lax.fori_loop(0, n // sub, tile, 0)

        def row(r, c):
            row_copy.wait()
            return c

        lax.fori_loop((n // sub) * sub, n, row, 0)

    def gather_wait(n):
        wait_rows(n, pltpu.make_async_copy(h1_hbm.at[pl.ds(0, sub)], xg_ref.at[pl.ds(0, sub)], gsem),
                  pltpu.make_async_copy(h1_hbm.at[pl.ds(0, 1)], xg_ref.at[pl.ds(0, 1)], gsem))

    def scatter_wait(blk):
        wait_rows(nreal_ref[blk], pltpu.make_async_copy(yacc_ref.at[pl.ds(0, sub)], y_hbm.at[pl.ds(0, sub)], ssem),
                  pltpu.make_async_copy(yacc_ref.at[pl.ds(0, 1)], y_hbm.at[pl.ds(0, 1)], ssem))

    @pl.when((f == 0) & (b == 0))
    def _():
        yacc_ref[...] = jnp.zeros(yacc_ref.shape, F32)

    @pl.when((f == 0) & (b > 0) & (nsub == 0))
    def _():
        scatter_wait(jnp.maximum(b - 1, 0))

    @pl.when(nsub > 0)
    def _():
        @pl.when(f == 0)
        def _():
            @pl.when(b == 0)
            def _():
                issue_range(gather_row, row0_ref[0], 0, nsub * sub)

            gather_wait(nsub * sub)

            def cast(s, c):
                rows = pl.ds(pl.multiple_of(s * sub, sub), sub)
                xb_ref[rows, :] = xg_ref[rows, :].astype(BF16)
                return c

            lax.fori_loop(0, nsub, cast, 0)

            @pl.when(b + 1 < nb)
            def _():
                nxt = jnp.minimum(b + 1, nb - 1)
                issue_tiles(gather_row, row0_ref[nxt], nsub_ref[nxt])

        def compute(m):
            rows = slice(0, m * sub)
            xs = xb_ref[rows, :]
            hb = (jax.nn.silu(_dot(xs, wg_ref[...])) * _dot(xs, wu_ref[...])).astype(BF16)

            @pl.when((f == 0) & (b > 0))
            def _():
                scatter_wait(jnp.maximum(b - 1, 0))

            yp = _dot(hb, wd_ref[...])
            yacc_ref[rows, :] = jnp.where(f == 0, yp, yacc_ref[rows, :] + yp)

        for m in range(1, xb_ref.shape[0] // sub + 1):
            pl.when(nsub == m)(functools.partial(compute, m))

        @pl.when(f == nf - 1)
        def _():
            full = nreal_ref[b] // sub
            issue_tiles(scatter_row, row0_ref[b], full)
            issue_range(scatter_row, row0_ref[b], full * sub, nreal_ref[b])

            @pl.when(b == nb - 1)
            def _():
                scatter_wait(b)


def _moe(be, row0, nsub, nreal, asg, h1, w_gate, w_up, w_down, n_out_rows):
    t, d = h1.shape
    ne, _, de = w_gate.shape
    nb = be.shape[0]
    nf = de // MOE_TF
    assert de % MOE_TF == 0 and t & (t - 1) == 0
    last = nf - 1
    fsel = lambda f, n: jnp.where(n > 0, f, last)
    grid_spec = pltpu.PrefetchScalarGridSpec(
        num_scalar_prefetch=5,
        grid=(nb, nf),
        in_specs=[
            pl.BlockSpec(memory_space=pl.ANY),
            pl.BlockSpec((None, d, MOE_TF), lambda b, f, be, r0, ns, nr, asg: (be[b], 0, fsel(f, ns[b]))),
            pl.BlockSpec((None, d, MOE_TF), lambda b, f, be, r0, ns, nr, asg: (be[b], 0, fsel(f, ns[b]))),
            pl.BlockSpec((None, MOE_TF, d), lambda b, f, be, r0, ns, nr, asg: (be[b], fsel(f, ns[b]), 0)),
        ],
        out_specs=pl.BlockSpec(memory_space=pl.ANY),
        scratch_shapes=[pltpu.VMEM((MOE_ROWS, d), F32),
                        pltpu.VMEM((MOE_ROWS, d), BF16),
                        pltpu.VMEM((MOE_ROWS, d), F32),
                        pltpu.SemaphoreType.DMA,
                        pltpu.SemaphoreType.DMA],
    )
    return pl.pallas_call(
        functools.partial(_moe_kernel, nb=nb, nf=nf, sub=MOE_SUB, tok_mask=t - 1),
        grid_spec=grid_spec,
        out_shape=jax.ShapeDtypeStruct((n_out_rows, d), F32),
        compiler_params=_params("arbitrary", "arbitrary", vmem=MOE_VMEM_LIMIT),
        name="moe",
    )(be, row0, nsub, nreal, asg, h1, w_gate, w_up, w_down)


def _combine_gate_kernel(h1_ref, y0_ref, y1_ref, pk_ref, g_ref, b_ref, wpg_ref, bpg_ref, p_ref, wpe_ref,
                         o_ref, za_ref, zb_ref, ha_ref, hb_ref, *, nj, tn, alpha):
    n = pl.program_id(0)
    j = n % nj
    parity = (n // nj) % 2

    @pl.when(n == 0)
    def _():
        for ref in (za_ref, zb_ref, ha_ref, hb_ref):
            ref[...] = jnp.zeros(ref.shape, ref.dtype)

    def layer_norm_tile(z_ref, h_ref):
        d = nj * tn
        mu = sum(jnp.sum(z_ref[jj], axis=-1, keepdims=True) for jj in range(nj)) / d
        var = sum(jnp.sum(jnp.square(z_ref[jj] - mu), axis=-1, keepdims=True) for jj in range(nj)) / d
        rstd = lax.rsqrt(var + LN_EPS)
        for jj in range(nj):
            cols = slice(jj * tn, (jj + 1) * tn)
            h = (z_ref[jj] - mu) * rstd * g_ref[:, cols] + b_ref[:, cols]
            z_ref[jj] = h
            h_ref[:, cols] = h.astype(BF16)

    def step(zw_ref, hw_ref, zo_ref, ho_ref, first):
        pk = pk_ref[...]
        ffn = pk[:, 2:3] * y0_ref[...] + pk[:, 3:4] * y1_ref[...]
        z = alpha * h1_ref[...] + ffn
        if first:
            zw_ref[0] = z
            layer_norm_tile(zo_ref, ho_ref)
            zg, hg = zw_ref[nj - 1], hw_ref[...]
        else:
            zw_ref[j] = z
            zg, hg = zo_ref[j - 1], ho_ref[...]
        gate = jax.nn.sigmoid(_dot(hg, wpg_ref[...]) + bpg_ref[...])
        ple = _dot(p_ref[...].astype(BF16), wpe_ref[...])
        o_ref[...] = zg + gate * ple

    slots = ((za_ref, ha_ref, zb_ref, hb_ref), (zb_ref, hb_ref, za_ref, ha_ref))
    for par in range(2):
        for first in (True, False):
            cond = (parity == par) & ((j == 0) if first else (j > 0))
            pl.when(cond)(functools.partial(step, *slots[par], first))


def _combine_gate(h1, y2, pk, g, b, w_pg_b, b_pg, p2, w_pe_b, alpha):
    t, d = h1.shape
    pd = p2.shape[1]
    tm, tn = MIX_TM, OUT_TN
    assert t % tm == 0 and d % tn == 0
    ni, nj = t // tm, d // tn
    n_comb = ni * nj
    cmb = lambda n: jnp.minimum(n, n_comb - 1)
    gat = lambda n: jnp.clip(n - nj - 1, 0, n_comb - 1)
    return pl.pallas_call(
        functools.partial(_combine_gate_kernel, nj=nj, tn=tn, alpha=alpha),
        grid=(n_comb + nj + 1,),
        in_specs=[
            pl.BlockSpec((tm, tn), lambda n: (cmb(n) // nj, cmb(n) % nj)),
            pl.BlockSpec((tm, tn), lambda n: (cmb(n) // nj, cmb(n) % nj)),
            pl.BlockSpec((tm, tn), lambda n: (cmb(n) // nj + ni, cmb(n) % nj)),
            pl.BlockSpec((tm, LANE), lambda n: (cmb(n) // nj, 0)),
            pl.BlockSpec((1, d), lambda n: (0, 0)),
            pl.BlockSpec((1, d), lambda n: (0, 0)),
            pl.BlockSpec((d, tn), lambda n: (0, gat(n) % nj)),
            pl.BlockSpec((1, tn), lambda n: (0, gat(n) % nj)),
            pl.BlockSpec((tm, pd), lambda n: (gat(n) // nj, 0)),
            pl.BlockSpec((pd, tn), lambda n: (0, gat(n) % nj)),
        ],
        out_specs=pl.BlockSpec((tm, tn), lambda n: (gat(n) // nj, gat(n) % nj)),
        out_shape=jax.ShapeDtypeStruct((t, d), F32),
        scratch_shapes=[pltpu.VMEM((nj, tm, tn), F32), pltpu.VMEM((nj, tm, tn), F32),
                        pltpu.VMEM((tm, d), BF16), pltpu.VMEM((tm, d), BF16)],
        compiler_params=_params("arbitrary"),
        name="combine_gate",
    )(h1, y2, y2, pk, g, b, w_pg_b, b_pg, p2, w_pe_b)


def _block_table(counts, n_asg):
    ne = counts.shape[0]
    padded = ((counts + MOE_SUB - 1) // MOE_SUB) * MOE_SUB
    pad_end = jnp.cumsum(padded)
    pad_start = pad_end - padded
    nblk = (padded + MOE_ROWS - 1) // MOE_ROWS
    blk_end = jnp.cumsum(nblk)
    blk_start = blk_end - nblk
    nb = ne + -(-n_asg // MOE_ROWS)
    bidx = jnp.arange(nb, dtype=jnp.int32)
    used = bidx < blk_end[-1]
    be = jnp.minimum(jnp.searchsorted(blk_end, jnp.minimum(bidx, blk_end[-1] - 1), side="right"), ne - 1)
    be = be.astype(jnp.int32)
    within = bidx - blk_start[be]
    row0 = pad_start[be] + within * MOE_ROWS
    nsub = jnp.clip((padded[be] - within * MOE_ROWS) // MOE_SUB, 0, MOE_ROWS // MOE_SUB)
    nsub = jnp.where(used, nsub, 0)
    row0 = jnp.where(used, row0, 0)
    nreal = jnp.where(used, jnp.clip(counts[be] - within * MOE_ROWS, 0, MOE_ROWS), 0)
    i32 = lambda a: a.astype(jnp.int32)
    return i32(pad_start), be, i32(row0), i32(nsub), i32(nreal)


def kernel(x, p, w_in, conv_w, sg_ln_g, sg_ln_b, sg_w, sg_b, w_out, ln1_g, ln1_b, w_rg, b_rg, w_re, b_re,
           w_gate, w_up, w_down, ln2_g, ln2_b, w_pg, b_pg, w_pe):
    depth = w_in.shape[0]
    bsz, seq, d = x.shape
    t = bsz * seq
    alpha = (2 * depth) ** 0.25
    n_groups = w_rg.shape[-1]
    ne = w_re.shape[-1]
    per_group = ne // n_groups
    n_asg = t * TOP_K
    n_rows = ((n_asg + ne * (MOE_SUB - 1) + MOE_SUB - 1) // MOE_SUB) * MOE_SUB

    h = x.reshape(t, d)
    for i in range(depth):
        cdim = conv_w.shape[-1]
        nj = cdim // MIX_TN
        w_in_p = _regroup_cols(w_in[i], 5, nj, MIX_TN)
        y, w_out_p, w_pg_b = _mixer(h, w_in_p, conv_w[i], sg_ln_g[i].reshape(1, -1), sg_ln_b[i].reshape(1, -1),
                                    sg_w[i], sg_b[i][:, :, None], w_out[i], w_pg[i], seq)
        wr = jnp.concatenate([w_rg[i], w_re[i], jnp.zeros((d, LANE - n_groups - ne), F32)], axis=1).astype(BF16)
        br = jnp.concatenate([b_rg[i], b_re[i], jnp.zeros((LANE - n_groups - ne,), F32)]).reshape(1, LANE)
        h1, logits = _outproj(y, w_out_p, h, ln1_g[i].reshape(1, d), ln1_b[i].reshape(1, d), wr, br, alpha)
        pk, cnt = _route(logits, n_groups, per_group)
        pad_start, be, row0, nsub, nreal = _block_table(cnt[0, :ne].astype(jnp.int32), n_asg)
        eid = jnp.concatenate([pk[:, 0], pk[:, 1]]).astype(jnp.int32)
        rank = jnp.concatenate([pk[:, 4], pk[:, 5]]).astype(jnp.int32)
        asg = _rowmap(eid, rank, pad_start, n_rows)
        y2 = _moe(be, row0, nsub, nreal, asg, h1, w_gate[i], w_up[i], w_down[i], n_asg)
        h = _combine_gate(h1, y2, pk, ln2_g[i].reshape(1, d), ln2_b[i].reshape(1, d), w_pg_b,
                          b_pg[i].reshape(1, d), p[i].reshape(t, -1), w_pe[i].astype(BF16), alpha)
    return h.reshape(bsz, seq, d)
```
